```python
import math
import jax, jax.numpy as jnp
from jax import lax
import numpy as np

D_MODEL = 4096
BATCH = 4
SEQ = 4096
DEPTH = 1

D_MIX = D_MODEL
D_ATTN = D_MIX // 2
D_SSM = D_MIX - D_ATTN
HEAD_DIM = 64
N_Q_HEADS = D_ATTN // HEAD_DIM
N_KV_HEADS = max(1, N_Q_HEADS // 8)
Q_PER_KV = N_Q_HEADS // N_KV_HEADS
D_KV = N_KV_HEADS * HEAD_DIM
WINDOW = 128
BLOCK = WINDOW
ROPE_THETA = 10000.0
SSM_GROUP = 16
N_SSM_GROUPS = D_SSM // SSM_GROUP
STATE = 64
DT_MIN = 1e-3
DT_MAX = 1e-1
D_FF = 4 * D_MODEL
N_MOD = 6
EPS = 1e-6
D_IN = D_ATTN + 2 * D_KV + D_SSM

kernel_name = "hymba_s5_swa_sink_adaln_block"


def rmsnorm(x, g):
    xf = x.astype(jnp.float32)
    y = xf * lax.rsqrt(jnp.mean(xf * xf, axis=-1, keepdims=True) + EPS)
    return (y * g.astype(jnp.float32)).astype(x.dtype)


def rope(x):
    s = x.shape[1]
    half = x.shape[-1] // 2
    inv_freq = ROPE_THETA ** (-jnp.arange(half, dtype=jnp.float32) / half)
    ang = jnp.arange(s, dtype=jnp.float32)[:, None] * inv_freq[None, :]
    cos = jnp.cos(ang)[None, :, None, :]
    sin = jnp.sin(ang)[None, :, None, :]
    xf = x.astype(jnp.float32)
    x1, x2 = xf[..., :half], xf[..., half:]
    out = jnp.concatenate([x1 * cos - x2 * sin, x2 * cos + x1 * sin], axis=-1)
    return out.astype(x.dtype)


def sliding_window_attention(q, k, v, sinks):
    b, s = q.shape[0], q.shape[1]
    nb = s // BLOCK
    qb = q.reshape(b, nb, BLOCK, N_KV_HEADS, Q_PER_KV, HEAD_DIM).astype(jnp.float32)
    kb = k.reshape(b, nb, BLOCK, N_KV_HEADS, HEAD_DIM).astype(jnp.float32)
    vb = v.reshape(b, nb, BLOCK, N_KV_HEADS, HEAD_DIM).astype(jnp.float32)
    kk = jnp.concatenate([jnp.concatenate([jnp.zeros_like(kb[:, :1]), kb[:, :-1]], axis=1), kb], axis=2)
    vv = jnp.concatenate([jnp.concatenate([jnp.zeros_like(vb[:, :1]), vb[:, :-1]], axis=1), vb], axis=2)
    scores = jnp.einsum('bnqhgd,bnkhd->bnhgqk', qb, kk) * (HEAD_DIM ** -0.5)
    qi = jnp.arange(BLOCK)[:, None] + BLOCK
    kj = jnp.arange(2 * BLOCK)[None, :]
    rel = qi - kj
    band = (rel >= 0) & (rel < WINDOW)
    key_ok = (jnp.arange(nb)[:, None] > 0) | (jnp.arange(2 * BLOCK)[None, :] >= BLOCK)
    mask = band[None, :, :] & key_ok[:, None, :]
    scores = jnp.where(mask[None, :, None, None], scores, jnp.float32(-1e30))
    sink = sinks.astype(jnp.float32).reshape(N_KV_HEADS, Q_PER_KV)[None, None, :, :, None, None]
    m = jnp.maximum(jnp.max(scores, axis=-1, keepdims=True), sink)
    p = jnp.exp(scores - m)
    probs = p / (jnp.sum(p, axis=-1, keepdims=True) + jnp.exp(sink - m))
    out = jnp.einsum('bnhgqk,bnkhd->bnqhgd', probs, vv)
    return out.reshape(b, s, N_Q_HEADS * HEAD_DIM).astype(q.dtype)


def _scan_op(e1, e2):
    a1, b1 = e1
    a2, b2 = e2
    return a2 * a1, a2 * b1 + b2


def s5_mixer(u, lam_re, lam_im, log_step, b_re, b_im, c_re, c_im, d_skip, w_glu, b_glu):
    bsz, s = u.shape[0], u.shape[1]
    f32 = jnp.float32
    uf = u.astype(f32).reshape(bsz, s, N_SSM_GROUPS, SSM_GROUP)
    step = jnp.exp(log_step.astype(f32))[:, None]
    lam = lax.complex(lam_re.astype(f32), lam_im.astype(f32))
    lam_bar = jnp.exp(lam * step)
    coef = (lam_bar - 1.0) / lam
    b_bar = coef[..., None] * lax.complex(b_re.astype(f32), b_im.astype(f32))
    bu = jnp.einsum('bsgh,gph->sbgp', uf.astype(jnp.complex64), b_bar)
    a = jnp.broadcast_to(lam_bar[None, None], (s, 1, N_SSM_GROUPS, STATE))
    _, states = lax.associative_scan(_scan_op, (a, bu), axis=0)
    c_mat = lax.complex(c_re.astype(f32), c_im.astype(f32))
    y = jnp.real(jnp.einsum('sbgp,ghp->bsgh', states, c_mat))
    y = y + d_skip.astype(f32).reshape(N_SSM_GROUPS, SSM_GROUP) * uf
    y = jax.nn.gelu(y.reshape(bsz, s, D_SSM), approximate=False)
    out = y * jax.nn.sigmoid(y @ w_glu.astype(f32) + b_glu.astype(f32))
    return out.astype(u.dtype)


def setup_inputs(seed: int = 0) -> dict:
    key = jax.random.key(seed)
    ks = jax.random.split(key, 24)
    f32 = jnp.float32
    nrm = lambda k, shape, sc: jax.random.normal(k, shape, f32) * sc
    inputs = {
        "x": nrm(ks[0], (BATCH, SEQ, D_MODEL), 1.0),
        "c": nrm(ks[1], (BATCH, D_MODEL), 1.0),
        "w_ada": nrm(ks[2], (DEPTH, D_MODEL, N_MOD * D_MODEL), 0.5 * D_MODEL ** -0.5),
        "b_ada": nrm(ks[3], (DEPTH, N_MOD * D_MODEL), 0.01),
        "norm1_g": 1.0 + nrm(ks[4], (DEPTH, D_MODEL), 0.02),
        "w_in": nrm(ks[5], (DEPTH, D_MODEL, D_IN), D_MODEL ** -0.5),
        "sinks": nrm(ks[6], (DEPTH, N_Q_HEADS), 0.5),
        "ssm_lam_re": -0.5 + nrm(ks[7], (DEPTH, N_SSM_GROUPS, STATE), 0.01),
        "ssm_lam_im": jnp.pi * jnp.arange(STATE, dtype=f32)[None, None, :] + nrm(ks[8], (DEPTH, N_SSM_GROUPS, STATE), 0.01),
        "ssm_log_step": jax.random.uniform(ks[9], (DEPTH, N_SSM_GROUPS), f32, math.log(DT_MIN), math.log(DT_MAX)),
        "ssm_b_re": nrm(ks[10], (DEPTH, N_SSM_GROUPS, STATE, SSM_GROUP), (2 * SSM_GROUP) ** -0.5),
        "ssm_b_im": nrm(ks[11], (DEPTH, N_SSM_GROUPS, STATE, SSM_GROUP), (2 * SSM_GROUP) ** -0.5),
        "ssm_c_re": nrm(ks[12], (DEPTH, N_SSM_GROUPS, SSM_GROUP, STATE), (2 * STATE) ** -0.5),
        "ssm_c_im": nrm(ks[13], (DEPTH, N_SSM_GROUPS, SSM_GROUP, STATE), (2 * STATE) ** -0.5),
        "ssm_d": nrm(ks[14], (DEPTH, D_SSM), 1.0),
        "w_glu": nrm(ks[15], (DEPTH, D_SSM, D_SSM), D_SSM ** -0.5),
        "b_glu": nrm(ks[16], (DEPTH, D_SSM), 0.01),
        "attn_out_g": 1.0 + nrm(ks[17], (DEPTH, D_ATTN), 0.02),
        "ssm_out_g": 1.0 + nrm(ks[18], (DEPTH, D_SSM), 0.02),
        "w_out": nrm(ks[19], (DEPTH, D_MIX, D_MODEL), D_MIX ** -0.5),
        "norm2_g": 1.0 + nrm(ks[20], (DEPTH, D_MODEL), 0.02),
        "w_ff1": nrm(ks[21], (DEPTH, D_MODEL, D_FF), D_MODEL ** -0.5),
        "w_ff2": nrm(ks[22], (DEPTH, D_FF, D_MODEL), D_FF ** -0.5),
        "final_g": 1.0 + nrm(ks[23], (D_MODEL,), 0.02),
    }
    return inputs


def reference(x, c, w_ada, b_ada, norm1_g, w_in, sinks, ssm_lam_re, ssm_lam_im, ssm_log_step,
              ssm_b_re, ssm_b_im, ssm_c_re, ssm_c_im, ssm_d, w_glu, b_glu, attn_out_g, ssm_out_g,
              w_out, norm2_g, w_ff1, w_ff2, final_g):
    bsz, s, _ = x.shape
    c_act = jax.nn.silu(c.astype(jnp.float32))
    for l in range(DEPTH):
        mod = (c_act @ w_ada[l].astype(jnp.float32) + b_ada[l].astype(jnp.float32)).astype(x.dtype)
        shift1, scale1, gate1, shift2, scale2, gate2 = [m[:, None, :] for m in jnp.split(mod, N_MOD, axis=-1)]

        h = rmsnorm(x, norm1_g[l]) * (1.0 + scale1) + shift1
        proj = h @ w_in[l]
        q = proj[..., :D_ATTN].reshape(bsz, s, N_Q_HEADS, HEAD_DIM)
        k = proj[..., D_ATTN:D_ATTN + D_KV].reshape(bsz, s, N_KV_HEADS, HEAD_DIM)
        v = proj[..., D_ATTN + D_KV:D_ATTN + 2 * D_KV].reshape(bsz, s, N_KV_HEADS, HEAD_DIM)
        u = proj[..., D_ATTN + 2 * D_KV:]
        attn = sliding_window_attention(rope(q), rope(k), v, sinks[l])
        ssm = s5_mixer(u, ssm_lam_re[l], ssm_lam_im[l], ssm_log_step[l], ssm_b_re[l], ssm_b_im[l],
                       ssm_c_re[l], ssm_c_im[l], ssm_d[l], w_glu[l], b_glu[l])
        mixed = jnp.concatenate([rmsnorm(attn, attn_out_g[l]), rmsnorm(ssm, ssm_out_g[l])], axis=-1)
        x = x + gate1 * (mixed @ w_out[l])

        h2 = rmsnorm(x, norm2_g[l]) * (1.0 + scale2) + shift2
        ff = jnp.square(jax.nn.relu(h2 @ w_ff1[l])) @ w_ff2[l]
        x = x + gate2 * ff
    return rmsnorm(x, final_g)
```

```python
import functools
import math

import jax
import jax.numpy as jnp
from jax import lax
from jax.experimental import pallas as pl
from jax.experimental.pallas import tpu as pltpu

D_MODEL = 4096
D_ATTN = 2048
D_SSM = 2048
HEAD_DIM = 64
N_Q_HEADS = 32
N_KV_HEADS = 4
Q_PER_KV = 8
D_KV = 256
WINDOW = 128
ROPE_THETA = 10000.0
SSM_GROUP = 16
N_SSM_GROUPS = 128
STATE = 64
D_FF = 4 * D_MODEL
N_MOD = 6
EPS = 1e-6

V7X_LANES = 128
V7X_SUBLANES = 8
V7X_VMEM_BYTES = 64 * 1024 * 1024

CHUNK = 16
N_SEG = 2
CHUNK_COLS = CHUNK * SSM_GROUP
PAIR_COLS = 2 * CHUNK_COLS
PAIR_STATE = 2 * STATE

BF16 = jnp.bfloat16
F32 = jnp.float32


def _vmem_limit(nbytes):
    return int(min(nbytes + 8 * 1024 * 1024, V7X_VMEM_BYTES - 4 * 1024 * 1024))


def _rms_scale(xf, width):
    return lax.rsqrt(jnp.sum(xf * xf, axis=-1, keepdims=True) * (1.0 / width) + EPS)


def _adaln_kernel(c_ref, w_ref, b_ref, o_ref):
    c = c_ref[...]
    ca = (c * jax.nn.sigmoid(c)).astype(BF16)
    acc = jnp.dot(ca, w_ref[...].astype(BF16), preferred_element_type=F32)
    o_ref[...] = acc + b_ref[...]


def _adaln(c_pad, w_ada, b_ada, tn=1024):
    m, d = c_pad.shape
    n = w_ada.shape[1]
    return pl.pallas_call(
        _adaln_kernel,
        grid=(n // tn,),
        in_specs=[
            pl.BlockSpec((m, d), lambda j: (0, 0)),
            pl.BlockSpec((d, tn), lambda j: (0, j)),
            pl.BlockSpec((1, tn), lambda j: (0, j)),
        ],
        out_specs=pl.BlockSpec((m, tn), lambda j: (0, j)),
        out_shape=jax.ShapeDtypeStruct((m, n), F32),
        compiler_params=pltpu.CompilerParams(
            dimension_semantics=("arbitrary",),
            vmem_limit_bytes=_vmem_limit(2 * d * tn * 4 + d * tn * 2)),
        name="adaln",
    )(c_pad, w_ada, b_ada)


def _rope(acc, cos, sin_signed):
    width = acc.shape[1]
    lane = lax.broadcasted_iota(jnp.int32, acc.shape, 1)
    first_half = (lane % HEAD_DIM) < (HEAD_DIM // 2)
    partner = jnp.where(first_half,
                        pltpu.roll(acc, width - HEAD_DIM // 2, 1),
                        pltpu.roll(acc, HEAD_DIM // 2, 1))
    reps = width // cos.shape[1]
    cos_w = jnp.concatenate([cos] * reps, axis=1)
    sin_w = jnp.concatenate([sin_signed] * reps, axis=1)
    return acc * cos_w + partner * sin_w


def _in_proj_kernel(x_ref, mod_ref, g_ref, w_ref, cos_ref, sin_ref,
                    q_ref, kv_ref, u_ref, h_scr, *, n_q_blocks):
    j = pl.program_id(1)

    @pl.when(j == 0)
    def _():
        xf = x_ref[...]
        y = xf * _rms_scale(xf, D_MODEL) * g_ref[...]
        shift = mod_ref[0, 0:1, :]
        scale = mod_ref[0, 1:2, :]
        h_scr[...] = (y * (1.0 + scale) + shift).astype(BF16)

    acc = jnp.dot(h_scr[...], w_ref[...], preferred_element_type=F32)

    @pl.when(j < n_q_blocks)
    def _():
        q_ref[...] = (_rope(acc, cos_ref[...], sin_ref[...]) * (HEAD_DIM ** -0.5)).astype(BF16)

    @pl.when(j == n_q_blocks)
    def _():
        k = _rope(acc[:, :D_KV], cos_ref[...], sin_ref[...])
        kv = jnp.concatenate([k, acc[:, D_KV:]], axis=1)
        lane = lax.broadcasted_iota(jnp.int32, (kv.shape[0], V7X_LANES), 1)
        left = lane < HEAD_DIM
        pieces = []
        for c0 in range(0, 2 * D_KV, V7X_LANES):
            a = kv[:, c0:c0 + V7X_LANES]
            s = pltpu.roll(a, HEAD_DIM, 1)
            pieces.append(jnp.where(left, a, s))
            pieces.append(jnp.where(left, s, a))
        kv_ref[...] = jnp.concatenate(pieces, axis=1).astype(BF16)

    @pl.when(j > n_q_blocks)
    def _():
        u_ref[...] = acc.astype(BF16)


def _in_proj(x2d, mod3, g1, w_in, cos_t, sin_t, seq, tm=512, tn=512):
    t, d = x2d.shape
    n_q = D_ATTN // tn
    n_u = D_SSM // tn
    assert 2 * D_KV == tn
    nb = seq // tm
    grid = (t // tm, n_q + 1 + n_u)
    kern = functools.partial(_in_proj_kernel, n_q_blocks=n_q)
    est = (2 * tm * d * 4 + tm * d * 2 + 2 * d * tn * 2 + 2 * tm * tn * 2 * 2
           + 2 * tm * 4 * D_KV * 2 + 4 * tm * V7X_LANES * 4 + 4 * tm * tn * 4)
    return pl.pallas_call(
        kern,
        grid=grid,
        in_specs=[
            pl.BlockSpec((tm, d), lambda i, j: (i, 0)),
            pl.BlockSpec((1, N_MOD, d), lambda i, j: (i // nb, 0, 0)),
            pl.BlockSpec((1, d), lambda i, j: (0, 0)),
            pl.BlockSpec((d, tn), lambda i, j: (0, j)),
            pl.BlockSpec((tm, V7X_LANES), lambda i, j: (i % nb, 0)),
            pl.BlockSpec((tm, V7X_LANES), lambda i, j: (i % nb, 0)),
        ],
        out_specs=[
            pl.BlockSpec((tm, tn), lambda i, j: (i, jnp.minimum(j, n_q - 1))),
            pl.BlockSpec((tm, 4 * D_KV), lambda i, j: (i, 0)),
            pl.BlockSpec((tm, tn), lambda i, j: (i, jnp.clip(j - n_q - 1, 0, n_u - 1))),
        ],
        out_shape=[
            jax.ShapeDtypeStruct((t, D_ATTN), BF16),
            jax.ShapeDtypeStruct((t, 4 * D_KV), BF16),
            jax.ShapeDtypeStruct((t, D_SSM), BF16),
        ],
        scratch_shapes=[pltpu.VMEM((tm, d), BF16)],
        compiler_params=pltpu.CompilerParams(
            dimension_semantics=("arbitrary", "arbitrary"),
            vmem_limit_bytes=_vmem_limit(est)),
        name="in_proj",
    )(x2d, mod3, g1, w_in, cos_t, sin_t)


def _attn_kernel(sink_ref, q_ref, kvc_ref, kvp_ref, g_ref, o_ref, o_scr):
    n = pl.program_id(1)
    blk = WINDOW
    pair_w = 2 * HEAD_DIM
    n_keys = 2 * blk

    qi = lax.broadcasted_iota(jnp.int32, (blk, 2 * n_keys), 0)
    col = lax.broadcasted_iota(jnp.int32, (blk, 2 * n_keys), 1)
    key = col % n_keys
    rel = qi + blk - key
    mask = (rel >= 0) & (rel < WINDOW) & ((key >= blk) | (n > 0))

    lane = lax.broadcasted_iota(jnp.int32, (n_keys, pair_w), 1)
    left = lane < HEAD_DIM
    lane_o = lax.broadcasted_iota(jnp.int32, (blk, pair_w), 1)
    left_o = lane_o < HEAD_DIM
    zero = jnp.zeros((n_keys, pair_w), BF16)

    for h in range(N_KV_HEADS):
        kcol = h * pair_w
        vcol = N_KV_HEADS * pair_w + h * pair_w
        kd = jnp.concatenate([kvp_ref[:, kcol:kcol + pair_w], kvc_ref[:, kcol:kcol + pair_w]], axis=0)
        vd = jnp.concatenate([kvp_ref[:, vcol:vcol + pair_w], kvc_ref[:, vcol:vcol + pair_w]], axis=0)
        k_bd = jnp.concatenate([jnp.where(left, kd, zero), jnp.where(left, zero, kd)], axis=0)
        v_bd = jnp.concatenate([jnp.where(left, vd, zero), jnp.where(left, zero, vd)], axis=0)
        for p in range(Q_PER_KV // 2):
            head_a = h * Q_PER_KV + 2 * p
            qcol = head_a * HEAD_DIM
            q2 = q_ref[:, qcol:qcol + pair_w]
            s = lax.dot_general(q2, k_bd, (((1,), (1,)), ((), ())), preferred_element_type=F32)
            s = jnp.where(mask, s, F32(-1e30))
            sink_a = sink_ref[head_a]
            sink_b = sink_ref[head_a + 1]
            m_a = jnp.maximum(jnp.max(s[:, :n_keys], axis=-1, keepdims=True), sink_a)
            m_b = jnp.maximum(jnp.max(s[:, n_keys:], axis=-1, keepdims=True), sink_b)
            p_a = jnp.exp(s[:, :n_keys] - m_a)
            p_b = jnp.exp(s[:, n_keys:] - m_b)
            l_a = jnp.sum(p_a, axis=-1, keepdims=True) + jnp.exp(sink_a - m_a)
            l_b = jnp.sum(p_b, axis=-1, keepdims=True) + jnp.exp(sink_b - m_b)
            pp = jnp.concatenate([p_a, p_b], axis=1).astype(BF16)
            o2 = jnp.dot(pp, v_bd, preferred_element_type=F32)
            inv = jnp.where(left_o, 1.0 / l_a, 1.0 / l_b)
            o_scr[:, qcol:qcol + pair_w] = o2 * inv

    o = o_scr[...]
    o_ref[...] = (o * _rms_scale(o, D_ATTN) * g_ref[...]).astype(BF16)


def _attention(q, kvd, sinks, g_attn, batch, seq):
    t = q.shape[0]
    nb = seq // WINDOW
    kvw = kvd.shape[1]
    return pl.pallas_call(
        _attn_kernel,
        grid=(batch, nb),
        in_specs=[
            pl.BlockSpec(memory_space=pltpu.SMEM),
            pl.BlockSpec((WINDOW, D_ATTN), lambda b, n: (b * nb + n, 0)),
            pl.BlockSpec((WINDOW, kvw), lambda b, n: (b * nb + n, 0)),
            pl.BlockSpec((WINDOW, kvw), lambda b, n: (b * nb + jnp.maximum(n - 1, 0), 0)),
            pl.BlockSpec((1, D_ATTN), lambda b, n: (0, 0)),
        ],
        out_specs=pl.BlockSpec((WINDOW, D_ATTN), lambda b, n: (b * nb + n, 0)),
        out_shape=jax.ShapeDtypeStruct((t, D_ATTN), BF16),
        scratch_shapes=[pltpu.VMEM((WINDOW, D_ATTN), F32)],
        compiler_params=pltpu.CompilerParams(
            dimension_semantics=("arbitrary", "arbitrary"),
            vmem_limit_bytes=_vmem_limit(8 * 1024 * 1024)),
        name="attention",
    )(sinks, q, kvd, kvd, g_attn)


def _ssm_kernel(u_ref, toep_ref, bpow_ref, cpow_ref, mu_ref, o_ref, z_scr, sp_scr, *, n_chunks, rows):
    u = u_ref[0]
    z_scr[...] = jnp.dot(u, bpow_ref[0], preferred_element_type=F32)
    mu_re = mu_ref[0, 0:1, :]
    mu_im = mu_ref[0, 1:2, :]
    seqs = rows // n_chunks

    def scan_step(c, carry):
        s_re, s_im = carry
        r = pl.multiple_of(c * seqs, seqs)
        sp_scr[pl.ds(r, seqs), 0:PAIR_STATE] = s_re
        sp_scr[pl.ds(r, seqs), PAIR_STATE:2 * PAIR_STATE] = s_im
        z_re = z_scr[pl.ds(r, seqs), 0:PAIR_STATE]
        z_im = z_scr[pl.ds(r, seqs), PAIR_STATE:2 * PAIR_STATE]
        return (mu_re * s_re - mu_im * s_im + z_re, mu_re * s_im + mu_im * s_re + z_im)

    zeros = jnp.zeros((seqs, PAIR_STATE), F32)
    f_re, f_im = lax.fori_loop(0, n_chunks, scan_step, (zeros, zeros))

    per_seg = seqs // N_SEG
    row_id = lax.broadcasted_iota(jnp.int32, (seqs, PAIR_STATE), 0)
    later = row_id >= per_seg
    c_re = jnp.where(later, pltpu.roll(f_re, per_seg, 0), 0.0)
    c_im = jnp.where(later, pltpu.roll(f_im, per_seg, 0), 0.0)

    def fix_step(c, carry):
        c_re, c_im = carry
        r = pl.multiple_of(c * seqs, seqs)
        sp_scr[pl.ds(r, seqs), 0:PAIR_STATE] += c_re
        sp_scr[pl.ds(r, seqs), PAIR_STATE:2 * PAIR_STATE] += c_im
        return (mu_re * c_re - mu_im * c_im, mu_re * c_im + mu_im * c_re)

    lax.fori_loop(0, n_chunks, fix_step, (c_re, c_im))

    y0 = jnp.dot(u[:, :CHUNK_COLS], toep_ref[0], preferred_element_type=F32)
    y1 = jnp.dot(u[:, CHUNK_COLS:], toep_ref[1], preferred_element_type=F32)
    y = jnp.concatenate([y0, y1], axis=1)
    y = y + jnp.dot(sp_scr[...].astype(BF16), cpow_ref[0], preferred_element_type=F32)
    o_ref[0] = (0.5 * y * (1.0 + lax.erf(y * (2.0 ** -0.5)))).astype(BF16)


def _ssm(u_t, toep, bpow, cpow, mu, n_chunks):
    n_pairs, rows, _ = u_t.shape
    assert N_SEG == 2, "the stitch step assumes two segments per sequence"
    kern = functools.partial(_ssm_kernel, n_chunks=n_chunks, rows=rows)
    est = (4 * rows * PAIR_COLS * 2 + 2 * (2 * CHUNK_COLS * CHUNK_COLS + 2 * PAIR_COLS * 2 * PAIR_STATE) * 2
           + 2 * rows * 2 * PAIR_STATE * 4 + 6 * rows * PAIR_COLS * 4)
    return pl.pallas_call(
        kern,
        grid=(n_pairs,),
        in_specs=[
            pl.BlockSpec((1, rows, PAIR_COLS), lambda g: (g, 0, 0)),
            pl.BlockSpec((2, CHUNK_COLS, CHUNK_COLS), lambda g: (g, 0, 0)),
            pl.BlockSpec((1, PAIR_COLS, 2 * PAIR_STATE), lambda g: (g, 0, 0)),
            pl.BlockSpec((1, 2 * PAIR_STATE, PAIR_COLS), lambda g: (g, 0, 0)),
            pl.BlockSpec((1, 2, PAIR_STATE), lambda g: (g, 0, 0)),
        ],
        out_specs=pl.BlockSpec((1, rows, PAIR_COLS), lambda g: (g, 0, 0)),
        out_shape=jax.ShapeDtypeStruct((n_pairs, rows, PAIR_COLS), BF16),
        scratch_shapes=[pltpu.VMEM((rows, 2 * PAIR_STATE), F32), pltpu.VMEM((rows, 2 * PAIR_STATE), F32)],
        compiler_params=pltpu.CompilerParams(
            dimension_semantics=("arbitrary",),
            vmem_limit_bytes=_vmem_limit(est)),
        name="ssm",
    )(u_t, toep, bpow, cpow, mu)


def _ssm_operators(lam_re, lam_im, log_step, b_re, b_im, c_re, c_im, d_skip):
    g = N_SSM_GROUPS
    step = jnp.exp(log_step)[:, None]
    lam = lax.complex(lam_re, lam_im)
    dtl = lam * step
    m = jnp.arange(CHUNK + 1, dtype=F32)
    pw = jnp.exp(dtl[:, :, None] * m[None, None, :])
    coef = (pw[:, :, 1] - 1.0) / lam
    bbar = coef[..., None] * lax.complex(b_re, b_im)
    cmat = lax.complex(c_re, c_im)
    kern = jnp.real(jnp.einsum('ghp,gpm,gpk->gmhk', cmat, pw[:, :, :CHUNK], bbar))
    jj = jnp.arange(CHUNK)
    lag = jj[None, :] - jj[:, None]
    blocks = kern[:, jnp.clip(lag, 0, CHUNK - 1)]
    blocks = jnp.where((lag >= 0)[None, :, :, None, None], blocks, 0.0)
    eye_h = jnp.eye(SSM_GROUP, dtype=F32)
    skip = d_skip.reshape(g, SSM_GROUP)[:, None, None, :, None] * eye_h[None, None, None] \
        * jnp.eye(CHUNK, dtype=F32)[None, :, :, None, None]
    toep = (blocks + skip).transpose(0, 1, 4, 2, 3).reshape(g, CHUNK_COLS, CHUNK_COLS)

    cp = cmat[:, :, :, None] * pw[:, None, :, 1:]
    cp = cp.transpose(0, 2, 3, 1).reshape(g, STATE, CHUNK_COLS)
    bp = pw[:, :, :CHUNK][:, :, ::-1][:, :, :, None] * bbar[:, :, None, :]
    bp = bp.transpose(0, 2, 3, 1).reshape(g, CHUNK_COLS, STATE)

    n_pairs = g // 2
    eye2 = jnp.eye(2, dtype=F32)
    bp_re = jnp.real(bp).reshape(n_pairs, 2, CHUNK_COLS, STATE)
    bp_im = jnp.imag(bp).reshape(n_pairs, 2, CHUNK_COLS, STATE)
    def bd_rows(a):
        return jnp.einsum('ngrp,gk->ngrkp', a, eye2).reshape(n_pairs, PAIR_COLS, PAIR_STATE)
    bpow = jnp.concatenate([bd_rows(bp_re), bd_rows(bp_im)], axis=2)
    cp_re = jnp.real(cp).reshape(n_pairs, 2, STATE, CHUNK_COLS)
    cp_im = -jnp.imag(cp).reshape(n_pairs, 2, STATE, CHUNK_COLS)
    def bd_cols(a):
        return jnp.einsum('ngpc,gk->ngpkc', a, eye2).reshape(n_pairs, PAIR_STATE, PAIR_COLS)
    cpow = jnp.concatenate([bd_cols(cp_re), bd_cols(cp_im)], axis=1)
    mu = pw[:, :, CHUNK].reshape(n_pairs, 1, PAIR_STATE)
    mu = jnp.concatenate([jnp.real(mu), jnp.imag(mu)], axis=1)
    return toep.astype(BF16), bpow.astype(BF16), cpow.astype(BF16), mu


def _glu_kernel(y_ref, w_ref, b_ref, g_ref, o_ref):
    y = y_ref[...]
    z = jnp.dot(y, w_ref[...], preferred_element_type=F32) + b_ref[...]
    out = y.astype(F32) * jax.nn.sigmoid(z)
    o_ref[...] = (out * _rms_scale(out, D_SSM) * g_ref[...]).astype(BF16)


def _glu(y, w_glu, b_glu, g_ssm, tm=512):
    t, d = y.shape
    est = 4 * tm * d * 2 + 2 * d * d * 2 + 4 * tm * d * 4
    return pl.pallas_call(
        _glu_kernel,
        grid=(t // tm,),
        in_specs=[
            pl.BlockSpec((tm, d), lambda i: (i, 0)),
            pl.BlockSpec((d, d), lambda i: (0, 0)),
            pl.BlockSpec((1, d), lambda i: (0, 0)),
            pl.BlockSpec((1, d), lambda i: (0, 0)),
        ],
        out_specs=pl.BlockSpec((tm, d), lambda i: (i, 0)),
        out_shape=jax.ShapeDtypeStruct((t, d), BF16),
        compiler_params=pltpu.CompilerParams(
            dimension_semantics=("arbitrary",),
            vmem_limit_bytes=_vmem_limit(est)),
        name="glu",
    )(y, w_glu, b_glu, g_ssm)


def _out_proj_kernel(a_ref, s_ref, wa_ref, ws_ref, x_ref, mod_ref, o_ref):
    acc = jnp.dot(a_ref[...], wa_ref[...], preferred_element_type=F32)
    acc = acc + jnp.dot(s_ref[...], ws_ref[...], preferred_element_type=F32)
    o_ref[...] = x_ref[...] + mod_ref[0, 2:3, :] * acc


def _out_proj(attn_n, ssm_n, w_out, x2d, mod3, seq, tm=512, tn=1024):
    t, d = x2d.shape
    nb = seq // tm
    ka = attn_n.shape[1]
    est = 2 * 2 * tm * ka * 2 + 2 * 2 * ka * tn * 2 + 4 * tm * tn * 4 + 2 * tm * tn * 4
    return pl.pallas_call(
        _out_proj_kernel,
        grid=(t // tm, d // tn),
        in_specs=[
            pl.BlockSpec((tm, ka), lambda i, j: (i, 0)),
            pl.BlockSpec((tm, ka), lambda i, j: (i, 0)),
            pl.BlockSpec((ka, tn), lambda i, j: (0, j)),
            pl.BlockSpec((ka, tn), lambda i, j: (1, j)),
            pl.BlockSpec((tm, tn), lambda i, j: (i, j)),
            pl.BlockSpec((1, N_MOD, tn), lambda i, j: (i // nb, 0, j)),
        ],
        out_specs=pl.BlockSpec((tm, tn), lambda i, j: (i, j)),
        out_shape=jax.ShapeDtypeStruct((t, d), F32),
        compiler_params=pltpu.CompilerParams(
            dimension_semantics=("arbitrary", "arbitrary"),
            vmem_limit_bytes=_vmem_limit(est)),
        name="out_proj",
    )(attn_n, ssm_n, w_out, w_out, x2d, mod3)


def _ffn_kernel(x_ref, mod_ref, g2_ref, w1_ref, w2_ref, gf_ref, o_ref, h_scr, *, n_chunk):
    k = pl.program_id(1)

    @pl.when(k == 0)
    def _():
        xf = x_ref[...]
        y = xf * _rms_scale(xf, D_MODEL) * g2_ref[...]
        h_scr[...] = (y * (1.0 + mod_ref[0, 4:5, :]) + mod_ref[0, 3:4, :]).astype(BF16)
        o_ref[...] = jnp.zeros_like(o_ref)

    a = jnp.dot(h_scr[...], w1_ref[...], preferred_element_type=F32)
    a = jnp.square(jnp.maximum(a, 0.0)).astype(BF16)

    for n0 in range(0, D_MODEL, n_chunk):
        o_ref[:, n0:n0 + n_chunk] += jnp.dot(a, w2_ref[:, n0:n0 + n_chunk], preferred_element_type=F32)

    @pl.when(k == pl.num_programs(1) - 1)
    def _():
        x2 = x_ref[...] + mod_ref[0, 5:6, :] * o_ref[...]
        o_ref[...] = x2 * _rms_scale(x2, D_MODEL) * gf_ref[...]


def _ffn(x1, mod3, g2, w1, w2, gf, seq, tm=512, tf=512):
    t, d = x1.shape
    nb = seq // tm
    f = w1.shape[1]
    est = tm * d * 4 + 2 * tm * d * 4 + tm * d * 2 + 2 * d * tf * 2 + 2 * tf * d * 2 + 4 * tm * tf * 4
    return pl.pallas_call(
        functools.partial(_ffn_kernel, n_chunk=tf),
        grid=(t // tm, f // tf),
        in_specs=[
            pl.BlockSpec((tm, d), lambda i, k: (i, 0), pipeline_mode=pl.Buffered(1)),
            pl.BlockSpec((1, N_MOD, d), lambda i, k: (i // nb, 0, 0)),
            pl.BlockSpec((1, d), lambda i, k: (0, 0)),
            pl.BlockSpec((d, tf), lambda i, k: (0, k)),
            pl.BlockSpec((tf, d), lambda i, k: (k, 0)),
            pl.BlockSpec((1, d), lambda i, k: (0, 0)),
        ],
        out_specs=pl.BlockSpec((tm, d), lambda i, k: (i, 0)),
        out_shape=jax.ShapeDtypeStruct((t, d), F32),
        scratch_shapes=[pltpu.VMEM((tm, d), BF16)],
        compiler_params=pltpu.CompilerParams(
            dimension_semantics=("arbitrary", "arbitrary"),
            vmem_limit_bytes=_vmem_limit(est)),
        name="ffn",
    )(x1, mod3, g2, w1, w2, gf)


def _rope_tables(seq):
    half = HEAD_DIM // 2
    inv_freq = ROPE_THETA ** (-jnp.arange(half, dtype=F32) / half)
    ang = jnp.arange(seq, dtype=F32)[:, None] * inv_freq[None, :]
    cos = jnp.cos(ang)
    sin = jnp.sin(ang)
    reps = V7X_LANES // HEAD_DIM
    cos_t = jnp.tile(jnp.concatenate([cos, cos], axis=1), (1, reps))
    sin_t = jnp.tile(jnp.concatenate([-sin, sin], axis=1), (1, reps))
    return cos_t, sin_t


def kernel(x, c, w_ada, b_ada, norm1_g, w_in, sinks, ssm_lam_re, ssm_lam_im, ssm_log_step, ssm_b_re, ssm_b_im, ssm_c_re, ssm_c_im, ssm_d, w_glu, b_glu, attn_out_g, ssm_out_g, w_out, norm2_g, w_ff1, w_ff2, final_g):
    bsz, seq, d = x.shape
    t = bsz * seq
    x2d = x.reshape(t, d)

    c_pad = jnp.pad(c, ((0, V7X_SUBLANES - bsz), (0, 0)))
    mod = _adaln(c_pad, w_ada[0], b_ada[0].reshape(1, -1))
    mod3 = mod[:bsz].reshape(bsz, N_MOD, d)

    cos_t, sin_t = _rope_tables(seq)
    q, kvd, u = _in_proj(x2d, mod3, norm1_g[0].reshape(1, d), w_in[0].astype(BF16), cos_t, sin_t, seq)

    attn_n = _attention(q, kvd, sinks[0], attn_out_g[0].reshape(1, -1), bsz, seq)

    n_pairs = N_SSM_GROUPS // 2
    seg_chunks = seq // (N_SEG * CHUNK)
    rows = seg_chunks * N_SEG * bsz
    u_t = u.reshape(bsz, N_SEG, seg_chunks, CHUNK, n_pairs, 2, SSM_GROUP)
    u_t = u_t.transpose(4, 2, 1, 0, 5, 3, 6).reshape(n_pairs, rows, PAIR_COLS)
    toep, bpow, cpow, mu = _ssm_operators(ssm_lam_re[0], ssm_lam_im[0], ssm_log_step[0], ssm_b_re[0],
                                          ssm_b_im[0], ssm_c_re[0], ssm_c_im[0], ssm_d[0])
    y_t = _ssm(u_t, toep, bpow, cpow, mu, seg_chunks)
    y = y_t.reshape(n_pairs, seg_chunks, N_SEG, bsz, 2, CHUNK, SSM_GROUP)
    y = y.transpose(3, 2, 1, 5, 0, 4, 6).reshape(t, D_SSM)
    ssm_n = _glu(y, w_glu[0].astype(BF16), b_glu[0].reshape(1, -1), ssm_out_g[0].reshape(1, -1))

    x1 = _out_proj(attn_n, ssm_n, w_out[0].astype(BF16), x2d, mod3, seq)
    out = _ffn(x1, mod3, norm2_g[0].reshape(1, d), w_ff1[0].astype(BF16), w_ff2[0].astype(BF16),
               final_g.reshape(1, d), seq)
    return out.reshape(bsz, seq, d)
```

```python
import functools
import math

import jax
import jax.numpy as jnp
from jax import lax
from jax.experimental import pallas as pl
from jax.experimental.pallas import tpu as pltpu

D_MODEL = 4096
D_ATTN = 2048
D_SSM = 2048
HEAD_DIM = 64
N_Q_HEADS = 32
N_KV_HEADS = 4
Q_PER_KV = 8
D_KV = 256
WINDOW = 128
ROPE_THETA = 10000.0
SSM_GROUP = 16
N_SSM_GROUPS = 128
STATE = 64
D_FF = 4 * D_MODEL
N_MOD = 6
EPS = 1e-6

V7X_LANES = 128
V7X_SUBLANES = 8
V7X_VMEM_BYTES = 64 * 1024 * 1024

CHUNK = 16
N_SEG = 4
SSM_BATCH_SPLIT = 2
SLAB_GROUPS = V7X_LANES // SSM_GROUP
N_SLABS = N_SSM_GROUPS // SLAB_GROUPS
SLAB_COLS = CHUNK * V7X_LANES
SLAB_STATE = SLAB_GROUPS * STATE
TILE = 256
STEPS_PER_TILE = TILE // V7X_LANES

BF16 = jnp.bfloat16
F32 = jnp.float32


def _vmem_limit(nbytes):
    return int(min(nbytes + 8 * 1024 * 1024, V7X_VMEM_BYTES - 4 * 1024 * 1024))


def _rms_scale(xf, width):
    return lax.rsqrt(jnp.sum(xf * xf, axis=-1, keepdims=True) * (1.0 / width) + EPS)


def _adaln_kernel(c_ref, w_ref, b_ref, o_ref):
    c = c_ref[...]
    ca = (c * jax.nn.sigmoid(c)).astype(BF16)
    acc = jnp.dot(ca, w_ref[...].astype(BF16), preferred_element_type=F32)
    o_ref[...] = acc + b_ref[...]


def _adaln(c_pad, w_ada, b_ada, tn=1024):
    m, d = c_pad.shape
    n = w_ada.shape[1]
    return pl.pallas_call(
        _adaln_kernel,
        grid=(n // tn,),
        in_specs=[
            pl.BlockSpec((m, d), lambda j: (0, 0)),
            pl.BlockSpec((d, tn), lambda j: (0, j)),
            pl.BlockSpec((1, tn), lambda j: (0, j)),
        ],
        out_specs=pl.BlockSpec((m, tn), lambda j: (0, j)),
        out_shape=jax.ShapeDtypeStruct((m, n), F32),
        compiler_params=pltpu.CompilerParams(
            dimension_semantics=("arbitrary",),
            vmem_limit_bytes=_vmem_limit(2 * d * tn * 4 + d * tn * 2)),
        name="adaln",
    )(c_pad, w_ada, b_ada)


def _rope(acc, cos, sin_signed):
    width = acc.shape[1]
    lane = lax.broadcasted_iota(jnp.int32, acc.shape, 1)
    first_half = (lane % HEAD_DIM) < (HEAD_DIM // 2)
    partner = jnp.where(first_half,
                        pltpu.roll(acc, width - HEAD_DIM // 2, 1),
                        pltpu.roll(acc, HEAD_DIM // 2, 1))
    reps = width // cos.shape[1]
    cos_w = jnp.concatenate([cos] * reps, axis=1)
    sin_w = jnp.concatenate([sin_signed] * reps, axis=1)
    return acc * cos_w + partner * sin_w


def _in_proj_kernel(x_ref, mod_ref, g_ref, w_ref, cos_ref, sin_ref,
                    q_ref, kv_ref, u_ref, h_scr, u_scr, *, n_q_blocks):
    j = pl.program_id(1)

    @pl.when(j == 0)
    def _():
        xf = x_ref[...]
        y = xf * _rms_scale(xf, D_MODEL) * g_ref[...]
        shift = mod_ref[0, 0:1, :]
        scale = mod_ref[0, 1:2, :]
        h_scr[...] = (y * (1.0 + scale) + shift).astype(BF16)

    acc = jnp.dot(h_scr[...], w_ref[...], preferred_element_type=F32)

    @pl.when(j < n_q_blocks)
    def _():
        q_ref[...] = (_rope(acc, cos_ref[...], sin_ref[...]) * (HEAD_DIM ** -0.5)).astype(BF16)

    @pl.when(j == n_q_blocks)
    def _():
        k = _rope(acc[:, :D_KV], cos_ref[...], sin_ref[...])
        kv = jnp.concatenate([k, acc[:, D_KV:]], axis=1)
        lane = lax.broadcasted_iota(jnp.int32, (kv.shape[0], V7X_LANES), 1)
        left = lane < HEAD_DIM
        pieces = []
        for c0 in range(0, 2 * D_KV, V7X_LANES):
            a = kv[:, c0:c0 + V7X_LANES]
            s = pltpu.roll(a, HEAD_DIM, 1)
            pieces.append(jnp.where(left, a, s))
            pieces.append(jnp.where(left, s, a))
        kv_ref[...] = jnp.concatenate(pieces, axis=1).astype(BF16)

    @pl.when(j > n_q_blocks)
    def _():
        n_slabs, chunk_rows, _ = u_ref.shape
        for s in range(n_slabs):
            u_scr[s] = acc[:, s * V7X_LANES:(s + 1) * V7X_LANES]
        for s in range(n_slabs):
            for step in range(CHUNK):
                rows = u_scr[s, pl.ds(step, chunk_rows, stride=CHUNK), :]
                u_ref[s, :, step * V7X_LANES:(step + 1) * V7X_LANES] = rows.astype(BF16)


def _in_proj(x2d, mod3, g1, w_in, cos_t, sin_t, seq, tm=512, tn=512):
    t, d = x2d.shape
    n_q = D_ATTN // tn
    n_u = D_SSM // tn
    assert 2 * D_KV == tn
    slabs_per_blk = tn // V7X_LANES
    nb = seq // tm
    grid = (t // tm, n_q + 1 + n_u)
    kern = functools.partial(_in_proj_kernel, n_q_blocks=n_q)
    est = (2 * tm * d * 4 + tm * d * 2 + 2 * d * tn * 2 + 2 * tm * tn * 2 * 2
           + 2 * tm * 4 * D_KV * 2 + 4 * tm * V7X_LANES * 4 + 4 * tm * tn * 4)
    return pl.pallas_call(
        kern,
        grid=grid,
        in_specs=[
            pl.BlockSpec((tm, d), lambda i, j: (i, 0)),
            pl.BlockSpec((1, N_MOD, d), lambda i, j: (i // nb, 0, 0)),
            pl.BlockSpec((1, d), lambda i, j: (0, 0)),
            pl.BlockSpec((d, tn), lambda i, j: (0, j)),
            pl.BlockSpec((tm, V7X_LANES), lambda i, j: (i % nb, 0)),
            pl.BlockSpec((tm, V7X_LANES), lambda i, j: (i % nb, 0)),
        ],
        out_specs=[
            pl.BlockSpec((tm, tn), lambda i, j: (i, jnp.minimum(j, n_q - 1))),
            pl.BlockSpec((tm, 4 * D_KV), lambda i, j: (i, 0)),
            pl.BlockSpec((slabs_per_blk, tm // CHUNK, SLAB_COLS),
                         lambda i, j: (jnp.clip(j - n_q - 1, 0, n_u - 1), i, 0)),
        ],
        out_shape=[
            jax.ShapeDtypeStruct((t, D_ATTN), BF16),
            jax.ShapeDtypeStruct((t, 4 * D_KV), BF16),
            jax.ShapeDtypeStruct((N_SLABS, t // CHUNK, SLAB_COLS), BF16),
        ],
        scratch_shapes=[pltpu.VMEM((tm, d), BF16), pltpu.VMEM((slabs_per_blk, tm, V7X_LANES), F32)],
        compiler_params=pltpu.CompilerParams(
            dimension_semantics=("arbitrary", "arbitrary"),
            vmem_limit_bytes=_vmem_limit(est)),
        name="in_proj",
    )(x2d, mod3, g1, w_in, cos_t, sin_t)


def _attn_kernel(sink_ref, q_ref, kvc_ref, kvp_ref, g_ref, o_ref, o_scr):
    n = pl.program_id(1)
    blk = WINDOW
    pair_w = 2 * HEAD_DIM
    n_keys = 2 * blk

    qi = lax.broadcasted_iota(jnp.int32, (blk, 2 * n_keys), 0)
    col = lax.broadcasted_iota(jnp.int32, (blk, 2 * n_keys), 1)
    key = col % n_keys
    rel = qi + blk - key
    mask = (rel >= 0) & (rel < WINDOW) & ((key >= blk) | (n > 0))

    lane = lax.broadcasted_iota(jnp.int32, (n_keys, pair_w), 1)
    left = lane < HEAD_DIM
    lane_o = lax.broadcasted_iota(jnp.int32, (blk, pair_w), 1)
    left_o = lane_o < HEAD_DIM
    zero = jnp.zeros((n_keys, pair_w), BF16)

    for h in range(N_KV_HEADS):
        kcol = h * pair_w
        vcol = N_KV_HEADS * pair_w + h * pair_w
        kd = jnp.concatenate([kvp_ref[:, kcol:kcol + pair_w], kvc_ref[:, kcol:kcol + pair_w]], axis=0)
        vd = jnp.concatenate([kvp_ref[:, vcol:vcol + pair_w], kvc_ref[:, vcol:vcol + pair_w]], axis=0)
        k_bd = jnp.concatenate([jnp.where(left, kd, zero), jnp.where(left, zero, kd)], axis=0)
        v_bd = jnp.concatenate([jnp.where(left, vd, zero), jnp.where(left, zero, vd)], axis=0)
        for p in range(Q_PER_KV // 2):
            head_a = h * Q_PER_KV + 2 * p
            qcol = head_a * HEAD_DIM
            q2 = q_ref[:, qcol:qcol + pair_w]
            s = lax.dot_general(q2, k_bd, (((1,), (1,)), ((), ())), preferred_element_type=F32)
            s = jnp.where(mask, s, F32(-1e30))
            sink_a = sink_ref[head_a]
            sink_b = sink_ref[head_a + 1]
            m_a = jnp.maximum(jnp.max(s[:, :n_keys], axis=-1, keepdims=True), sink_a)
            m_b = jnp.maximum(jnp.max(s[:, n_keys:], axis=-1, keepdims=True), sink_b)
            p_a = jnp.exp(s[:, :n_keys] - m_a)
            p_b = jnp.exp(s[:, n_keys:] - m_b)
            l_a = jnp.sum(p_a, axis=-1, keepdims=True) + jnp.exp(sink_a - m_a)
            l_b = jnp.sum(p_b, axis=-1, keepdims=True) + jnp.exp(sink_b - m_b)
            pp = jnp.concatenate([p_a, p_b], axis=1).astype(BF16)
            o2 = jnp.dot(pp, v_bd, preferred_element_type=F32)
            inv = jnp.where(left_o, 1.0 / l_a, 1.0 / l_b)
            o_scr[:, qcol:qcol + pair_w] = o2 * inv

    o = o_scr[...]
    o_ref[...] = (o * _rms_scale(o, D_ATTN) * g_ref[...]).astype(BF16)


def _attention(q, kvd, sinks, g_attn, batch, seq):
    t = q.shape[0]
    nb = seq // WINDOW
    kvw = kvd.shape[1]
    return pl.pallas_call(
        _attn_kernel,
        grid=(batch, nb),
        in_specs=[
            pl.BlockSpec(memory_space=pltpu.SMEM),
            pl.BlockSpec((WINDOW, D_ATTN), lambda b, n: (b * nb + n, 0)),
            pl.BlockSpec((WINDOW, kvw), lambda b, n: (b * nb + n, 0)),
            pl.BlockSpec((WINDOW, kvw), lambda b, n: (b * nb + jnp.maximum(n - 1, 0), 0)),
            pl.BlockSpec((1, D_ATTN), lambda b, n: (0, 0)),
        ],
        out_specs=pl.BlockSpec((WINDOW, D_ATTN), lambda b, n: (b * nb + n, 0)),
        out_shape=jax.ShapeDtypeStruct((t, D_ATTN), BF16),
        scratch_shapes=[pltpu.VMEM((WINDOW, D_ATTN), F32)],
        compiler_params=pltpu.CompilerParams(
            dimension_semantics=("arbitrary", "arbitrary"),
            vmem_limit_bytes=_vmem_limit(8 * 1024 * 1024)),
        name="attention",
    )(sinks, q, kvd, kvd, g_attn)


def _cmul(a_re, a_im, b_re, b_im):
    return a_re * b_re - a_im * b_im, a_re * b_im + a_im * b_re


def _ssm_kernel(u_ref, rev_ref, bpow_ref, cpow_ref, mu_ref, o_ref, z_scr, sp_scr, *, seg_chunks):
    n_k = SLAB_STATE // V7X_LANES
    rows = u_ref.shape[1]
    seqs = rows // seg_chunks
    shape = (seqs, V7X_LANES)

    for nb in range(2 * SLAB_STATE // TILE):
        zz = jnp.dot(u_ref[0], bpow_ref[0, :, nb * TILE:(nb + 1) * TILE], preferred_element_type=F32)
        z_scr[2 * nb] = zz[:, :V7X_LANES]
        z_scr[2 * nb + 1] = zz[:, V7X_LANES:]

    mu_re = [jnp.broadcast_to(mu_ref[0, 0:1, k * V7X_LANES:(k + 1) * V7X_LANES], shape) for k in range(n_k)]
    mu_im = [jnp.broadcast_to(mu_ref[0, 1:2, k * V7X_LANES:(k + 1) * V7X_LANES], shape) for k in range(n_k)]

    def rows_at(c):
        return pl.ds(c, seqs, stride=seg_chunks)

    def scan_step(c, carry):
        new = []
        for k in range(n_k):
            s_re, s_im = carry[2 * k], carry[2 * k + 1]
            sp_scr[k, rows_at(c), :] = s_re
            sp_scr[n_k + k, rows_at(c), :] = s_im
            p_re, p_im = _cmul(mu_re[k], mu_im[k], s_re, s_im)
            new += [p_re + z_scr[k, rows_at(c), :], p_im + z_scr[n_k + k, rows_at(c), :]]
        return tuple(new)

    zeros = jnp.zeros(shape, F32)
    final = lax.fori_loop(0, seg_chunks, scan_step, (zeros,) * (2 * n_k))

    first_seg = (lax.broadcasted_iota(jnp.int32, shape, 0) % N_SEG) == 0
    init = []
    for k in range(n_k):
        m_re, m_im = mu_re[k], mu_im[k]
        for _ in range(int(math.log2(seg_chunks))):
            m_re, m_im = _cmul(m_re, m_im, m_re, m_im)
        i_re, i_im = zeros, zeros
        for _ in range(N_SEG - 1):
            t_re, t_im = _cmul(m_re, m_im, i_re, i_im)
            i_re = jnp.where(first_seg, 0.0, pltpu.roll(t_re + final[2 * k], 1, 0))
            i_im = jnp.where(first_seg, 0.0, pltpu.roll(t_im + final[2 * k + 1], 1, 0))
        init += [i_re, i_im]

    def fix_step(c, carry):
        new = []
        for k in range(n_k):
            c_re, c_im = carry[2 * k], carry[2 * k + 1]
            sp_scr[k, rows_at(c), :] += c_re
            sp_scr[n_k + k, rows_at(c), :] += c_im
            new += list(_cmul(mu_re[k], mu_im[k], c_re, c_im))
        return tuple(new)

    lax.fori_loop(0, seg_chunks, fix_step, tuple(init))

    sp = jnp.concatenate([sp_scr[k] for k in range(2 * n_k)], axis=1).astype(BF16)
    for jt in range(SLAB_COLS // TILE):
        k_len = (jt + 1) * TILE
        y = jnp.dot(u_ref[0, :, :k_len], rev_ref[0, SLAB_COLS - k_len:, :], preferred_element_type=F32)
        y = y + jnp.dot(sp, cpow_ref[0, :, jt * TILE:(jt + 1) * TILE], preferred_element_type=F32)
        y = 0.5 * y * (1.0 + lax.erf(y * (2.0 ** -0.5)))
        for q in range(STEPS_PER_TILE):
            step = jt * STEPS_PER_TILE + q
            o_ref[pl.ds(step, rows, stride=CHUNK), :] = y[:, q * V7X_LANES:(q + 1) * V7X_LANES]


def _ssm(u_c, rev, bpow, cpow, mu, bsz, seq):
    n_slabs, chunk_rows, _ = u_c.shape
    t = chunk_rows * CHUNK
    rows = chunk_rows // SSM_BATCH_SPLIT
    seg_chunks = seq // (CHUNK * N_SEG)
    assert bsz % SSM_BATCH_SPLIT == 0 and rows // seg_chunks == V7X_SUBLANES
    assert seg_chunks & (seg_chunks - 1) == 0
    kern = functools.partial(_ssm_kernel, seg_chunks=seg_chunks)
    est = (2 * rows * SLAB_COLS * 2 + 2 * SLAB_COLS * TILE * 2 + 4 * SLAB_COLS * 2 * SLAB_STATE * 2
           + 2 * rows * CHUNK * V7X_LANES * 4 + 2 * rows * 2 * SLAB_STATE * 4 + 8 * rows * TILE * 4
           + rows * 2 * SLAB_STATE * 2)
    return pl.pallas_call(
        kern,
        grid=(n_slabs, SSM_BATCH_SPLIT),
        in_specs=[
            pl.BlockSpec((1, rows, SLAB_COLS), lambda s, h: (s, h, 0)),
            pl.BlockSpec((1, SLAB_COLS, TILE), lambda s, h: (s, 0, 0)),
            pl.BlockSpec((1, SLAB_COLS, 2 * SLAB_STATE), lambda s, h: (s, 0, 0)),
            pl.BlockSpec((1, 2 * SLAB_STATE, SLAB_COLS), lambda s, h: (s, 0, 0)),
            pl.BlockSpec((1, 2, SLAB_STATE), lambda s, h: (s, 0, 0)),
        ],
        out_specs=pl.BlockSpec((rows * CHUNK, V7X_LANES), lambda s, h: (h, s)),
        out_shape=jax.ShapeDtypeStruct((t, n_slabs * V7X_LANES), F32),
        scratch_shapes=[pltpu.VMEM((2 * SLAB_STATE // V7X_LANES, rows, V7X_LANES), F32),
                        pltpu.VMEM((2 * SLAB_STATE // V7X_LANES, rows, V7X_LANES), F32)],
        compiler_params=pltpu.CompilerParams(
            dimension_semantics=("arbitrary", "arbitrary"),
            vmem_limit_bytes=_vmem_limit(est)),
        name="ssm",
    )(u_c, rev, bpow, cpow, mu)


def _ssm_operators(lam_re, lam_im, log_step, b_re, b_im, c_re, c_im, d_skip):
    g, s, sg = N_SSM_GROUPS, N_SLABS, SLAB_GROUPS
    step = jnp.exp(log_step)[:, None]
    dt_re, dt_im = lam_re * step, lam_im * step
    m = jnp.arange(CHUNK + 1, dtype=F32)
    mag = jnp.exp(dt_re[:, :, None] * m)
    pw_re = mag * jnp.cos(dt_im[:, :, None] * m)
    pw_im = mag * jnp.sin(dt_im[:, :, None] * m)
    num_re, num_im = pw_re[:, :, 1] - 1.0, pw_im[:, :, 1]
    den = lam_re * lam_re + lam_im * lam_im
    coef_re = (num_re * lam_re + num_im * lam_im) / den
    coef_im = (num_im * lam_re - num_re * lam_im) / den
    bb_re = coef_re[..., None] * b_re - coef_im[..., None] * b_im
    bb_im = coef_re[..., None] * b_im + coef_im[..., None] * b_re
    w_re = c_re[..., None] * pw_re[:, None] - c_im[..., None] * pw_im[:, None]
    w_im = c_re[..., None] * pw_im[:, None] + c_im[..., None] * pw_re[:, None]
    eye_g = jnp.eye(sg, dtype=F32)

    kern = (jnp.einsum('ghpm,gpk->gmhk', w_re[..., :CHUNK], bb_re)
            - jnp.einsum('ghpm,gpk->gmhk', w_im[..., :CHUNK], bb_im))
    kern = kern.at[:, 0].add(d_skip.reshape(g, SSM_GROUP)[:, :, None] * jnp.eye(SSM_GROUP, dtype=F32))
    kern = jnp.concatenate([jnp.zeros_like(kern[:, :1]), kern], axis=1)
    n_tiles = SLAB_COLS // TILE
    d_idx = jnp.arange(n_tiles)[:, None, None]
    jin = jnp.arange(STEPS_PER_TILE)[None, :, None]
    jout = jnp.arange(STEPS_PER_TILE)[None, None, :]
    lag_slot = STEPS_PER_TILE * d_idx + jout - jin + 1
    tiles = kern[:, lag_slot]
    tiles = tiles.reshape(s, sg, n_tiles, STEPS_PER_TILE, STEPS_PER_TILE, SSM_GROUP, SSM_GROUP)
    tiles = tiles.transpose(0, 2, 3, 1, 6, 4, 5)
    tiles = tiles[:, :, :, :, :, :, None, :] * eye_g[None, None, None, :, None, None, :, None]
    rev = tiles[:, ::-1].reshape(s, SLAB_COLS, TILE)

    pb_re = pw_re[:, :, CHUNK - 1::-1]
    pb_im = pw_im[:, :, CHUNK - 1::-1]
    bp_re = pb_re[:, :, :, None] * bb_re[:, :, None, :] - pb_im[:, :, :, None] * bb_im[:, :, None, :]
    bp_im = pb_re[:, :, :, None] * bb_im[:, :, None, :] + pb_im[:, :, :, None] * bb_re[:, :, None, :]
    bp = jnp.stack([bp_re, bp_im], axis=1)
    bp = bp.reshape(s, sg, 2, STATE, CHUNK, SSM_GROUP).transpose(0, 4, 1, 5, 2, 3)
    bp = bp[:, :, :, :, :, None, :] * eye_g[None, None, :, None, None, :, None]
    bpow = bp.reshape(s, SLAB_COLS, 2 * SLAB_STATE)

    cp = jnp.stack([w_re[..., 1:], -w_im[..., 1:]], axis=1)
    cp = cp.reshape(s, sg, 2, SSM_GROUP, STATE, CHUNK).transpose(0, 2, 1, 4, 5, 3)
    cp = cp[:, :, :, :, :, None, :] * eye_g[None, None, :, None, None, :, None]
    cpow = cp.reshape(s, 2 * SLAB_STATE, SLAB_COLS)

    mu = jnp.stack([pw_re[:, :, CHUNK], pw_im[:, :, CHUNK]], axis=0)
    mu = mu.reshape(2, s, SLAB_STATE).transpose(1, 0, 2)
    return rev.astype(BF16), bpow.astype(BF16), cpow.astype(BF16), mu


def _glu_kernel(y_ref, w_ref, b_ref, g_ref, o_ref):
    y = y_ref[...]
    z = jnp.dot(y.astype(BF16), w_ref[...], preferred_element_type=F32) + b_ref[...]
    out = y * jax.nn.sigmoid(z)
    o_ref[...] = (out * _rms_scale(out, D_SSM) * g_ref[...]).astype(BF16)


def _glu(y, w_glu, b_glu, g_ssm, tm=512):
    t, d = y.shape
    est = 2 * tm * d * 4 + 2 * tm * d * 2 + 2 * d * d * 2 + 4 * tm * d * 4
    return pl.pallas_call(
        _glu_kernel,
        grid=(t // tm,),
        in_specs=[
            pl.BlockSpec((tm, d), lambda i: (i, 0)),
            pl.BlockSpec((d, d), lambda i: (0, 0)),
            pl.BlockSpec((1, d), lambda i: (0, 0)),
            pl.BlockSpec((1, d), lambda i: (0, 0)),
        ],
        out_specs=pl.BlockSpec((tm, d), lambda i: (i, 0)),
        out_shape=jax.ShapeDtypeStruct((t, d), BF16),
        compiler_params=pltpu.CompilerParams(
            dimension_semantics=("arbitrary",),
            vmem_limit_bytes=_vmem_limit(est)),
        name="glu",
    )(y, w_glu, b_glu, g_ssm)


def _out_proj_kernel(a_ref, s_ref, wa_ref, ws_ref, x_ref, mod_ref, o_ref):
    acc = jnp.dot(a_ref[...], wa_ref[...], preferred_element_type=F32)
    acc = acc + jnp.dot(s_ref[...], ws_ref[...], preferred_element_type=F32)
    o_ref[...] = x_ref[...] + mod_ref[0, 2:3, :] * acc


def _out_proj(attn_n, ssm_n, w_out, x2d, mod3, seq, tm=512, tn=1024):
    t, d = x2d.shape
    nb = seq // tm
    ka = attn_n.shape[1]
    est = 2 * 2 * tm * ka * 2 + 2 * 2 * ka * tn * 2 + 4 * tm * tn * 4 + 2 * tm * tn * 4
    return pl.pallas_call(
        _out_proj_kernel,
        grid=(t // tm, d // tn),
        in_specs=[
            pl.BlockSpec((tm, ka), lambda i, j: (i, 0)),
            pl.BlockSpec((tm, ka), lambda i, j: (i, 0)),
            pl.BlockSpec((ka, tn), lambda i, j: (0, j)),
            pl.BlockSpec((ka, tn), lambda i, j: (1, j)),
            pl.BlockSpec((tm, tn), lambda i, j: (i, j)),
            pl.BlockSpec((1, N_MOD, tn), lambda i, j: (i // nb, 0, j)),
        ],
        out_specs=pl.BlockSpec((tm, tn), lambda i, j: (i, j)),
        out_shape=jax.ShapeDtypeStruct((t, d), F32),
        compiler_params=pltpu.CompilerParams(
            dimension_semantics=("arbitrary", "arbitrary"),
            vmem_limit_bytes=_vmem_limit(est)),
        name="out_proj",
    )(attn_n, ssm_n, w_out, w_out, x2d, mod3)


def _ffn_kernel(x_ref, mod_ref, g2_ref, w1_ref, w2_ref, gf_ref, o_ref, h_scr, *, n_chunk):
    k = pl.program_id(1)

    @pl.when(k == 0)
    def _():
        xf = x_ref[...]
        y = xf * _rms_scale(xf, D_MODEL) * g2_ref[...]
        h_scr[...] = (y * (1.0 + mod_ref[0, 4:5, :]) + mod_ref[0, 3:4, :]).astype(BF16)
        o_ref[...] = jnp.zeros_like(o_ref)

    a = jnp.dot(h_scr[...], w1_ref[...], preferred_element_type=F32)
    a = jnp.square(jnp.maximum(a, 0.0)).astype(BF16)

    for n0 in range(0, D_MODEL, n_chunk):
        o_ref[:, n0:n0 + n_chunk] += jnp.dot(a, w2_ref[:, n0:n0 + n_chunk], preferred_element_type=F32)

    @pl.when(k == pl.num_programs(1) - 1)
    def _():
        x2 = x_ref[...] + mod_ref[0, 5:6, :] * o_ref[...]
        o_ref[...] = x2 * _rms_scale(x2, D_MODEL) * gf_ref[...]


def _ffn(x1, mod3, g2, w1, w2, gf, seq, tm=512, tf=512):
    t, d = x1.shape
    nb = seq // tm
    f = w1.shape[1]
    est = tm * d * 4 + 2 * tm * d * 4 + tm * d * 2 + 2 * d * tf * 2 + 2 * tf * d * 2 + 4 * tm * tf * 4
    return pl.pallas_call(
        functools.partial(_ffn_kernel, n_chunk=tf),
        grid=(t // tm, f // tf),
        in_specs=[
            pl.BlockSpec((tm, d), lambda i, k: (i, 0), pipeline_mode=pl.Buffered(1)),
            pl.BlockSpec((1, N_MOD, d), lambda i, k: (i // nb, 0, 0)),
            pl.BlockSpec((1, d), lambda i, k: (0, 0)),
            pl.BlockSpec((d, tf), lambda i, k: (0, k)),
            pl.BlockSpec((tf, d), lambda i, k: (k, 0)),
            pl.BlockSpec((1, d), lambda i, k: (0, 0)),
        ],
        out_specs=pl.BlockSpec((tm, d), lambda i, k: (i, 0)),
        out_shape=jax.ShapeDtypeStruct((t, d), F32),
        scratch_shapes=[pltpu.VMEM((tm, d), BF16)],
        compiler_params=pltpu.CompilerParams(
            dimension_semantics=("arbitrary", "arbitrary"),
            vmem_limit_bytes=_vmem_limit(est)),
        name="ffn",
    )(x1, mod3, g2, w1, w2, gf)


def _rope_tables(seq):
    half = HEAD_DIM // 2
    inv_freq = ROPE_THETA ** (-jnp.arange(half, dtype=F32) / half)
    ang = jnp.arange(seq, dtype=F32)[:, None] * inv_freq[None, :]
    cos = jnp.cos(ang)
    sin = jnp.sin(ang)
    reps = V7X_LANES // HEAD_DIM
    cos_t = jnp.tile(jnp.concatenate([cos, cos], axis=1), (1, reps))
    sin_t = jnp.tile(jnp.concatenate([-sin, sin], axis=1), (1, reps))
    return cos_t, sin_t


def kernel(x, c, w_ada, b_ada, norm1_g, w_in, sinks, ssm_lam_re, ssm_lam_im, ssm_log_step, ssm_b_re, ssm_b_im, ssm_c_re, ssm_c_im, ssm_d, w_glu, b_glu, attn_out_g, ssm_out_g, w_out, norm2_g, w_ff1, w_ff2, final_g):
    bsz, seq, d = x.shape
    t = bsz * seq
    x2d = x.reshape(t, d)

    c_pad = jnp.pad(c, ((0, V7X_SUBLANES - bsz), (0, 0)))
    mod = _adaln(c_pad, w_ada[0], b_ada[0].reshape(1, -1))
    mod3 = mod[:bsz].reshape(bsz, N_MOD, d)

    cos_t, sin_t = _rope_tables(seq)
    q, kvd, u_c = _in_proj(x2d, mod3, norm1_g[0].reshape(1, d), w_in[0].astype(BF16), cos_t, sin_t, seq)

    attn_n = _attention(q, kvd, sinks[0], attn_out_g[0].reshape(1, -1), bsz, seq)

    rev, bpow, cpow, mu = _ssm_operators(ssm_lam_re[0], ssm_lam_im[0], ssm_log_step[0], ssm_b_re[0],
                                         ssm_b_im[0], ssm_c_re[0], ssm_c_im[0], ssm_d[0])
    y = _ssm(u_c, rev, bpow, cpow, mu, bsz, seq)
    ssm_n = _glu(y, w_glu[0].astype(BF16), b_glu[0].reshape(1, -1), ssm_out_g[0].reshape(1, -1))

    x1 = _out_proj(attn_n, ssm_n, w_out[0].astype(BF16), x2d, mod3, seq)
    out = _ffn(x1, mod3, norm2_g[0].reshape(1, d), w_ff1[0].astype(BF16), w_ff2[0].astype(BF16),
               final_g.reshape(1, d), seq)
    return out.reshape(bsz, seq, d)
```

```python
import functools
import math

import jax
import jax.numpy as jnp
from jax import lax
from jax.experimental import pallas as pl
from jax.experimental.pallas import tpu as pltpu

D_MODEL = 4096
D_ATTN = 2048
D_SSM = 2048
HEAD_DIM = 64
N_Q_HEADS = 32
N_KV_HEADS = 4
Q_PER_KV = 8
D_KV = 256
WINDOW = 128
ROPE_THETA = 10000.0
SSM_GROUP = 16
N_SSM_GROUPS = 128
STATE = 64
D_FF = 4 * D_MODEL
N_MOD = 6
EPS = 1e-6

V7X_LANES = 128
V7X_SUBLANES = 8
V7X_VMEM_BYTES = 64 * 1024 * 1024

CHUNK = 16
N_SEG = 4
SSM_BATCH_SPLIT = 2
SLAB_GROUPS = V7X_LANES // SSM_GROUP
N_SLABS = N_SSM_GROUPS // SLAB_GROUPS
SLAB_COLS = CHUNK * V7X_LANES
SLAB_STATE = SLAB_GROUPS * STATE
TILE = 256
STEPS_PER_TILE = TILE // V7X_LANES
SCAN_UNROLL = 8

BF16 = jnp.bfloat16
F32 = jnp.float32


def _vmem_limit(nbytes):
    return int(min(nbytes + 8 * 1024 * 1024, V7X_VMEM_BYTES - 4 * 1024 * 1024))


def _rms_scale(xf, width):
    return lax.rsqrt(jnp.sum(xf * xf, axis=-1, keepdims=True) * (1.0 / width) + EPS)


NORM_ROWS = 16
NORM_UNROLL = 4


def _modulated_norm(x_ref, gain_ref, shift_ref, out_ref):
    width = x_ref.shape[1]

    def body(r, carry):
        rows = pl.ds(pl.multiple_of(r * NORM_ROWS, NORM_ROWS), NORM_ROWS)
        xf = x_ref[rows, :]
        out_ref[rows, :] = (xf * _rms_scale(xf, width) * gain_ref[...] + shift_ref[...]).astype(out_ref.dtype)
        return carry

    lax.fori_loop(0, x_ref.shape[0] // NORM_ROWS, body, 0, unroll=NORM_UNROLL)


def _adaln_kernel(c_ref, w_ref, b_ref, o_ref):
    c = c_ref[...]
    ca = (c * jax.nn.sigmoid(c)).astype(BF16)
    acc = jnp.dot(ca, w_ref[...].astype(BF16), preferred_element_type=F32)
    o_ref[...] = acc + b_ref[...]


def _adaln(c_pad, w_ada, b_ada, tn=1024):
    m, d = c_pad.shape
    n = w_ada.shape[1]
    return pl.pallas_call(
        _adaln_kernel,
        grid=(n // tn,),
        in_specs=[
            pl.BlockSpec((m, d), lambda j: (0, 0)),
            pl.BlockSpec((d, tn), lambda j: (0, j)),
            pl.BlockSpec((1, tn), lambda j: (0, j)),
        ],
        out_specs=pl.BlockSpec((m, tn), lambda j: (0, j)),
        out_shape=jax.ShapeDtypeStruct((m, n), F32),
        compiler_params=pltpu.CompilerParams(
            dimension_semantics=("arbitrary",),
            vmem_limit_bytes=_vmem_limit(2 * d * tn * 4 + d * tn * 2)),
        name="adaln",
    )(c_pad, w_ada, b_ada)


def _rope(acc, cos, sin_signed):
    width = acc.shape[1]
    lane = lax.broadcasted_iota(jnp.int32, acc.shape, 1)
    first_half = (lane % HEAD_DIM) < (HEAD_DIM // 2)
    partner = jnp.where(first_half,
                        pltpu.roll(acc, width - HEAD_DIM // 2, 1),
                        pltpu.roll(acc, HEAD_DIM // 2, 1))
    reps = width // cos.shape[1]
    cos_w = jnp.concatenate([cos] * reps, axis=1)
    sin_w = jnp.concatenate([sin_signed] * reps, axis=1)
    return acc * cos_w + partner * sin_w


def _in_proj_kernel(x_ref, mod_ref, g_ref, w_ref, cos_ref, sin_ref,
                    q_ref, kv_ref, u_ref, h_scr, u_scr, gain_scr, *, n_q_blocks):
    j = pl.program_id(1)

    @pl.when(j == 0)
    def _():
        gain_scr[...] = g_ref[...] * (1.0 + mod_ref[0, 1:2, :])
        _modulated_norm(x_ref, gain_scr, mod_ref.at[0, 0:1, :], h_scr)

    acc = jnp.dot(h_scr[...], w_ref[...], preferred_element_type=F32)

    @pl.when(j < n_q_blocks)
    def _():
        q_ref[...] = (_rope(acc, cos_ref[...], sin_ref[...]) * (HEAD_DIM ** -0.5)).astype(BF16)

    @pl.when(j == n_q_blocks)
    def _():
        k = _rope(acc[:, :D_KV], cos_ref[...], sin_ref[...])
        kv = jnp.concatenate([k, acc[:, D_KV:]], axis=1)
        lane = lax.broadcasted_iota(jnp.int32, (kv.shape[0], V7X_LANES), 1)
        left = lane < HEAD_DIM
        pieces = []
        for c0 in range(0, 2 * D_KV, V7X_LANES):
            a = kv[:, c0:c0 + V7X_LANES]
            s = pltpu.roll(a, HEAD_DIM, 1)
            pieces.append(jnp.where(left, a, s))
            pieces.append(jnp.where(left, s, a))
        kv_ref[...] = jnp.concatenate(pieces, axis=1).astype(BF16)

    @pl.when(j > n_q_blocks)
    def _():
        n_slabs, chunk_rows, _ = u_ref.shape
        for s in range(n_slabs):
            u_scr[s] = acc[:, s * V7X_LANES:(s + 1) * V7X_LANES]
        for s in range(n_slabs):
            for step in range(CHUNK):
                rows = u_scr[s, pl.ds(step, chunk_rows, stride=CHUNK), :]
                u_ref[s, :, step * V7X_LANES:(step + 1) * V7X_LANES] = rows.astype(BF16)


def _in_proj(x2d, mod3, g1, w_in, cos_t, sin_t, seq, tm=512, tn=512):
    t, d = x2d.shape
    n_q = D_ATTN // tn
    n_u = D_SSM // tn
    assert 2 * D_KV == tn
    slabs_per_blk = tn // V7X_LANES
    nb = seq // tm
    grid = (t // tm, n_q + 1 + n_u)
    kern = functools.partial(_in_proj_kernel, n_q_blocks=n_q)
    est = (2 * tm * d * 4 + tm * d * 2 + 2 * d * tn * 2 + 2 * tm * tn * 2 * 2
           + 2 * tm * 4 * D_KV * 2 + 4 * tm * V7X_LANES * 4 + 4 * tm * tn * 4)
    return pl.pallas_call(
        kern,
        grid=grid,
        in_specs=[
            pl.BlockSpec((tm, d), lambda i, j: (i, 0)),
            pl.BlockSpec((1, N_MOD, d), lambda i, j: (i // nb, 0, 0)),
            pl.BlockSpec((1, d), lambda i, j: (0, 0)),
            pl.BlockSpec((d, tn), lambda i, j: (0, j)),
            pl.BlockSpec((tm, V7X_LANES), lambda i, j: (i % nb, 0)),
            pl.BlockSpec((tm, V7X_LANES), lambda i, j: (i % nb, 0)),
        ],
        out_specs=[
            pl.BlockSpec((tm, tn), lambda i, j: (i, jnp.minimum(j, n_q - 1))),
            pl.BlockSpec((tm, 4 * D_KV), lambda i, j: (i, 0)),
            pl.BlockSpec((slabs_per_blk, tm // CHUNK, SLAB_COLS),
                         lambda i, j: (jnp.clip(j - n_q - 1, 0, n_u - 1), i, 0)),
        ],
        out_shape=[
            jax.ShapeDtypeStruct((t, D_ATTN), BF16),
            jax.ShapeDtypeStruct((t, 4 * D_KV), BF16),
            jax.ShapeDtypeStruct((N_SLABS, t // CHUNK, SLAB_COLS), BF16),
        ],
        scratch_shapes=[pltpu.VMEM((tm, d), BF16), pltpu.VMEM((slabs_per_blk, tm, V7X_LANES), F32),
                        pltpu.VMEM((1, d), F32)],
        compiler_params=pltpu.CompilerParams(
            dimension_semantics=("arbitrary", "arbitrary"),
            vmem_limit_bytes=_vmem_limit(est)),
        name="in_proj",
    )(x2d, mod3, g1, w_in, cos_t, sin_t)


def _attn_kernel(sink_ref, q_ref, kvc_ref, kvp_ref, g_ref, o_ref, o_scr, cap_scr):
    n = pl.program_id(1)
    blk = WINDOW
    pair_w = 2 * HEAD_DIM
    n_keys = 2 * blk

    n_pairs = Q_PER_KV // 2
    rows = n_pairs * blk

    @pl.when((pl.program_id(0) == 0) & (n == 0))
    def _():
        qi = lax.broadcasted_iota(jnp.int32, (blk, 2 * n_keys), 0)
        key = lax.broadcasted_iota(jnp.int32, (blk, 2 * n_keys), 1) % n_keys
        rel = qi + blk - key
        band = (rel >= 0) & (rel < WINDOW)
        cap_scr[0] = jnp.where(band & (key >= blk), jnp.inf, F32(-1e30))
        cap_scr[1] = jnp.where(band, jnp.inf, F32(-1e30))

    cap = cap_scr[jnp.minimum(n, 1)]
    pair_of_row = lax.broadcasted_iota(jnp.int32, (rows, 1), 0) // blk

    lane = lax.broadcasted_iota(jnp.int32, (n_keys, pair_w), 1)
    left = lane < HEAD_DIM
    left_o = lax.broadcasted_iota(jnp.int32, (rows, pair_w), 1) < HEAD_DIM
    zero = jnp.zeros((n_keys, pair_w), BF16)

    for h in range(N_KV_HEADS):
        kcol = h * pair_w
        vcol = N_KV_HEADS * pair_w + h * pair_w
        kd = jnp.concatenate([kvp_ref[:, kcol:kcol + pair_w], kvc_ref[:, kcol:kcol + pair_w]], axis=0)
        vd = jnp.concatenate([kvp_ref[:, vcol:vcol + pair_w], kvc_ref[:, vcol:vcol + pair_w]], axis=0)
        k_bd = jnp.concatenate([jnp.where(left, kd, zero), jnp.where(left, zero, kd)], axis=0)
        v_bd = jnp.concatenate([jnp.where(left, vd, zero), jnp.where(left, zero, vd)], axis=0)
        head0 = h * Q_PER_KV
        q4 = jnp.concatenate([q_ref[:, (head0 + 2 * p) * HEAD_DIM:(head0 + 2 * p + 2) * HEAD_DIM]
                              for p in range(n_pairs)], axis=0)
        sink_a = jnp.zeros((rows, 1), F32)
        sink_b = jnp.zeros((rows, 1), F32)
        for p in range(n_pairs):
            sink_a = jnp.where(pair_of_row == p, sink_ref[head0 + 2 * p], sink_a)
            sink_b = jnp.where(pair_of_row == p, sink_ref[head0 + 2 * p + 1], sink_b)
        s = lax.dot_general(q4, k_bd, (((1,), (1,)), ((), ())), preferred_element_type=F32)
        s = jnp.minimum(s.reshape(n_pairs, blk, 2 * n_keys), cap[None]).reshape(rows, 2 * n_keys)
        m_a = jnp.maximum(jnp.max(s[:, :n_keys], axis=-1, keepdims=True), sink_a)
        m_b = jnp.maximum(jnp.max(s[:, n_keys:], axis=-1, keepdims=True), sink_b)
        p_a = jnp.exp(s[:, :n_keys] - m_a)
        p_b = jnp.exp(s[:, n_keys:] - m_b)
        l_a = jnp.sum(p_a, axis=-1, keepdims=True) + jnp.exp(sink_a - m_a)
        l_b = jnp.sum(p_b, axis=-1, keepdims=True) + jnp.exp(sink_b - m_b)
        pp = jnp.concatenate([p_a, p_b], axis=1).astype(BF16)
        o4 = jnp.dot(pp, v_bd, preferred_element_type=F32) * jnp.where(left_o, 1.0 / l_a, 1.0 / l_b)
        for p in range(n_pairs):
            qcol = (head0 + 2 * p) * HEAD_DIM
            o_scr[:, qcol:qcol + pair_w] = o4[p * blk:(p + 1) * blk, :]

    o = o_scr[...]
    o_ref[...] = (o * _rms_scale(o, D_ATTN) * g_ref[...]).astype(BF16)


def _attention(q, kvd, sinks, g_attn, batch, seq):
    t = q.shape[0]
    nb = seq // WINDOW
    kvw = kvd.shape[1]
    return pl.pallas_call(
        _attn_kernel,
        grid=(batch, nb),
        in_specs=[
            pl.BlockSpec(memory_space=pltpu.SMEM),
            pl.BlockSpec((WINDOW, D_ATTN), lambda b, n: (b * nb + n, 0)),
            pl.BlockSpec((WINDOW, kvw), lambda b, n: (b * nb + n, 0)),
            pl.BlockSpec((WINDOW, kvw), lambda b, n: (b * nb + jnp.maximum(n - 1, 0), 0)),
            pl.BlockSpec((1, D_ATTN), lambda b, n: (0, 0)),
        ],
        out_specs=pl.BlockSpec((WINDOW, D_ATTN), lambda b, n: (b * nb + n, 0)),
        out_shape=jax.ShapeDtypeStruct((t, D_ATTN), BF16),
        scratch_shapes=[pltpu.VMEM((WINDOW, D_ATTN), F32), pltpu.VMEM((2, WINDOW, 4 * WINDOW), F32)],
        compiler_params=pltpu.CompilerParams(
            dimension_semantics=("arbitrary", "arbitrary"),
            vmem_limit_bytes=_vmem_limit(8 * 1024 * 1024)),
        name="attention",
    )(sinks, q, kvd, kvd, g_attn)


def _cmul(a_re, a_im, b_re, b_im):
    return a_re * b_re - a_im * b_im, a_re * b_im + a_im * b_re


def _ssm_kernel(u_ref, rev_ref, bpow_ref, cpow_ref, mu_ref, o_ref, z_scr, sp_scr, *, seg_chunks):
    n_k = SLAB_STATE // V7X_LANES
    rows = u_ref.shape[1]
    seqs = rows // seg_chunks
    shape = (seqs, V7X_LANES)

    pitch = z_scr.shape[1] // seqs

    for nb in range(2 * SLAB_STATE // TILE):
        zz = jnp.dot(u_ref[0], bpow_ref[0, :, nb * TILE:(nb + 1) * TILE], preferred_element_type=F32)
        for q in range(seqs):
            src = slice(q * seg_chunks, (q + 1) * seg_chunks)
            dst = slice(q * pitch, q * pitch + seg_chunks)
            z_scr[2 * nb, dst, :] = zz[src, :V7X_LANES]
            z_scr[2 * nb + 1, dst, :] = zz[src, V7X_LANES:]

    mu_re = [jnp.broadcast_to(mu_ref[0, 0:1, k * V7X_LANES:(k + 1) * V7X_LANES], shape) for k in range(n_k)]
    mu_im = [jnp.broadcast_to(mu_ref[0, 1:2, k * V7X_LANES:(k + 1) * V7X_LANES], shape) for k in range(n_k)]

    def rows_at(c):
        return pl.ds(c, seqs, stride=pitch)

    def scan_step(c, carry):
        new = []
        for k in range(n_k):
            s_re, s_im = carry[2 * k], carry[2 * k + 1]
            sp_scr[k, rows_at(c), :] = s_re
            sp_scr[n_k + k, rows_at(c), :] = s_im
            p_re, p_im = _cmul(mu_re[k], mu_im[k], s_re, s_im)
            new += [p_re + z_scr[k, rows_at(c), :], p_im + z_scr[n_k + k, rows_at(c), :]]
        return tuple(new)

    zeros = jnp.zeros(shape, F32)
    final = lax.fori_loop(0, seg_chunks, scan_step, (zeros,) * (2 * n_k), unroll=SCAN_UNROLL)

    first_seg = (lax.broadcasted_iota(jnp.int32, shape, 0) % N_SEG) == 0
    init = []
    for k in range(n_k):
        m_re, m_im = mu_re[k], mu_im[k]
        for _ in range(int(math.log2(seg_chunks))):
            m_re, m_im = _cmul(m_re, m_im, m_re, m_im)
        i_re, i_im = zeros, zeros
        for _ in range(N_SEG - 1):
            t_re, t_im = _cmul(m_re, m_im, i_re, i_im)
            i_re = jnp.where(first_seg, 0.0, pltpu.roll(t_re + final[2 * k], 1, 0))
            i_im = jnp.where(first_seg, 0.0, pltpu.roll(t_im + final[2 * k + 1], 1, 0))
        init += [i_re, i_im]

    def fix_step(c, carry):
        new = []
        for k in range(n_k):
            c_re, c_im = carry[2 * k], carry[2 * k + 1]
            sp_scr[k, rows_at(c), :] += c_re
            sp_scr[n_k + k, rows_at(c), :] += c_im
            new += list(_cmul(mu_re[k], mu_im[k], c_re, c_im))
        return tuple(new)

    lax.fori_loop(0, seg_chunks, fix_step, tuple(init), unroll=SCAN_UNROLL)

    sp = jnp.concatenate(
        [jnp.concatenate([sp_scr[k, q * pitch:q * pitch + seg_chunks, :] for q in range(seqs)], axis=0)
         for k in range(2 * n_k)], axis=1).astype(BF16)
    for jt in range(SLAB_COLS // TILE):
        k_len = (jt + 1) * TILE
        y = jnp.dot(u_ref[0, :, :k_len], rev_ref[0, SLAB_COLS - k_len:, :], preferred_element_type=F32)
        y = y + jnp.dot(sp, cpow_ref[0, :, jt * TILE:(jt + 1) * TILE], preferred_element_type=F32)
        y = 0.5 * y * (1.0 + lax.erf(y * (2.0 ** -0.5)))
        for q in range(STEPS_PER_TILE):
            step = jt * STEPS_PER_TILE + q
            o_ref[pl.ds(step, rows, stride=CHUNK), :] = y[:, q * V7X_LANES:(q + 1) * V7X_LANES]


def _ssm(u_c, rev, bpow, cpow, mu, bsz, seq):
    n_slabs, chunk_rows, _ = u_c.shape
    t = chunk_rows * CHUNK
    rows = chunk_rows // SSM_BATCH_SPLIT
    seg_chunks = seq // (CHUNK * N_SEG)
    assert bsz % SSM_BATCH_SPLIT == 0 and rows // seg_chunks == V7X_SUBLANES
    assert seg_chunks & (seg_chunks - 1) == 0
    assert (seg_chunks // V7X_SUBLANES) % 2 == 0
    scan_rows = V7X_SUBLANES * (seg_chunks + V7X_SUBLANES)
    kern = functools.partial(_ssm_kernel, seg_chunks=seg_chunks)
    est = (2 * rows * SLAB_COLS * 2 + 2 * SLAB_COLS * TILE * 2 + 4 * SLAB_COLS * 2 * SLAB_STATE * 2
           + 2 * rows * CHUNK * V7X_LANES * 4 + 2 * rows * 2 * SLAB_STATE * 4 + 8 * rows * TILE * 4
           + rows * 2 * SLAB_STATE * 2)
    return pl.pallas_call(
        kern,
        grid=(n_slabs, SSM_BATCH_SPLIT),
        in_specs=[
            pl.BlockSpec((1, rows, SLAB_COLS), lambda s, h: (s, h, 0)),
            pl.BlockSpec((1, SLAB_COLS, TILE), lambda s, h: (s, 0, 0)),
            pl.BlockSpec((1, SLAB_COLS, 2 * SLAB_STATE), lambda s, h: (s, 0, 0)),
            pl.BlockSpec((1, 2 * SLAB_STATE, SLAB_COLS), lambda s, h: (s, 0, 0)),
            pl.BlockSpec((1, 2, SLAB_STATE), lambda s, h: (s, 0, 0)),
        ],
        out_specs=pl.BlockSpec((rows * CHUNK, V7X_LANES), lambda s, h: (h, s)),
        out_shape=jax.ShapeDtypeStruct((t, n_slabs * V7X_LANES), F32),
        scratch_shapes=[pltpu.VMEM((2 * SLAB_STATE // V7X_LANES, scan_rows, V7X_LANES), F32),
                        pltpu.VMEM((2 * SLAB_STATE // V7X_LANES, scan_rows, V7X_LANES), F32)],
        compiler_params=pltpu.CompilerParams(
            dimension_semantics=("arbitrary", "arbitrary"),
            vmem_limit_bytes=_vmem_limit(est)),
        name="ssm",
    )(u_c, rev, bpow, cpow, mu)


def _zoh(lam_re, lam_im, log_step):
    step = jnp.exp(log_step)
    mag = jnp.exp(lam_re * step)
    bar_re = mag * jnp.cos(lam_im * step)
    bar_im = mag * jnp.sin(lam_im * step)
    num_re, num_im = bar_re - 1.0, bar_im
    inv_den = 1.0 / (lam_re * lam_re + lam_im * lam_im)
    coef_re = (num_re * lam_re + num_im * lam_im) * inv_den
    coef_im = (num_im * lam_re - num_re * lam_im) * inv_den
    return bar_re, bar_im, coef_re, coef_im


def _ssm_ops_kernel(lam_r_ref, lam_c_ref, bt_re_ref, bt_im_ref, ct_re_ref, ct_im_ref, d_ref,
                    rev_ref, bpow_ref, cpow_ref, mu_ref, km_scr):
    lanes = V7X_LANES
    bar_re, bar_im, coef_re, coef_im = _zoh(lam_r_ref[0, 0:1, :], lam_r_ref[0, 1:2, :], lam_r_ref[0, 2:3, :])
    bb_re, bb_im = _cmul(bt_re_ref[0], bt_im_ref[0], coef_re, coef_im)
    p_re, p_im = jnp.ones_like(bar_re), jnp.zeros_like(bar_re)
    for jp in range(CHUNK - 1, -1, -1):
        r0 = jp * lanes
        blk_re, blk_im = _cmul(bb_re, bb_im, p_re, p_im)
        bpow_ref[0, r0:r0 + lanes, :SLAB_STATE] = blk_re.astype(BF16)
        bpow_ref[0, r0:r0 + lanes, SLAB_STATE:] = blk_im.astype(BF16)
        p_re, p_im = _cmul(p_re, p_im, bar_re, bar_im)
    mu_ref[0, 0:1, :] = p_re
    mu_ref[0, 1:2, :] = p_im

    col = (SLAB_STATE, lanes)
    lam_c = lam_c_ref[0]
    cbar_re, cbar_im, _, _ = _zoh(*(jnp.broadcast_to(lam_c[:, i:i + 1], col) for i in range(3)))
    w_re, w_im = ct_re_ref[0], ct_im_ref[0]
    row_i = lax.broadcasted_iota(jnp.int32, (lanes, lanes), 0)
    col_i = lax.broadcasted_iota(jnp.int32, (lanes, lanes), 1)
    skip = jnp.where(row_i == col_i, d_ref[0], 0.0)
    hi = lax.Precision.HIGHEST
    for m in range(CHUNK):
        km = (jnp.dot(bb_re, w_re, preferred_element_type=F32, precision=hi)
              - jnp.dot(bb_im, w_im, preferred_element_type=F32, precision=hi))
        km_scr[m] = km + skip if m == 0 else km
        w_re, w_im = _cmul(w_re, w_im, cbar_re, cbar_im)
        cpow_ref[0, :SLAB_STATE, m * lanes:(m + 1) * lanes] = w_re.astype(BF16)
        cpow_ref[0, SLAB_STATE:, m * lanes:(m + 1) * lanes] = (-w_im).astype(BF16)

    assert STEPS_PER_TILE == 2
    n_tiles = SLAB_COLS // TILE
    for i in range(n_tiles):
        d = n_tiles - 1 - i
        r0 = i * TILE
        below = km_scr[2 * d - 1] if d > 0 else jnp.zeros((lanes, lanes), F32)
        rev_ref[0, r0:r0 + lanes, :lanes] = km_scr[2 * d].astype(BF16)
        rev_ref[0, r0:r0 + lanes, lanes:] = km_scr[2 * d + 1].astype(BF16)
        rev_ref[0, r0 + lanes:r0 + TILE, :lanes] = below.astype(BF16)
        rev_ref[0, r0 + lanes:r0 + TILE, lanes:] = km_scr[2 * d].astype(BF16)


def _ssm_operators(lam_re, lam_im, log_step, b_re, b_im, c_re, c_im, d_skip):
    s, sg = N_SLABS, SLAB_GROUPS
    eye_g = jnp.eye(sg, dtype=F32)

    def per_state(a):
        return a.reshape(s, SLAB_STATE)

    step_gp = jnp.broadcast_to(log_step[:, None], lam_re.shape)
    lam_rows = jnp.stack([per_state(lam_re), per_state(lam_im), per_state(step_gp)], axis=1)
    lam_cols = lam_rows.transpose(0, 2, 1)

    def block_diag_bt(b):
        bt = b.reshape(s, sg, STATE, SSM_GROUP).transpose(0, 1, 3, 2)
        return (bt[:, :, :, None, :] * eye_g[None, :, None, :, None]).reshape(s, V7X_LANES, SLAB_STATE)

    def block_diag_ct(c):
        ct = c.reshape(s, sg, SSM_GROUP, STATE).transpose(0, 1, 3, 2)
        return (ct[:, :, :, None, :] * eye_g[None, :, None, :, None]).reshape(s, SLAB_STATE, V7X_LANES)

    d_rows = d_skip.reshape(s, 1, V7X_LANES)
    operands = (lam_rows, lam_cols, block_diag_bt(b_re), block_diag_bt(b_im),
                block_diag_ct(c_re), block_diag_ct(c_im), d_rows)
    return pl.pallas_call(
        _ssm_ops_kernel,
        grid=(s,),
        in_specs=[
            pl.BlockSpec((1, 3, SLAB_STATE), lambda i: (i, 0, 0)),
            pl.BlockSpec((1, SLAB_STATE, 3), lambda i: (i, 0, 0)),
            pl.BlockSpec((1, V7X_LANES, SLAB_STATE), lambda i: (i, 0, 0)),
            pl.BlockSpec((1, V7X_LANES, SLAB_STATE), lambda i: (i, 0, 0)),
            pl.BlockSpec((1, SLAB_STATE, V7X_LANES), lambda i: (i, 0, 0)),
            pl.BlockSpec((1, SLAB_STATE, V7X_LANES), lambda i: (i, 0, 0)),
            pl.BlockSpec((1, 1, V7X_LANES), lambda i: (i, 0, 0)),
        ],
        out_specs=[
            pl.BlockSpec((1, SLAB_COLS, TILE), lambda i: (i, 0, 0)),
            pl.BlockSpec((1, SLAB_COLS, 2 * SLAB_STATE), lambda i: (i, 0, 0)),
            pl.BlockSpec((1, 2 * SLAB_STATE, SLAB_COLS), lambda i: (i, 0, 0)),
            pl.BlockSpec((1, 2, SLAB_STATE), lambda i: (i, 0, 0)),
        ],
        out_shape=[
            jax.ShapeDtypeStruct((s, SLAB_COLS, TILE), BF16),
            jax.ShapeDtypeStruct((s, SLAB_COLS, 2 * SLAB_STATE), BF16),
            jax.ShapeDtypeStruct((s, 2 * SLAB_STATE, SLAB_COLS), BF16),
            jax.ShapeDtypeStruct((s, 2, SLAB_STATE), F32),
        ],
        scratch_shapes=[pltpu.VMEM((CHUNK, V7X_LANES, V7X_LANES), F32)],
        compiler_params=pltpu.CompilerParams(
            dimension_semantics=("arbitrary",),
            vmem_limit_bytes=_vmem_limit(2 * (SLAB_COLS * TILE + 4 * SLAB_COLS * SLAB_STATE) * 2
                                         + 24 * SLAB_STATE * V7X_LANES * 4)),
        name="ssm_ops",
    )(*operands)


def _glu_kernel(y_ref, w_ref, b_ref, g_ref, o_ref):
    y = y_ref[...]
    z = jnp.dot(y.astype(BF16), w_ref[...], preferred_element_type=F32) + b_ref[...]
    out = y * jax.nn.sigmoid(z)
    o_ref[...] = (out * _rms_scale(out, D_SSM) * g_ref[...]).astype(BF16)


def _glu(y, w_glu, b_glu, g_ssm, tm=512):
    t, d = y.shape
    est = 2 * tm * d * 4 + 2 * tm * d * 2 + 2 * d * d * 2 + 4 * tm * d * 4
    return pl.pallas_call(
        _glu_kernel,
        grid=(t // tm,),
        in_specs=[
            pl.BlockSpec((tm, d), lambda i: (i, 0)),
            pl.BlockSpec((d, d), lambda i: (0, 0)),
            pl.BlockSpec((1, d), lambda i: (0, 0)),
            pl.BlockSpec((1, d), lambda i: (0, 0)),
        ],
        out_specs=pl.BlockSpec((tm, d), lambda i: (i, 0)),
        out_shape=jax.ShapeDtypeStruct((t, d), BF16),
        compiler_params=pltpu.CompilerParams(
            dimension_semantics=("arbitrary",),
            vmem_limit_bytes=_vmem_limit(est)),
        name="glu",
    )(y, w_glu, b_glu, g_ssm)


def _out_proj_kernel(a_ref, s_ref, wa_ref, ws_ref, x_ref, mod_ref, o_ref):
    acc = jnp.dot(a_ref[...], wa_ref[...], preferred_element_type=F32)
    acc = acc + jnp.dot(s_ref[...], ws_ref[...], preferred_element_type=F32)
    o_ref[...] = x_ref[...] + mod_ref[0, 2:3, :] * acc


def _out_proj(attn_n, ssm_n, w_out, x2d, mod3, seq, tm=512, tn=1024):
    t, d = x2d.shape
    nb = seq // tm
    ka = attn_n.shape[1]
    est = 2 * 2 * tm * ka * 2 + 2 * 2 * ka * tn * 2 + 4 * tm * tn * 4 + 2 * tm * tn * 4
    return pl.pallas_call(
        _out_proj_kernel,
        grid=(t // tm, d // tn),
        in_specs=[
            pl.BlockSpec((tm, ka), lambda i, j: (i, 0)),
            pl.BlockSpec((tm, ka), lambda i, j: (i, 0)),
            pl.BlockSpec((ka, tn), lambda i, j: (0, j)),
            pl.BlockSpec((ka, tn), lambda i, j: (1, j)),
            pl.BlockSpec((tm, tn), lambda i, j: (i, j)),
            pl.BlockSpec((1, N_MOD, tn), lambda i, j: (i // nb, 0, j)),
        ],
        out_specs=pl.BlockSpec((tm, tn), lambda i, j: (i, j)),
        out_shape=jax.ShapeDtypeStruct((t, d), F32),
        compiler_params=pltpu.CompilerParams(
            dimension_semantics=("arbitrary", "arbitrary"),
            vmem_limit_bytes=_vmem_limit(est)),
        name="out_proj",
    )(attn_n, ssm_n, w_out, w_out, x2d, mod3)


def _ffn_kernel(x_ref, mod_ref, g2_ref, w1_ref, w2_ref, gf_ref, o_ref, h_scr, gain_scr, *, n_chunk):
    k = pl.program_id(1)

    @pl.when(k == 0)
    def _():
        gain_scr[...] = g2_ref[...] * (1.0 + mod_ref[0, 4:5, :])
        _modulated_norm(x_ref, gain_scr, mod_ref.at[0, 3:4, :], h_scr)
        o_ref[...] = jnp.zeros_like(o_ref)

    a = jnp.dot(h_scr[...], w1_ref[...], preferred_element_type=F32)
    a = jnp.square(jnp.maximum(a, 0.0)).astype(BF16)

    for n0 in range(0, D_MODEL, n_chunk):
        o_ref[:, n0:n0 + n_chunk] += jnp.dot(a, w2_ref[:, n0:n0 + n_chunk], preferred_element_type=F32)

    @pl.when(k == pl.num_programs(1) - 1)
    def _():
        def body(r, carry):
            rows = pl.ds(pl.multiple_of(r * V7X_SUBLANES, V7X_SUBLANES), V7X_SUBLANES)
            x2 = x_ref[rows, :] + mod_ref[0, 5:6, :] * o_ref[rows, :]
            o_ref[rows, :] = x2 * _rms_scale(x2, D_MODEL) * gf_ref[...]
            return carry

        lax.fori_loop(0, o_ref.shape[0] // V7X_SUBLANES, body, 0, unroll=NORM_UNROLL)


def _ffn(x1, mod3, g2, w1, w2, gf, seq, tm=512, tf=512):
    t, d = x1.shape
    nb = seq // tm
    f = w1.shape[1]
    est = tm * d * 4 + 2 * tm * d * 4 + tm * d * 2 + 2 * d * tf * 2 + 2 * tf * d * 2 + 4 * tm * tf * 4
    return pl.pallas_call(
        functools.partial(_ffn_kernel, n_chunk=tf),
        grid=(t // tm, f // tf),
        in_specs=[
            pl.BlockSpec((tm, d), lambda i, k: (i, 0), pipeline_mode=pl.Buffered(1)),
            pl.BlockSpec((1, N_MOD, d), lambda i, k: (i // nb, 0, 0)),
            pl.BlockSpec((1, d), lambda i, k: (0, 0)),
            pl.BlockSpec((d, tf), lambda i, k: (0, k)),
            pl.BlockSpec((tf, d), lambda i, k: (k, 0)),
            pl.BlockSpec((1, d), lambda i, k: (0, 0)),
        ],
        out_specs=pl.BlockSpec((tm, d), lambda i, k: (i, 0)),
        out_shape=jax.ShapeDtypeStruct((t, d), F32),
        scratch_shapes=[pltpu.VMEM((tm, d), BF16), pltpu.VMEM((1, d), F32)],
        compiler_params=pltpu.CompilerParams(
            dimension_semantics=("arbitrary", "arbitrary"),
            vmem_limit_bytes=_vmem_limit(est)),
        name="ffn",
    )(x1, mod3, g2, w1, w2, gf)


def _rope_tables(seq):
    half = HEAD_DIM // 2
    inv_freq = ROPE_THETA ** (-jnp.arange(half, dtype=F32) / half)
    ang = jnp.arange(seq, dtype=F32)[:, None] * inv_freq[None, :]
    cos = jnp.cos(ang)
    sin = jnp.sin(ang)
    reps = V7X_LANES // HEAD_DIM
    cos_t = jnp.tile(jnp.concatenate([cos, cos], axis=1), (1, reps))
    sin_t = jnp.tile(jnp.concatenate([-sin, sin], axis=1), (1, reps))
    return cos_t, sin_t


def kernel(x, c, w_ada, b_ada, norm1_g, w_in, sinks, ssm_lam_re, ssm_lam_im, ssm_log_step, ssm_b_re, ssm_b_im, ssm_c_re, ssm_c_im, ssm_d, w_glu, b_glu, attn_out_g, ssm_out_g, w_out, norm2_g, w_ff1, w_ff2, final_g):
    bsz, seq, d = x.shape
    t = bsz * seq
    x2d = x.reshape(t, d)

    c_pad = jnp.pad(c, ((0, V7X_SUBLANES - bsz), (0, 0)))
    mod = _adaln(c_pad, w_ada[0], b_ada[0].reshape(1, -1))
    mod3 = mod[:bsz].reshape(bsz, N_MOD, d)

    cos_t, sin_t = _rope_tables(seq)
    q, kvd, u_c = _in_proj(x2d, mod3, norm1_g[0].reshape(1, d), w_in[0].astype(BF16), cos_t, sin_t, seq)

    attn_n = _attention(q, kvd, sinks[0], attn_out_g[0].reshape(1, -1), bsz, seq)

    rev, bpow, cpow, mu = _ssm_operators(ssm_lam_re[0], ssm_lam_im[0], ssm_log_step[0], ssm_b_re[0],
                                         ssm_b_im[0], ssm_c_re[0], ssm_c_im[0], ssm_d[0])
    y = _ssm(u_c, rev, bpow, cpow, mu, bsz, seq)
    ssm_n = _glu(y, w_glu[0].astype(BF16), b_glu[0].reshape(1, -1), ssm_out_g[0].reshape(1, -1))

    x1 = _out_proj(attn_n, ssm_n, w_out[0].astype(BF16), x2d, mod3, seq)
    out = _ffn(x1, mod3, norm2_g[0].reshape(1, d), w_ff1[0].astype(BF16), w_ff2[0].astype(BF16),
               final_g.reshape(1, d), seq)
    return out.reshape(bsz, seq, d)
```

```python
import functools
import math

import jax
import jax.numpy as jnp
from jax import lax
from jax.experimental import pallas as pl
from jax.experimental.pallas import tpu as pltpu

D_MODEL = 4096
D_ATTN = 2048
D_SSM = 2048
HEAD_DIM = 64
N_Q_HEADS = 32
N_KV_HEADS = 4
Q_PER_KV = 8
D_KV = 256
WINDOW = 128
ROPE_THETA = 10000.0
SSM_GROUP = 16
N_SSM_GROUPS = 128
STATE = 64
D_FF = 4 * D_MODEL
N_MOD = 6
EPS = 1e-6

V7X_LANES = 128
V7X_SUBLANES = 8
V7X_VMEM_BYTES = 64 * 1024 * 1024

CHUNK = 16
N_SEG = 4
SSM_BATCH_SPLIT = 2
SLAB_GROUPS = V7X_LANES // SSM_GROUP
N_SLABS = N_SSM_GROUPS // SLAB_GROUPS
SLAB_COLS = CHUNK * V7X_LANES
SLAB_STATE = SLAB_GROUPS * STATE
TILE = 256
STEPS_PER_TILE = TILE // V7X_LANES
SCAN_UNROLL = 8

BF16 = jnp.bfloat16
F32 = jnp.float32


def _vmem_limit(nbytes):
    return int(min(nbytes + 8 * 1024 * 1024, V7X_VMEM_BYTES - 4 * 1024 * 1024))


def _rms_scale(xf, width):
    return lax.rsqrt(jnp.sum(xf * xf, axis=-1, keepdims=True) * (1.0 / width) + EPS)


NORM_ROWS = 16
NORM_UNROLL = 4


def _modulated_norm(x_ref, gain_ref, shift_ref, out_ref):
    width = x_ref.shape[1]

    def body(r, carry):
        rows = pl.ds(pl.multiple_of(r * NORM_ROWS, NORM_ROWS), NORM_ROWS)
        xf = x_ref[rows, :]
        out_ref[rows, :] = (xf * _rms_scale(xf, width) * gain_ref[...] + shift_ref[...]).astype(out_ref.dtype)
        return carry

    lax.fori_loop(0, x_ref.shape[0] // NORM_ROWS, body, 0, unroll=NORM_UNROLL)


def _adaln_kernel(c_ref, w_ref, b_ref, o_ref):
    c = c_ref[...]
    ca = (c * jax.nn.sigmoid(c)).astype(BF16)
    acc = jnp.dot(ca, w_ref[...].astype(BF16), preferred_element_type=F32)
    o_ref[...] = acc + b_ref[...]


def _adaln(c_pad, w_ada, b_ada, tn=1024):
    m, d = c_pad.shape
    n = w_ada.shape[1]
    return pl.pallas_call(
        _adaln_kernel,
        grid=(n // tn,),
        in_specs=[
            pl.BlockSpec((m, d), lambda j: (0, 0)),
            pl.BlockSpec((d, tn), lambda j: (0, j)),
            pl.BlockSpec((1, tn), lambda j: (0, j)),
        ],
        out_specs=pl.BlockSpec((m, tn), lambda j: (0, j)),
        out_shape=jax.ShapeDtypeStruct((m, n), F32),
        compiler_params=pltpu.CompilerParams(
            dimension_semantics=("arbitrary",),
            vmem_limit_bytes=_vmem_limit(2 * d * tn * 4 + d * tn * 2)),
        name="adaln",
    )(c_pad, w_ada, b_ada)


def _rope(acc, cos, sin_signed):
    width = acc.shape[1]
    lane = lax.broadcasted_iota(jnp.int32, acc.shape, 1)
    first_half = (lane % HEAD_DIM) < (HEAD_DIM // 2)
    partner = jnp.where(first_half,
                        pltpu.roll(acc, width - HEAD_DIM // 2, 1),
                        pltpu.roll(acc, HEAD_DIM // 2, 1))
    reps = width // cos.shape[1]
    cos_w = jnp.concatenate([cos] * reps, axis=1)
    sin_w = jnp.concatenate([sin_signed] * reps, axis=1)
    return acc * cos_w + partner * sin_w


def _in_proj_kernel(x_ref, mod_ref, g_ref, w_ref, cos_ref, sin_ref,
                    q_ref, kv_ref, u_ref, h_scr, u_scr, gain_scr, *, n_q_blocks):
    j = pl.program_id(1)

    @pl.when(j == 0)
    def _():
        gain_scr[...] = g_ref[...] * (1.0 + mod_ref[0, 1:2, :])
        _modulated_norm(x_ref, gain_scr, mod_ref.at[0, 0:1, :], h_scr)

    acc = jnp.dot(h_scr[...], w_ref[...], preferred_element_type=F32)

    @pl.when(j < n_q_blocks)
    def _():
        q_ref[...] = (_rope(acc, cos_ref[...], sin_ref[...]) * (HEAD_DIM ** -0.5)).astype(BF16)

    @pl.when(j == n_q_blocks)
    def _():
        k = _rope(acc[:, :D_KV], cos_ref[...], sin_ref[...])
        kv = jnp.concatenate([k, acc[:, D_KV:]], axis=1)
        lane = lax.broadcasted_iota(jnp.int32, (kv.shape[0], V7X_LANES), 1)
        left = lane < HEAD_DIM
        pieces = []
        for c0 in range(0, 2 * D_KV, V7X_LANES):
            a = kv[:, c0:c0 + V7X_LANES]
            s = pltpu.roll(a, HEAD_DIM, 1)
            pieces.append(jnp.where(left, a, s))
            pieces.append(jnp.where(left, s, a))
        kv_ref[...] = jnp.concatenate(pieces, axis=1).astype(BF16)

    @pl.when(j > n_q_blocks)
    def _():
        n_slabs, chunk_rows, _ = u_ref.shape
        for s in range(n_slabs):
            u_scr[s] = acc[:, s * V7X_LANES:(s + 1) * V7X_LANES]
        for s in range(n_slabs):
            for step in range(CHUNK):
                rows = u_scr[s, pl.ds(step, chunk_rows, stride=CHUNK), :]
                u_ref[s, :, step * V7X_LANES:(step + 1) * V7X_LANES] = rows.astype(BF16)


def _in_proj(x2d, mod3, g1, w_in, cos_t, sin_t, seq, tm=512, tn=512):
    t, d = x2d.shape
    n_q = D_ATTN // tn
    n_u = D_SSM // tn
    assert 2 * D_KV == tn
    slabs_per_blk = tn // V7X_LANES
    nb = seq // tm
    grid = (t // tm, n_q + 1 + n_u)
    kern = functools.partial(_in_proj_kernel, n_q_blocks=n_q)
    est = (2 * tm * d * 4 + tm * d * 2 + 2 * d * tn * 2 + 2 * tm * tn * 2 * 2
           + 2 * tm * 4 * D_KV * 2 + 4 * tm * V7X_LANES * 4 + 4 * tm * tn * 4)
    return pl.pallas_call(
        kern,
        grid=grid,
        in_specs=[
            pl.BlockSpec((tm, d), lambda i, j: (i, 0)),
            pl.BlockSpec((1, N_MOD, d), lambda i, j: (i // nb, 0, 0)),
            pl.BlockSpec((1, d), lambda i, j: (0, 0)),
            pl.BlockSpec((d, tn), lambda i, j: (0, j)),
            pl.BlockSpec((tm, V7X_LANES), lambda i, j: (i % nb, 0)),
            pl.BlockSpec((tm, V7X_LANES), lambda i, j: (i % nb, 0)),
        ],
        out_specs=[
            pl.BlockSpec((tm, tn), lambda i, j: (i, jnp.minimum(j, n_q - 1))),
            pl.BlockSpec((tm, 4 * D_KV), lambda i, j: (i, 0)),
            pl.BlockSpec((slabs_per_blk, tm // CHUNK, SLAB_COLS),
                         lambda i, j: (jnp.clip(j - n_q - 1, 0, n_u - 1), i, 0)),
        ],
        out_shape=[
            jax.ShapeDtypeStruct((t, D_ATTN), BF16),
            jax.ShapeDtypeStruct((t, 4 * D_KV), BF16),
            jax.ShapeDtypeStruct((N_SLABS, t // CHUNK, SLAB_COLS), BF16),
        ],
        scratch_shapes=[pltpu.VMEM((tm, d), BF16), pltpu.VMEM((slabs_per_blk, tm, V7X_LANES), F32),
                        pltpu.VMEM((1, d), F32)],
        compiler_params=pltpu.CompilerParams(
            dimension_semantics=("arbitrary", "arbitrary"),
            vmem_limit_bytes=_vmem_limit(est)),
        name="in_proj",
    )(x2d, mod3, g1, w_in, cos_t, sin_t)


def _attn_kernel(sink_ref, q_ref, kvc_ref, kvp_ref, g_ref, o_ref, o_scr, cap_scr):
    n = pl.program_id(1)
    blk = WINDOW
    pair_w = 2 * HEAD_DIM
    n_keys = 2 * blk

    n_pairs = Q_PER_KV // 2
    rows = n_pairs * blk

    @pl.when((pl.program_id(0) == 0) & (n == 0))
    def _():
        qi = lax.broadcasted_iota(jnp.int32, (blk, 2 * n_keys), 0)
        key = lax.broadcasted_iota(jnp.int32, (blk, 2 * n_keys), 1) % n_keys
        rel = qi + blk - key
        band = (rel >= 0) & (rel < WINDOW)
        cap_scr[0] = jnp.where(band & (key >= blk), jnp.inf, F32(-1e30))
        cap_scr[1] = jnp.where(band, jnp.inf, F32(-1e30))

    cap = cap_scr[jnp.minimum(n, 1)]
    pair_of_row = lax.broadcasted_iota(jnp.int32, (rows, 1), 0) // blk

    lane = lax.broadcasted_iota(jnp.int32, (n_keys, pair_w), 1)
    left = lane < HEAD_DIM
    left_o = lax.broadcasted_iota(jnp.int32, (rows, pair_w), 1) < HEAD_DIM
    zero = jnp.zeros((n_keys, pair_w), BF16)

    for h in range(N_KV_HEADS):
        kcol = h * pair_w
        vcol = N_KV_HEADS * pair_w + h * pair_w
        kd = jnp.concatenate([kvp_ref[:, kcol:kcol + pair_w], kvc_ref[:, kcol:kcol + pair_w]], axis=0)
        vd = jnp.concatenate([kvp_ref[:, vcol:vcol + pair_w], kvc_ref[:, vcol:vcol + pair_w]], axis=0)
        k_bd = jnp.concatenate([jnp.where(left, kd, zero), jnp.where(left, zero, kd)], axis=0)
        v_bd = jnp.concatenate([jnp.where(left, vd, zero), jnp.where(left, zero, vd)], axis=0)
        head0 = h * Q_PER_KV
        q4 = jnp.concatenate([q_ref[:, (head0 + 2 * p) * HEAD_DIM:(head0 + 2 * p + 2) * HEAD_DIM]
                              for p in range(n_pairs)], axis=0)
        sink_a = jnp.zeros((rows, 1), F32)
        sink_b = jnp.zeros((rows, 1), F32)
        for p in range(n_pairs):
            sink_a = jnp.where(pair_of_row == p, sink_ref[head0 + 2 * p], sink_a)
            sink_b = jnp.where(pair_of_row == p, sink_ref[head0 + 2 * p + 1], sink_b)
        s = lax.dot_general(q4, k_bd, (((1,), (1,)), ((), ())), preferred_element_type=F32)
        s = jnp.minimum(s.reshape(n_pairs, blk, 2 * n_keys), cap[None]).reshape(rows, 2 * n_keys)
        m_a = jnp.maximum(jnp.max(s[:, :n_keys], axis=-1, keepdims=True), sink_a)
        m_b = jnp.maximum(jnp.max(s[:, n_keys:], axis=-1, keepdims=True), sink_b)
        p_a = jnp.exp(s[:, :n_keys] - m_a)
        p_b = jnp.exp(s[:, n_keys:] - m_b)
        l_a = jnp.sum(p_a, axis=-1, keepdims=True) + jnp.exp(sink_a - m_a)
        l_b = jnp.sum(p_b, axis=-1, keepdims=True) + jnp.exp(sink_b - m_b)
        pp = jnp.concatenate([p_a, p_b], axis=1).astype(BF16)
        o4 = jnp.dot(pp, v_bd, preferred_element_type=F32) * jnp.where(left_o, 1.0 / l_a, 1.0 / l_b)
        for p in range(n_pairs):
            qcol = (head0 + 2 * p) * HEAD_DIM
            o_scr[:, qcol:qcol + pair_w] = o4[p * blk:(p + 1) * blk, :]

    o = o_scr[...]
    o_ref[...] = (o * _rms_scale(o, D_ATTN) * g_ref[...]).astype(BF16)


def _attention(q, kvd, sinks, g_attn, batch, seq):
    t = q.shape[0]
    nb = seq // WINDOW
    kvw = kvd.shape[1]
    return pl.pallas_call(
        _attn_kernel,
        grid=(batch, nb),
        in_specs=[
            pl.BlockSpec(memory_space=pltpu.SMEM),
            pl.BlockSpec((WINDOW, D_ATTN), lambda b, n: (b * nb + n, 0)),
            pl.BlockSpec((WINDOW, kvw), lambda b, n: (b * nb + n, 0)),
            pl.BlockSpec((WINDOW, kvw), lambda b, n: (b * nb + jnp.maximum(n - 1, 0), 0)),
            pl.BlockSpec((1, D_ATTN), lambda b, n: (0, 0)),
        ],
        out_specs=pl.BlockSpec((WINDOW, D_ATTN), lambda b, n: (b * nb + n, 0)),
        out_shape=jax.ShapeDtypeStruct((t, D_ATTN), BF16),
        scratch_shapes=[pltpu.VMEM((WINDOW, D_ATTN), F32), pltpu.VMEM((2, WINDOW, 4 * WINDOW), F32)],
        compiler_params=pltpu.CompilerParams(
            dimension_semantics=("arbitrary", "arbitrary"),
            vmem_limit_bytes=_vmem_limit(8 * 1024 * 1024)),
        name="attention",
    )(sinks, q, kvd, kvd, g_attn)


def _cmul(a_re, a_im, b_re, b_im):
    return a_re * b_re - a_im * b_im, a_re * b_im + a_im * b_re


def _ssm_kernel(u_ref, rev_ref, bpow_ref, cpow_ref, mu_ref, o_ref, z_scr, sp_scr, *, seg_chunks):
    n_k = SLAB_STATE // V7X_LANES
    rows = u_ref.shape[1]
    seqs = rows // seg_chunks
    shape = (seqs, V7X_LANES)

    pitch = z_scr.shape[1] // seqs

    for nb in range(2 * SLAB_STATE // TILE):
        zz = jnp.dot(u_ref[0], bpow_ref[0, :, nb * TILE:(nb + 1) * TILE], preferred_element_type=F32)
        for q in range(seqs):
            src = slice(q * seg_chunks, (q + 1) * seg_chunks)
            dst = slice(q * pitch, q * pitch + seg_chunks)
            z_scr[2 * nb, dst, :] = zz[src, :V7X_LANES]
            z_scr[2 * nb + 1, dst, :] = zz[src, V7X_LANES:]

    mu_re = [jnp.broadcast_to(mu_ref[0, 0:1, k * V7X_LANES:(k + 1) * V7X_LANES], shape) for k in range(n_k)]
    mu_im = [jnp.broadcast_to(mu_ref[0, 1:2, k * V7X_LANES:(k + 1) * V7X_LANES], shape) for k in range(n_k)]

    def rows_at(c):
        return pl.ds(c, seqs, stride=pitch)

    def scan_step(c, carry):
        new = []
        for k in range(n_k):
            s_re, s_im = carry[2 * k], carry[2 * k + 1]
            sp_scr[k, rows_at(c), :] = s_re
            sp_scr[n_k + k, rows_at(c), :] = s_im
            p_re, p_im = _cmul(mu_re[k], mu_im[k], s_re, s_im)
            new += [p_re + z_scr[k, rows_at(c), :], p_im + z_scr[n_k + k, rows_at(c), :]]
        return tuple(new)

    zeros = jnp.zeros(shape, F32)
    final = lax.fori_loop(0, seg_chunks, scan_step, (zeros,) * (2 * n_k), unroll=SCAN_UNROLL)

    first_seg = (lax.broadcasted_iota(jnp.int32, shape, 0) % N_SEG) == 0
    init = []
    for k in range(n_k):
        m_re, m_im = mu_re[k], mu_im[k]
        for _ in range(int(math.log2(seg_chunks))):
            m_re, m_im = _cmul(m_re, m_im, m_re, m_im)
        i_re, i_im = zeros, zeros
        for _ in range(N_SEG - 1):
            t_re, t_im = _cmul(m_re, m_im, i_re, i_im)
            i_re = jnp.where(first_seg, 0.0, pltpu.roll(t_re + final[2 * k], 1, 0))
            i_im = jnp.where(first_seg, 0.0, pltpu.roll(t_im + final[2 * k + 1], 1, 0))
        init += [i_re, i_im]

    def fix_step(c, carry):
        new = []
        for k in range(n_k):
            c_re, c_im = carry[2 * k], carry[2 * k + 1]
            sp_scr[k, rows_at(c), :] += c_re
            sp_scr[n_k + k, rows_at(c), :] += c_im
            new += list(_cmul(mu_re[k], mu_im[k], c_re, c_im))
        return tuple(new)

    lax.fori_loop(0, seg_chunks, fix_step, tuple(init), unroll=SCAN_UNROLL)

    sp = jnp.concatenate(
        [jnp.concatenate([sp_scr[k, q * pitch:q * pitch + seg_chunks, :] for q in range(seqs)], axis=0)
         for k in range(2 * n_k)], axis=1).astype(BF16)
    for jt in range(SLAB_COLS // TILE):
        k_len = (jt + 1) * TILE
        y = jnp.dot(u_ref[0, :, :k_len], rev_ref[0, SLAB_COLS - k_len:, :], preferred_element_type=F32)
        y = y + jnp.dot(sp, cpow_ref[0, :, jt * TILE:(jt + 1) * TILE], preferred_element_type=F32)
        y = 0.5 * y * (1.0 + lax.erf(y * (2.0 ** -0.5)))
        for q in range(STEPS_PER_TILE):
            step = jt * STEPS_PER_TILE + q
            o_ref[pl.ds(step, rows, stride=CHUNK), :] = y[:, q * V7X_LANES:(q + 1) * V7X_LANES]


def _ssm(u_c, rev, bpow, cpow, mu, bsz, seq):
    n_slabs, chunk_rows, _ = u_c.shape
    t = chunk_rows * CHUNK
    rows = chunk_rows // SSM_BATCH_SPLIT
    seg_chunks = seq // (CHUNK * N_SEG)
    assert bsz % SSM_BATCH_SPLIT == 0 and rows // seg_chunks == V7X_SUBLANES
    assert seg_chunks & (seg_chunks - 1) == 0
    assert (seg_chunks // V7X_SUBLANES) % 2 == 0
    scan_rows = V7X_SUBLANES * (seg_chunks + V7X_SUBLANES)
    kern = functools.partial(_ssm_kernel, seg_chunks=seg_chunks)
    est = (2 * rows * SLAB_COLS * 2 + 2 * SLAB_COLS * TILE * 2 + 4 * SLAB_COLS * 2 * SLAB_STATE * 2
           + 2 * rows * CHUNK * V7X_LANES * 4 + 2 * rows * 2 * SLAB_STATE * 4 + 8 * rows * TILE * 4
           + rows * 2 * SLAB_STATE * 2)
    return pl.pallas_call(
        kern,
        grid=(n_slabs, SSM_BATCH_SPLIT),
        in_specs=[
            pl.BlockSpec((1, rows, SLAB_COLS), lambda s, h: (s, h, 0)),
            pl.BlockSpec((1, SLAB_COLS, TILE), lambda s, h: (s, 0, 0)),
            pl.BlockSpec((1, SLAB_COLS, 2 * SLAB_STATE), lambda s, h: (s, 0, 0)),
            pl.BlockSpec((1, 2 * SLAB_STATE, SLAB_COLS), lambda s, h: (s, 0, 0)),
            pl.BlockSpec((1, 2, SLAB_STATE), lambda s, h: (s, 0, 0)),
        ],
        out_specs=pl.BlockSpec((rows * CHUNK, V7X_LANES), lambda s, h: (h, s)),
        out_shape=jax.ShapeDtypeStruct((t, n_slabs * V7X_LANES), F32),
        scratch_shapes=[pltpu.VMEM((2 * SLAB_STATE // V7X_LANES, scan_rows, V7X_LANES), F32),
                        pltpu.VMEM((2 * SLAB_STATE // V7X_LANES, scan_rows, V7X_LANES), F32)],
        compiler_params=pltpu.CompilerParams(
            dimension_semantics=("arbitrary", "arbitrary"),
            vmem_limit_bytes=_vmem_limit(est)),
        name="ssm",
    )(u_c, rev, bpow, cpow, mu)


def _zoh(lam_re, lam_im, log_step):
    step = jnp.exp(log_step)
    mag = jnp.exp(lam_re * step)
    bar_re = mag * jnp.cos(lam_im * step)
    bar_im = mag * jnp.sin(lam_im * step)
    num_re, num_im = bar_re - 1.0, bar_im
    inv_den = 1.0 / (lam_re * lam_re + lam_im * lam_im)
    coef_re = (num_re * lam_re + num_im * lam_im) * inv_den
    coef_im = (num_im * lam_re - num_re * lam_im) * inv_den
    return bar_re, bar_im, coef_re, coef_im


def _ssm_ops_kernel(lam_r_ref, bt_re_ref, bt_im_ref, ct_re_ref, ct_im_ref, d_ref,
                    rev_ref, bpow_ref, cpow_ref, mu_ref, km_scr):
    lanes = V7X_LANES
    bar_re, bar_im, coef_re, coef_im = _zoh(lam_r_ref[0, 0:1, :], lam_r_ref[0, 1:2, :], lam_r_ref[0, 2:3, :])
    bb_re, bb_im = _cmul(bt_re_ref[0], bt_im_ref[0], coef_re, coef_im)
    p_re, p_im = jnp.ones_like(bar_re), jnp.zeros_like(bar_re)
    for jp in range(CHUNK - 1, -1, -1):
        r0 = jp * lanes
        blk_re, blk_im = _cmul(bb_re, bb_im, p_re, p_im)
        bpow_ref[0, r0:r0 + lanes, :SLAB_STATE] = blk_re.astype(BF16)
        bpow_ref[0, r0:r0 + lanes, SLAB_STATE:] = blk_im.astype(BF16)
        p_re, p_im = _cmul(p_re, p_im, bar_re, bar_im)
    mu_ref[0, 0:1, :] = p_re
    mu_ref[0, 1:2, :] = p_im

    cbar_re = jnp.broadcast_to(bar_re, (lanes, SLAB_STATE)).T
    cbar_im = jnp.broadcast_to(bar_im, (lanes, SLAB_STATE)).T
    w_re, w_im = ct_re_ref[0], ct_im_ref[0]
    row_i = lax.broadcasted_iota(jnp.int32, (lanes, lanes), 0)
    col_i = lax.broadcasted_iota(jnp.int32, (lanes, lanes), 1)
    skip = jnp.where(row_i == col_i, d_ref[0], 0.0)
    bb_cat = jnp.concatenate([bb_re, -bb_im], axis=1).astype(BF16)
    for m in range(CHUNK):
        w_cat = jnp.concatenate([w_re, w_im], axis=0).astype(BF16)
        km = jnp.dot(bb_cat, w_cat, preferred_element_type=F32)
        km_scr[m] = km + skip if m == 0 else km
        w_re, w_im = _cmul(w_re, w_im, cbar_re, cbar_im)
        cpow_ref[0, :SLAB_STATE, m * lanes:(m + 1) * lanes] = w_re.astype(BF16)
        cpow_ref[0, SLAB_STATE:, m * lanes:(m + 1) * lanes] = (-w_im).astype(BF16)

    assert STEPS_PER_TILE == 2
    n_tiles = SLAB_COLS // TILE
    for i in range(n_tiles):
        d = n_tiles - 1 - i
        r0 = i * TILE
        below = km_scr[2 * d - 1] if d > 0 else jnp.zeros((lanes, lanes), F32)
        rev_ref[0, r0:r0 + lanes, :lanes] = km_scr[2 * d].astype(BF16)
        rev_ref[0, r0:r0 + lanes, lanes:] = km_scr[2 * d + 1].astype(BF16)
        rev_ref[0, r0 + lanes:r0 + TILE, :lanes] = below.astype(BF16)
        rev_ref[0, r0 + lanes:r0 + TILE, lanes:] = km_scr[2 * d].astype(BF16)


def _ssm_operators(lam_re, lam_im, log_step, b_re, b_im, c_re, c_im, d_skip):
    s, sg = N_SLABS, SLAB_GROUPS
    eye_g = jnp.eye(sg, dtype=F32)

    def per_state(a):
        return a.reshape(s, SLAB_STATE)

    step_gp = jnp.broadcast_to(log_step[:, None], lam_re.shape)
    lam_rows = jnp.stack([per_state(lam_re), per_state(lam_im), per_state(step_gp)], axis=1)

    def block_diag_bt(b):
        bt = b.reshape(s, sg, STATE, SSM_GROUP).transpose(0, 1, 3, 2)
        return (bt[:, :, :, None, :] * eye_g[None, :, None, :, None]).reshape(s, V7X_LANES, SLAB_STATE)

    def block_diag_ct(c):
        ct = c.reshape(s, sg, SSM_GROUP, STATE).transpose(0, 1, 3, 2)
        return (ct[:, :, :, None, :] * eye_g[None, :, None, :, None]).reshape(s, SLAB_STATE, V7X_LANES)

    d_rows = d_skip.reshape(s, 1, V7X_LANES)
    operands = (lam_rows, block_diag_bt(b_re), block_diag_bt(b_im),
                block_diag_ct(c_re), block_diag_ct(c_im), d_rows)
    return pl.pallas_call(
        _ssm_ops_kernel,
        grid=(s,),
        in_specs=[
            pl.BlockSpec((1, 3, SLAB_STATE), lambda i: (i, 0, 0)),
            pl.BlockSpec((1, V7X_LANES, SLAB_STATE), lambda i: (i, 0, 0)),
            pl.BlockSpec((1, V7X_LANES, SLAB_STATE), lambda i: (i, 0, 0)),
            pl.BlockSpec((1, SLAB_STATE, V7X_LANES), lambda i: (i, 0, 0)),
            pl.BlockSpec((1, SLAB_STATE, V7X_LANES), lambda i: (i, 0, 0)),
            pl.BlockSpec((1, 1, V7X_LANES), lambda i: (i, 0, 0)),
        ],
        out_specs=[
            pl.BlockSpec((1, SLAB_COLS, TILE), lambda i: (i, 0, 0)),
            pl.BlockSpec((1, SLAB_COLS, 2 * SLAB_STATE), lambda i: (i, 0, 0)),
            pl.BlockSpec((1, 2 * SLAB_STATE, SLAB_COLS), lambda i: (i, 0, 0)),
            pl.BlockSpec((1, 2, SLAB_STATE), lambda i: (i, 0, 0)),
        ],
        out_shape=[
            jax.ShapeDtypeStruct((s, SLAB_COLS, TILE), BF16),
            jax.ShapeDtypeStruct((s, SLAB_COLS, 2 * SLAB_STATE), BF16),
            jax.ShapeDtypeStruct((s, 2 * SLAB_STATE, SLAB_COLS), BF16),
            jax.ShapeDtypeStruct((s, 2, SLAB_STATE), F32),
        ],
        scratch_shapes=[pltpu.VMEM((CHUNK, V7X_LANES, V7X_LANES), F32)],
        compiler_params=pltpu.CompilerParams(
            dimension_semantics=("arbitrary",),
            vmem_limit_bytes=_vmem_limit(2 * (SLAB_COLS * TILE + 4 * SLAB_COLS * SLAB_STATE) * 2
                                         + 24 * SLAB_STATE * V7X_LANES * 4)),
        name="ssm_ops",
    )(*operands)


def _glu_kernel(y_ref, w_ref, b_ref, g_ref, o_ref):
    y = y_ref[...]
    z = jnp.dot(y.astype(BF16), w_ref[...], preferred_element_type=F32) + b_ref[...]
    out = y * jax.nn.sigmoid(z)
    o_ref[...] = (out * _rms_scale(out, D_SSM) * g_ref[...]).astype(BF16)


def _glu(y, w_glu, b_glu, g_ssm, tm=512):
    t, d = y.shape
    est = 2 * tm * d * 4 + 2 * tm * d * 2 + 2 * d * d * 2 + 4 * tm * d * 4
    return pl.pallas_call(
        _glu_kernel,
        grid=(t // tm,),
        in_specs=[
            pl.BlockSpec((tm, d), lambda i: (i, 0)),
            pl.BlockSpec((d, d), lambda i: (0, 0)),
            pl.BlockSpec((1, d), lambda i: (0, 0)),
            pl.BlockSpec((1, d), lambda i: (0, 0)),
        ],
        out_specs=pl.BlockSpec((tm, d), lambda i: (i, 0)),
        out_shape=jax.ShapeDtypeStruct((t, d), BF16),
        compiler_params=pltpu.CompilerParams(
            dimension_semantics=("arbitrary",),
            vmem_limit_bytes=_vmem_limit(est)),
        name="glu",
    )(y, w_glu, b_glu, g_ssm)


def _out_proj_kernel(a_ref, s_ref, wa_ref, ws_ref, x_ref, mod_ref, o_ref):
    acc = jnp.dot(a_ref[...], wa_ref[...], preferred_element_type=F32)
    acc = acc + jnp.dot(s_ref[...], ws_ref[...], preferred_element_type=F32)
    o_ref[...] = x_ref[...] + mod_ref[0, 2:3, :] * acc


def _out_proj(attn_n, ssm_n, w_out, x2d, mod3, seq, tm=512, tn=1024):
    t, d = x2d.shape
    nb = seq // tm
    ka = attn_n.shape[1]
    est = 2 * 2 * tm * ka * 2 + 2 * 2 * ka * tn * 2 + 4 * tm * tn * 4 + 2 * tm * tn * 4
    return pl.pallas_call(
        _out_proj_kernel,
        grid=(t // tm, d // tn),
        in_specs=[
            pl.BlockSpec((tm, ka), lambda i, j: (i, 0)),
            pl.BlockSpec((tm, ka), lambda i, j: (i, 0)),
            pl.BlockSpec((ka, tn), lambda i, j: (0, j)),
            pl.BlockSpec((ka, tn), lambda i, j: (1, j)),
            pl.BlockSpec((tm, tn), lambda i, j: (i, j)),
            pl.BlockSpec((1, N_MOD, tn), lambda i, j: (i // nb, 0, j)),
        ],
        out_specs=pl.BlockSpec((tm, tn), lambda i, j: (i, j)),
        out_shape=jax.ShapeDtypeStruct((t, d), F32),
        compiler_params=pltpu.CompilerParams(
            dimension_semantics=("arbitrary", "arbitrary"),
            vmem_limit_bytes=_vmem_limit(est)),
        name="out_proj",
    )(attn_n, ssm_n, w_out, w_out, x2d, mod3)


N_WBUF = 2


def _ffn_kernel(x_ref, mod_ref, g2_ref, w1_hbm, w2_hbm, gf_ref, o_ref,
                h_scr, gain_scr, rs_scr, w1_buf, w2_buf, sem, *, tf):
    n_blocks = w1_hbm.shape[1] // tf

    def w_copies(k, slot):
        cols = pl.ds(pl.multiple_of(k * tf, tf), tf)
        return (pltpu.make_async_copy(w1_hbm.at[:, cols], w1_buf.at[slot], sem.at[0, slot]),
                pltpu.make_async_copy(w2_hbm.at[cols, :], w2_buf.at[slot], sem.at[1, slot]))

    def start(k, slot):
        for cp in w_copies(k, slot):
            cp.start()

    def wait(k, slot):
        for cp in w_copies(k, slot):
            cp.wait()

    start(0, 0)
    gain_scr[...] = g2_ref[...] * (1.0 + mod_ref[0, 4:5, :])
    _modulated_norm(x_ref, gain_scr, mod_ref.at[0, 3:4, :], h_scr)
    o_ref[...] = jnp.zeros_like(o_ref)

    def block(k, slot):
        @pl.when(k + 1 < n_blocks)
        def _():
            start(k + 1, (slot + 1) % N_WBUF)

        wait(k, slot)
        a = jnp.dot(h_scr[...], w1_buf[slot], preferred_element_type=F32)
        a = jnp.square(jnp.maximum(a, 0.0)).astype(BF16)
        for n0 in range(0, D_MODEL, tf):
            o_ref[:, n0:n0 + tf] += jnp.dot(a, w2_buf[slot, :, n0:n0 + tf], preferred_element_type=F32)

    def trip(kk, carry):
        for slot in range(N_WBUF):
            block(kk * N_WBUF + slot, slot)
        return carry

    lax.fori_loop(0, n_blocks // N_WBUF, trip, 0)

    gate = mod_ref.at[0, 5:6, :]

    def scale_rows(r, carry):
        rows = pl.ds(pl.multiple_of(r * V7X_SUBLANES, V7X_SUBLANES), V7X_SUBLANES)
        x2 = x_ref[rows, :] + gate[...] * o_ref[rows, :]
        rs_scr[rows, :] = jnp.broadcast_to(_rms_scale(x2, D_MODEL), (V7X_SUBLANES, V7X_LANES))
        return carry

    lax.fori_loop(0, o_ref.shape[0] // V7X_SUBLANES, scale_rows, 0, unroll=NORM_UNROLL)

    def rescale_rows(r, carry):
        rows = pl.ds(pl.multiple_of(r * V7X_SUBLANES, V7X_SUBLANES), V7X_SUBLANES)
        x2 = x_ref[rows, :] + gate[...] * o_ref[rows, :]
        rs = jnp.concatenate([rs_scr[rows, :]] * (D_MODEL // V7X_LANES), axis=1)
        o_ref[rows, :] = x2 * rs * gf_ref[...]
        return carry

    lax.fori_loop(0, o_ref.shape[0] // V7X_SUBLANES, rescale_rows, 0, unroll=NORM_UNROLL)


def _ffn(x1, mod3, g2, w1, w2, gf, seq, tm=512, tf=512):
    t, d = x1.shape
    nb = seq // tm
    f = w1.shape[1]
    assert (f // tf) % N_WBUF == 0
    est = (2 * tm * d * 4 + 2 * tm * d * 4 + tm * d * 2 + N_WBUF * 2 * d * tf * 2
           + tm * V7X_LANES * 4 + 4 * tm * tf * 4)
    return pl.pallas_call(
        functools.partial(_ffn_kernel, tf=tf),
        grid=(t // tm,),
        in_specs=[
            pl.BlockSpec((tm, d), lambda i: (i, 0)),
            pl.BlockSpec((1, N_MOD, d), lambda i: (i // nb, 0, 0)),
            pl.BlockSpec((1, d), lambda i: (0, 0)),
            pl.BlockSpec(memory_space=pl.ANY),
            pl.BlockSpec(memory_space=pl.ANY),
            pl.BlockSpec((1, d), lambda i: (0, 0)),
        ],
        out_specs=pl.BlockSpec((tm, d), lambda i: (i, 0)),
        out_shape=jax.ShapeDtypeStruct((t, d), F32),
        scratch_shapes=[
            pltpu.VMEM((tm, d), BF16),
            pltpu.VMEM((1, d), F32),
            pltpu.VMEM((tm, V7X_LANES), F32),
            pltpu.VMEM((N_WBUF, d, tf), BF16),
            pltpu.VMEM((N_WBUF, tf, d), BF16),
            pltpu.SemaphoreType.DMA((2, N_WBUF)),
        ],
        compiler_params=pltpu.CompilerParams(
            dimension_semantics=("arbitrary",),
            vmem_limit_bytes=_vmem_limit(est)),
        name="ffn",
    )(x1, mod3, g2, w1, w2, gf)


def _rope_tables(seq):
    half = HEAD_DIM // 2
    inv_freq = ROPE_THETA ** (-jnp.arange(half, dtype=F32) / half)
    ang = jnp.arange(seq, dtype=F32)[:, None] * inv_freq[None, :]
    cos = jnp.cos(ang)
    sin = jnp.sin(ang)
    reps = V7X_LANES // HEAD_DIM
    cos_t = jnp.tile(jnp.concatenate([cos, cos], axis=1), (1, reps))
    sin_t = jnp.tile(jnp.concatenate([-sin, sin], axis=1), (1, reps))
    return cos_t, sin_t


def kernel(x, c, w_ada, b_ada, norm1_g, w_in, sinks, ssm_lam_re, ssm_lam_im, ssm_log_step, ssm_b_re, ssm_b_im, ssm_c_re, ssm_c_im, ssm_d, w_glu, b_glu, attn_out_g, ssm_out_g, w_out, norm2_g, w_ff1, w_ff2, final_g):
    bsz, seq, d = x.shape
    t = bsz * seq
    x2d = x.reshape(t, d)

    c_pad = jnp.pad(c, ((0, V7X_SUBLANES - bsz), (0, 0)))
    mod = _adaln(c_pad, w_ada[0], b_ada[0].reshape(1, -1))
    mod3 = mod[:bsz].reshape(bsz, N_MOD, d)

    cos_t, sin_t = _rope_tables(seq)
    q, kvd, u_c = _in_proj(x2d, mod3, norm1_g[0].reshape(1, d), w_in[0].astype(BF16), cos_t, sin_t, seq)

    attn_n = _attention(q, kvd, sinks[0], attn_out_g[0].reshape(1, -1), bsz, seq)

    rev, bpow, cpow, mu = _ssm_operators(ssm_lam_re[0], ssm_lam_im[0], ssm_log_step[0], ssm_b_re[0],
                                         ssm_b_im[0], ssm_c_re[0], ssm_c_im[0], ssm_d[0])
    y = _ssm(u_c, rev, bpow, cpow, mu, bsz, seq)
    ssm_n = _glu(y, w_glu[0].astype(BF16), b_glu[0].reshape(1, -1), ssm_out_g[0].reshape(1, -1))

    x1 = _out_proj(attn_n, ssm_n, w_out[0].astype(BF16), x2d, mod3, seq)
    out = _ffn(x1, mod3, norm2_g[0].reshape(1, d), w_ff1[0].astype(BF16), w_ff2[0].astype(BF16),
               final_g.reshape(1, d), seq)
    return out.reshape(bsz, seq, d)
```

```python
import functools
import math

import jax
import jax.numpy as jnp
from jax import lax
from jax.experimental import pallas as pl
from jax.experimental.pallas import tpu as pltpu

D_MODEL = 4096
D_ATTN = 2048
D_SSM = 2048
HEAD_DIM = 64
N_Q_HEADS = 32
N_KV_HEADS = 4
Q_PER_KV = 8
D_KV = 256
WINDOW = 128
ROPE_THETA = 10000.0
SSM_GROUP = 16
N_SSM_GROUPS = 128
STATE = 64
D_FF = 4 * D_MODEL
N_MOD = 6
EPS = 1e-6

V7X_LANES = 128
V7X_SUBLANES = 8
V7X_VMEM_BYTES = 64 * 1024 * 1024

CHUNK = 16
N_SEG = 4
SSM_BATCH_SPLIT = 2
SLAB_GROUPS = V7X_LANES // SSM_GROUP
N_SLABS = N_SSM_GROUPS // SLAB_GROUPS
SLAB_COLS = CHUNK * V7X_LANES
SLAB_STATE = SLAB_GROUPS * STATE
TILE = 256
STEPS_PER_TILE = TILE // V7X_LANES
SCAN_UNROLL = 8

N_WBUF = 2

BF16 = jnp.bfloat16
F32 = jnp.float32


def _vmem_limit(nbytes):
    return int(min(nbytes + 8 * 1024 * 1024, V7X_VMEM_BYTES - 4 * 1024 * 1024))


def _rms_scale(xf, width):
    return lax.rsqrt(jnp.sum(xf * xf, axis=-1, keepdims=True) * (1.0 / width) + EPS)


NORM_ROWS = 16
NORM_UNROLL = 4


def _modulated_norm(x_ref, gain_ref, shift_ref, out_ref):
    width = x_ref.shape[1]

    def body(r, carry):
        rows = pl.ds(pl.multiple_of(r * NORM_ROWS, NORM_ROWS), NORM_ROWS)
        xf = x_ref[rows, :]
        out_ref[rows, :] = (xf * _rms_scale(xf, width) * gain_ref[...] + shift_ref[...]).astype(out_ref.dtype)
        return carry

    lax.fori_loop(0, x_ref.shape[0] // NORM_ROWS, body, 0, unroll=NORM_UNROLL)


def _adaln_kernel(c_ref, w_ref, b_ref, o_ref):
    c = c_ref[...]
    ca = (c * jax.nn.sigmoid(c)).astype(BF16)
    acc = jnp.dot(ca, w_ref[...].astype(BF16), preferred_element_type=F32)
    o_ref[...] = acc + b_ref[...]


def _adaln(c_pad, w_ada, b_ada, tn=1024):
    m, d = c_pad.shape
    n = w_ada.shape[1]
    return pl.pallas_call(
        _adaln_kernel,
        grid=(n // tn,),
        in_specs=[
            pl.BlockSpec((m, d), lambda j: (0, 0)),
            pl.BlockSpec((d, tn), lambda j: (0, j)),
            pl.BlockSpec((1, tn), lambda j: (0, j)),
        ],
        out_specs=pl.BlockSpec((m, tn), lambda j: (0, j)),
        out_shape=jax.ShapeDtypeStruct((m, n), F32),
        compiler_params=pltpu.CompilerParams(
            dimension_semantics=("arbitrary",),
            vmem_limit_bytes=_vmem_limit(2 * d * tn * 4 + d * tn * 2)),
        name="adaln",
    )(c_pad, w_ada, b_ada)


def _rope(acc, cos, sin_signed):
    width = acc.shape[1]
    lane = lax.broadcasted_iota(jnp.int32, acc.shape, 1)
    first_half = (lane % HEAD_DIM) < (HEAD_DIM // 2)
    partner = jnp.where(first_half,
                        pltpu.roll(acc, width - HEAD_DIM // 2, 1),
                        pltpu.roll(acc, HEAD_DIM // 2, 1))
    reps = width // cos.shape[1]
    cos_w = jnp.concatenate([cos] * reps, axis=1)
    sin_w = jnp.concatenate([sin_signed] * reps, axis=1)
    return acc * cos_w + partner * sin_w


def _in_proj_kernel(x_ref, mod_ref, g_ref, w_hbm, cos_ref, sin_ref,
                    q_ref, kv_ref, u_ref, h_scr, u_scr, gain_scr, w_buf, sem, *, tn):
    n_q = D_ATTN // tn
    n_blocks = w_hbm.shape[1] // tn
    slabs_per_blk = tn // V7X_LANES
    chunk_rows = u_ref.shape[1]

    def w_copy(j, slot):
        return pltpu.make_async_copy(w_hbm.at[:, j * tn:(j + 1) * tn], w_buf.at[slot], sem.at[slot])

    def finish(j, acc):
        if j < n_q:
            q = _rope(acc, cos_ref[...], sin_ref[...]) * (HEAD_DIM ** -0.5)
            q_ref[:, j * tn:(j + 1) * tn] = q.astype(BF16)
        elif j == n_q:
            k = _rope(acc[:, :D_KV], cos_ref[...], sin_ref[...])
            kv = jnp.concatenate([k, acc[:, D_KV:]], axis=1)
            lane = lax.broadcasted_iota(jnp.int32, (kv.shape[0], V7X_LANES), 1)
            left = lane < HEAD_DIM
            pieces = []
            for c0 in range(0, 2 * D_KV, V7X_LANES):
                a = kv[:, c0:c0 + V7X_LANES]
                s = pltpu.roll(a, HEAD_DIM, 1)
                pieces.append(jnp.where(left, a, s))
                pieces.append(jnp.where(left, s, a))
            kv_ref[...] = jnp.concatenate(pieces, axis=1).astype(BF16)
        else:
            slab0 = (j - n_q - 1) * slabs_per_blk
            for s in range(slabs_per_blk):
                u_scr[s] = acc[:, s * V7X_LANES:(s + 1) * V7X_LANES]
            for s in range(slabs_per_blk):
                for step in range(CHUNK):
                    rows = u_scr[s, pl.ds(step, chunk_rows, stride=CHUNK), :]
                    u_ref[slab0 + s, :, step * V7X_LANES:(step + 1) * V7X_LANES] = rows.astype(BF16)

    w_copy(0, 0).start()
    gain_scr[...] = g_ref[...] * (1.0 + mod_ref[0, 1:2, :])
    _modulated_norm(x_ref, gain_scr, mod_ref.at[0, 0:1, :], h_scr)

    prev = None
    for j in range(n_blocks):
        slot = j % N_WBUF
        if j + 1 < n_blocks:
            w_copy(j + 1, (j + 1) % N_WBUF).start()
        w_copy(j, slot).wait()
        acc = jnp.dot(h_scr[...], w_buf[slot], preferred_element_type=F32)
        if prev is not None:
            finish(j - 1, prev)
        prev = acc
    finish(n_blocks - 1, prev)


def _in_proj(x2d, mod3, g1, w_in, cos_t, sin_t, seq, tm=512, tn=512):
    t, d = x2d.shape
    assert 2 * D_KV == tn and w_in.shape[1] == D_ATTN + tn + D_SSM
    slabs_per_blk = tn // V7X_LANES
    nb = seq // tm
    est = (2 * tm * d * 4 + tm * d * 2 + N_WBUF * d * tn * 2 + 2 * tm * D_ATTN * 2
           + 2 * tm * 4 * D_KV * 2 + 2 * tm * D_SSM * 2 + 4 * tm * V7X_LANES * 4
           + slabs_per_blk * tm * V7X_LANES * 4 + 6 * tm * tn * 4)
    return pl.pallas_call(
        functools.partial(_in_proj_kernel, tn=tn),
        grid=(t // tm,),
        in_specs=[
            pl.BlockSpec((tm, d), lambda i: (i, 0)),
            pl.BlockSpec((1, N_MOD, d), lambda i: (i // nb, 0, 0)),
            pl.BlockSpec((1, d), lambda i: (0, 0)),
            pl.BlockSpec(memory_space=pl.ANY),
            pl.BlockSpec((tm, V7X_LANES), lambda i: (i % nb, 0)),
            pl.BlockSpec((tm, V7X_LANES), lambda i: (i % nb, 0)),
        ],
        out_specs=[
            pl.BlockSpec((tm, D_ATTN), lambda i: (i, 0)),
            pl.BlockSpec((tm, 4 * D_KV), lambda i: (i, 0)),
            pl.BlockSpec((N_SLABS, tm // CHUNK, SLAB_COLS), lambda i: (0, i, 0)),
        ],
        out_shape=[
            jax.ShapeDtypeStruct((t, D_ATTN), BF16),
            jax.ShapeDtypeStruct((t, 4 * D_KV), BF16),
            jax.ShapeDtypeStruct((N_SLABS, t // CHUNK, SLAB_COLS), BF16),
        ],
        scratch_shapes=[
            pltpu.VMEM((tm, d), BF16),
            pltpu.VMEM((slabs_per_blk, tm, V7X_LANES), F32),
            pltpu.VMEM((1, d), F32),
            pltpu.VMEM((N_WBUF, d, tn), BF16),
            pltpu.SemaphoreType.DMA((N_WBUF,)),
        ],
        compiler_params=pltpu.CompilerParams(
            dimension_semantics=("arbitrary",),
            vmem_limit_bytes=_vmem_limit(est)),
        name="in_proj",
    )(x2d, mod3, g1, w_in, cos_t, sin_t)


def _attn_kernel(sink_ref, q_ref, kvc_ref, kvp_ref, g_ref, o_ref, o_scr, cap_scr):
    n = pl.program_id(1)
    blk = WINDOW
    pair_w = 2 * HEAD_DIM
    n_keys = 2 * blk

    n_pairs = Q_PER_KV // 2
    rows = n_pairs * blk

    @pl.when((pl.program_id(0) == 0) & (n == 0))
    def _():
        qi = lax.broadcasted_iota(jnp.int32, (blk, 2 * n_keys), 0)
        key = lax.broadcasted_iota(jnp.int32, (blk, 2 * n_keys), 1) % n_keys
        rel = qi + blk - key
        band = (rel >= 0) & (rel < WINDOW)
        cap_scr[0] = jnp.where(band & (key >= blk), jnp.inf, F32(-1e30))
        cap_scr[1] = jnp.where(band, jnp.inf, F32(-1e30))

    cap = cap_scr[jnp.minimum(n, 1)]
    pair_of_row = lax.broadcasted_iota(jnp.int32, (rows, 1), 0) // blk

    lane = lax.broadcasted_iota(jnp.int32, (n_keys, pair_w), 1)
    left = lane < HEAD_DIM
    left_o = lax.broadcasted_iota(jnp.int32, (rows, pair_w), 1) < HEAD_DIM
    zero = jnp.zeros((n_keys, pair_w), BF16)

    for h in range(N_KV_HEADS):
        kcol = h * pair_w
        vcol = N_KV_HEADS * pair_w + h * pair_w
        kd = jnp.concatenate([kvp_ref[:, kcol:kcol + pair_w], kvc_ref[:, kcol:kcol + pair_w]], axis=0)
        vd = jnp.concatenate([kvp_ref[:, vcol:vcol + pair_w], kvc_ref[:, vcol:vcol + pair_w]], axis=0)
        k_bd = jnp.concatenate([jnp.where(left, kd, zero), jnp.where(left, zero, kd)], axis=0)
        v_bd = jnp.concatenate([jnp.where(left, vd, zero), jnp.where(left, zero, vd)], axis=0)
        head0 = h * Q_PER_KV
        q4 = jnp.concatenate([q_ref[:, (head0 + 2 * p) * HEAD_DIM:(head0 + 2 * p + 2) * HEAD_DIM]
                              for p in range(n_pairs)], axis=0)
        sink_a = jnp.zeros((rows, 1), F32)
        sink_b = jnp.zeros((rows, 1), F32)
        for p in range(n_pairs):
            sink_a = jnp.where(pair_of_row == p, sink_ref[head0 + 2 * p], sink_a)
            sink_b = jnp.where(pair_of_row == p, sink_ref[head0 + 2 * p + 1], sink_b)
        s = lax.dot_general(q4, k_bd, (((1,), (1,)), ((), ())), preferred_element_type=F32)
        s = jnp.minimum(s.reshape(n_pairs, blk, 2 * n_keys), cap[None]).reshape(rows, 2 * n_keys)
        m_a = jnp.maximum(jnp.max(s[:, :n_keys], axis=-1, keepdims=True), sink_a)
        m_b = jnp.maximum(jnp.max(s[:, n_keys:], axis=-1, keepdims=True), sink_b)
        p_a = jnp.exp(s[:, :n_keys] - m_a)
        p_b = jnp.exp(s[:, n_keys:] - m_b)
        l_a = jnp.sum(p_a, axis=-1, keepdims=True) + jnp.exp(sink_a - m_a)
        l_b = jnp.sum(p_b, axis=-1, keepdims=True) + jnp.exp(sink_b - m_b)
        pp = jnp.concatenate([p_a, p_b], axis=1).astype(BF16)
        o4 = jnp.dot(pp, v_bd, preferred_element_type=F32) * jnp.where(left_o, 1.0 / l_a, 1.0 / l_b)
        for p in range(n_pairs):
            qcol = (head0 + 2 * p) * HEAD_DIM
            o_scr[:, qcol:qcol + pair_w] = o4[p * blk:(p + 1) * blk, :]

    o = o_scr[...]
    o_ref[...] = (o * _rms_scale(o, D_ATTN) * g_ref[...]).astype(BF16)


def _attention(q, kvd, sinks, g_attn, batch, seq):
    t = q.shape[0]
    nb = seq // WINDOW
    kvw = kvd.shape[1]
    return pl.pallas_call(
        _attn_kernel,
        grid=(batch, nb),
        in_specs=[
            pl.BlockSpec(memory_space=pltpu.SMEM),
            pl.BlockSpec((WINDOW, D_ATTN), lambda b, n: (b * nb + n, 0)),
            pl.BlockSpec((WINDOW, kvw), lambda b, n: (b * nb + n, 0)),
            pl.BlockSpec((WINDOW, kvw), lambda b, n: (b * nb + jnp.maximum(n - 1, 0), 0)),
            pl.BlockSpec((1, D_ATTN), lambda b, n: (0, 0)),
        ],
        out_specs=pl.BlockSpec((WINDOW, D_ATTN), lambda b, n: (b * nb + n, 0)),
        out_shape=jax.ShapeDtypeStruct((t, D_ATTN), BF16),
        scratch_shapes=[pltpu.VMEM((WINDOW, D_ATTN), F32), pltpu.VMEM((2, WINDOW, 4 * WINDOW), F32)],
        compiler_params=pltpu.CompilerParams(
            dimension_semantics=("arbitrary", "arbitrary"),
            vmem_limit_bytes=_vmem_limit(8 * 1024 * 1024)),
        name="attention",
    )(sinks, q, kvd, kvd, g_attn)


def _cmul(a_re, a_im, b_re, b_im):
    return a_re * b_re - a_im * b_im, a_re * b_im + a_im * b_re


def _ssm_kernel(u_ref, rev_ref, bpow_ref, cpow_ref, mu_ref, o_ref, z_scr, sp_scr, *, seg_chunks):
    n_k = SLAB_STATE // V7X_LANES
    rows = u_ref.shape[1]
    seqs = rows // seg_chunks
    shape = (seqs, V7X_LANES)

    pitch = z_scr.shape[1] // seqs

    for nb in range(2 * SLAB_STATE // TILE):
        zz = jnp.dot(u_ref[0], bpow_ref[0, :, nb * TILE:(nb + 1) * TILE], preferred_element_type=F32)
        for q in range(seqs):
            src = slice(q * seg_chunks, (q + 1) * seg_chunks)
            dst = slice(q * pitch, q * pitch + seg_chunks)
            z_scr[2 * nb, dst, :] = zz[src, :V7X_LANES]
            z_scr[2 * nb + 1, dst, :] = zz[src, V7X_LANES:]

    mu_re = [jnp.broadcast_to(mu_ref[0, 0:1, k * V7X_LANES:(k + 1) * V7X_LANES], shape) for k in range(n_k)]
    mu_im = [jnp.broadcast_to(mu_ref[0, 1:2, k * V7X_LANES:(k + 1) * V7X_LANES], shape) for k in range(n_k)]

    def rows_at(c):
        return pl.ds(c, seqs, stride=pitch)

    def scan_step(c, carry):
        new = []
        for k in range(n_k):
            s_re, s_im = carry[2 * k], carry[2 * k + 1]
            sp_scr[k, rows_at(c), :] = s_re
            sp_scr[n_k + k, rows_at(c), :] = s_im
            p_re, p_im = _cmul(mu_re[k], mu_im[k], s_re, s_im)
            new += [p_re + z_scr[k, rows_at(c), :], p_im + z_scr[n_k + k, rows_at(c), :]]
        return tuple(new)

    zeros = jnp.zeros(shape, F32)
    final = lax.fori_loop(0, seg_chunks, scan_step, (zeros,) * (2 * n_k), unroll=SCAN_UNROLL)

    first_seg = (lax.broadcasted_iota(jnp.int32, shape, 0) % N_SEG) == 0
    init = []
    for k in range(n_k):
        m_re, m_im = mu_re[k], mu_im[k]
        for _ in range(int(math.log2(seg_chunks))):
            m_re, m_im = _cmul(m_re, m_im, m_re, m_im)
        i_re, i_im = zeros, zeros
        for _ in range(N_SEG - 1):
            t_re, t_im = _cmul(m_re, m_im, i_re, i_im)
            i_re = jnp.where(first_seg, 0.0, pltpu.roll(t_re + final[2 * k], 1, 0))
            i_im = jnp.where(first_seg, 0.0, pltpu.roll(t_im + final[2 * k + 1], 1, 0))
        init += [i_re, i_im]

    def fix_step(c, carry):
        new = []
        for k in range(n_k):
            c_re, c_im = carry[2 * k], carry[2 * k + 1]
            sp_scr[k, rows_at(c), :] += c_re
            sp_scr[n_k + k, rows_at(c), :] += c_im
            new += list(_cmul(mu_re[k], mu_im[k], c_re, c_im))
        return tuple(new)

    lax.fori_loop(0, seg_chunks, fix_step, tuple(init), unroll=SCAN_UNROLL)

    sp = jnp.concatenate(
        [jnp.concatenate([sp_scr[k, q * pitch:q * pitch + seg_chunks, :] for q in range(seqs)], axis=0)
         for k in range(2 * n_k)], axis=1).astype(BF16)
    for jt in range(SLAB_COLS // TILE):
        k_len = (jt + 1) * TILE
        y = jnp.dot(u_ref[0, :, :k_len], rev_ref[0, SLAB_COLS - k_len:, :], preferred_element_type=F32)
        y = y + jnp.dot(sp, cpow_ref[0, :, jt * TILE:(jt + 1) * TILE], preferred_element_type=F32)
        y = 0.5 * y * (1.0 + lax.erf(y * (2.0 ** -0.5)))
        for q in range(STEPS_PER_TILE):
            step = jt * STEPS_PER_TILE + q
            o_ref[pl.ds(step, rows, stride=CHUNK), :] = y[:, q * V7X_LANES:(q + 1) * V7X_LANES]


def _ssm(u_c, rev, bpow, cpow, mu, bsz, seq):
    n_slabs, chunk_rows, _ = u_c.shape
    t = chunk_rows * CHUNK
    rows = chunk_rows // SSM_BATCH_SPLIT
    seg_chunks = seq // (CHUNK * N_SEG)
    assert bsz % SSM_BATCH_SPLIT == 0 and rows // seg_chunks == V7X_SUBLANES
    assert seg_chunks & (seg_chunks - 1) == 0
    assert (seg_chunks // V7X_SUBLANES) % 2 == 0
    scan_rows = V7X_SUBLANES * (seg_chunks + V7X_SUBLANES)
    kern = functools.partial(_ssm_kernel, seg_chunks=seg_chunks)
    est = (2 * rows * SLAB_COLS * 2 + 2 * SLAB_COLS * TILE * 2 + 4 * SLAB_COLS * 2 * SLAB_STATE * 2
           + 2 * rows * CHUNK * V7X_LANES * 4 + 2 * rows * 2 * SLAB_STATE * 4 + 8 * rows * TILE * 4
           + rows * 2 * SLAB_STATE * 2)
    return pl.pallas_call(
        kern,
        grid=(n_slabs, SSM_BATCH_SPLIT),
        in_specs=[
            pl.BlockSpec((1, rows, SLAB_COLS), lambda s, h: (s, h, 0)),
            pl.BlockSpec((1, SLAB_COLS, TILE), lambda s, h: (s, 0, 0)),
            pl.BlockSpec((1, SLAB_COLS, 2 * SLAB_STATE), lambda s, h: (s, 0, 0)),
            pl.BlockSpec((1, 2 * SLAB_STATE, SLAB_COLS), lambda s, h: (s, 0, 0)),
            pl.BlockSpec((1, 2, SLAB_STATE), lambda s, h: (s, 0, 0)),
        ],
        out_specs=pl.BlockSpec((rows * CHUNK, V7X_LANES), lambda s, h: (h, s)),
        out_shape=jax.ShapeDtypeStruct((t, n_slabs * V7X_LANES), F32),
        scratch_shapes=[pltpu.VMEM((2 * SLAB_STATE // V7X_LANES, scan_rows, V7X_LANES), F32),
                        pltpu.VMEM((2 * SLAB_STATE // V7X_LANES, scan_rows, V7X_LANES), F32)],
        compiler_params=pltpu.CompilerParams(
            dimension_semantics=("arbitrary", "arbitrary"),
            vmem_limit_bytes=_vmem_limit(est)),
        name="ssm",
    )(u_c, rev, bpow, cpow, mu)


def _zoh(lam_re, lam_im, log_step):
    step = jnp.exp(log_step)
    mag = jnp.exp(lam_re * step)
    bar_re = mag * jnp.cos(lam_im * step)
    bar_im = mag * jnp.sin(lam_im * step)
    num_re, num_im = bar_re - 1.0, bar_im
    inv_den = 1.0 / (lam_re * lam_re + lam_im * lam_im)
    coef_re = (num_re * lam_re + num_im * lam_im) * inv_den
    coef_im = (num_im * lam_re - num_re * lam_im) * inv_den
    return bar_re, bar_im, coef_re, coef_im


def _ssm_ops_kernel(lam_r_ref, bt_re_ref, bt_im_ref, ct_re_ref, ct_im_ref, d_ref,
                    rev_ref, bpow_ref, cpow_ref, mu_ref, km_scr):
    lanes = V7X_LANES
    bar_re, bar_im, coef_re, coef_im = _zoh(lam_r_ref[0, 0:1, :], lam_r_ref[0, 1:2, :], lam_r_ref[0, 2:3, :])
    bb_re, bb_im = _cmul(bt_re_ref[0], bt_im_ref[0], coef_re, coef_im)
    p_re, p_im = jnp.ones_like(bar_re), jnp.zeros_like(bar_re)
    for jp in range(CHUNK - 1, -1, -1):
        r0 = jp * lanes
        blk_re, blk_im = _cmul(bb_re, bb_im, p_re, p_im)
        bpow_ref[0, r0:r0 + lanes, :SLAB_STATE] = blk_re.astype(BF16)
        bpow_ref[0, r0:r0 + lanes, SLAB_STATE:] = blk_im.astype(BF16)
        p_re, p_im = _cmul(p_re, p_im, bar_re, bar_im)
    mu_ref[0, 0:1, :] = p_re
    mu_ref[0, 1:2, :] = p_im

    cbar_re = jnp.broadcast_to(bar_re, (lanes, SLAB_STATE)).T
    cbar_im = jnp.broadcast_to(bar_im, (lanes, SLAB_STATE)).T
    w_re, w_im = ct_re_ref[0], ct_im_ref[0]
    row_i = lax.broadcasted_iota(jnp.int32, (lanes, lanes), 0)
    col_i = lax.broadcasted_iota(jnp.int32, (lanes, lanes), 1)
    skip = jnp.where(row_i == col_i, d_ref[0], 0.0)
    bb_cat = jnp.concatenate([bb_re, -bb_im], axis=1).astype(BF16)
    for m in range(CHUNK):
        w_cat = jnp.concatenate([w_re, w_im], axis=0).astype(BF16)
        km = jnp.dot(bb_cat, w_cat, preferred_element_type=F32)
        km_scr[m] = km + skip if m == 0 else km
        w_re, w_im = _cmul(w_re, w_im, cbar_re, cbar_im)
        cpow_ref[0, :SLAB_STATE, m * lanes:(m + 1) * lanes] = w_re.astype(BF16)
        cpow_ref[0, SLAB_STATE:, m * lanes:(m + 1) * lanes] = (-w_im).astype(BF16)

    assert STEPS_PER_TILE == 2
    n_tiles = SLAB_COLS // TILE
    for i in range(n_tiles):
        d = n_tiles - 1 - i
        r0 = i * TILE
        below = km_scr[2 * d - 1] if d > 0 else jnp.zeros((lanes, lanes), F32)
        rev_ref[0, r0:r0 + lanes, :lanes] = km_scr[2 * d].astype(BF16)
        rev_ref[0, r0:r0 + lanes, lanes:] = km_scr[2 * d + 1].astype(BF16)
        rev_ref[0, r0 + lanes:r0 + TILE, :lanes] = below.astype(BF16)
        rev_ref[0, r0 + lanes:r0 + TILE, lanes:] = km_scr[2 * d].astype(BF16)


def _ssm_operators(lam_re, lam_im, log_step, b_re, b_im, c_re, c_im, d_skip):
    s, sg = N_SLABS, SLAB_GROUPS
    eye_g = jnp.eye(sg, dtype=F32)

    def per_state(a):
        return a.reshape(s, SLAB_STATE)

    step_gp = jnp.broadcast_to(log_step[:, None], lam_re.shape)
    lam_rows = jnp.stack([per_state(lam_re), per_state(lam_im), per_state(step_gp)], axis=1)

    def block_diag_bt(b):
        bt = b.reshape(s, sg, STATE, SSM_GROUP).transpose(0, 1, 3, 2)
        return (bt[:, :, :, None, :] * eye_g[None, :, None, :, None]).reshape(s, V7X_LANES, SLAB_STATE)

    def block_diag_ct(c):
        ct = c.reshape(s, sg, SSM_GROUP, STATE).transpose(0, 1, 3, 2)
        return (ct[:, :, :, None, :] * eye_g[None, :, None, :, None]).reshape(s, SLAB_STATE, V7X_LANES)

    d_rows = d_skip.reshape(s, 1, V7X_LANES)
    operands = (lam_rows, block_diag_bt(b_re), block_diag_bt(b_im),
                block_diag_ct(c_re), block_diag_ct(c_im), d_rows)
    return pl.pallas_call(
        _ssm_ops_kernel,
        grid=(s,),
        in_specs=[
            pl.BlockSpec((1, 3, SLAB_STATE), lambda i: (i, 0, 0)),
            pl.BlockSpec((1, V7X_LANES, SLAB_STATE), lambda i: (i, 0, 0)),
            pl.BlockSpec((1, V7X_LANES, SLAB_STATE), lambda i: (i, 0, 0)),
            pl.BlockSpec((1, SLAB_STATE, V7X_LANES), lambda i: (i, 0, 0)),
            pl.BlockSpec((1, SLAB_STATE, V7X_LANES), lambda i: (i, 0, 0)),
            pl.BlockSpec((1, 1, V7X_LANES), lambda i: (i, 0, 0)),
        ],
        out_specs=[
            pl.BlockSpec((1, SLAB_COLS, TILE), lambda i: (i, 0, 0)),
            pl.BlockSpec((1, SLAB_COLS, 2 * SLAB_STATE), lambda i: (i, 0, 0)),
            pl.BlockSpec((1, 2 * SLAB_STATE, SLAB_COLS), lambda i: (i, 0, 0)),
            pl.BlockSpec((1, 2, SLAB_STATE), lambda i: (i, 0, 0)),
        ],
        out_shape=[
            jax.ShapeDtypeStruct((s, SLAB_COLS, TILE), BF16),
            jax.ShapeDtypeStruct((s, SLAB_COLS, 2 * SLAB_STATE), BF16),
            jax.ShapeDtypeStruct((s, 2 * SLAB_STATE, SLAB_COLS), BF16),
            jax.ShapeDtypeStruct((s, 2, SLAB_STATE), F32),
        ],
        scratch_shapes=[pltpu.VMEM((CHUNK, V7X_LANES, V7X_LANES), F32)],
        compiler_params=pltpu.CompilerParams(
            dimension_semantics=("arbitrary",),
            vmem_limit_bytes=_vmem_limit(2 * (SLAB_COLS * TILE + 4 * SLAB_COLS * SLAB_STATE) * 2
                                         + 24 * SLAB_STATE * V7X_LANES * 4)),
        name="ssm_ops",
    )(*operands)


def _glu_kernel(y_ref, w_ref, b_ref, g_ref, o_ref):
    y = y_ref[...]
    z = jnp.dot(y.astype(BF16), w_ref[...], preferred_element_type=F32) + b_ref[...]
    out = y * jax.nn.sigmoid(z)
    o_ref[...] = (out * _rms_scale(out, D_SSM) * g_ref[...]).astype(BF16)


def _glu(y, w_glu, b_glu, g_ssm, tm=512):
    t, d = y.shape
    est = 2 * tm * d * 4 + 2 * tm * d * 2 + 2 * d * d * 2 + 4 * tm * d * 4
    return pl.pallas_call(
        _glu_kernel,
        grid=(t // tm,),
        in_specs=[
            pl.BlockSpec((tm, d), lambda i: (i, 0)),
            pl.BlockSpec((d, d), lambda i: (0, 0)),
            pl.BlockSpec((1, d), lambda i: (0, 0)),
            pl.BlockSpec((1, d), lambda i: (0, 0)),
        ],
        out_specs=pl.BlockSpec((tm, d), lambda i: (i, 0)),
        out_shape=jax.ShapeDtypeStruct((t, d), BF16),
        compiler_params=pltpu.CompilerParams(
            dimension_semantics=("arbitrary",),
            vmem_limit_bytes=_vmem_limit(est)),
        name="glu",
    )(y, w_glu, b_glu, g_ssm)


def _out_proj_kernel(a_ref, s_ref, wa_ref, ws_ref, x_ref, mod_ref, o_ref):
    acc = jnp.dot(a_ref[...], wa_ref[...], preferred_element_type=F32)
    acc = acc + jnp.dot(s_ref[...], ws_ref[...], preferred_element_type=F32)
    o_ref[...] = x_ref[...] + mod_ref[0, 2:3, :] * acc


def _out_proj(attn_n, ssm_n, w_out, x2d, mod3, seq, tm=512, tn=1024):
    t, d = x2d.shape
    nb = seq // tm
    ka = attn_n.shape[1]
    est = 2 * 2 * tm * ka * 2 + 2 * 2 * ka * tn * 2 + 4 * tm * tn * 4 + 2 * tm * tn * 4
    return pl.pallas_call(
        _out_proj_kernel,
        grid=(t // tm, d // tn),
        in_specs=[
            pl.BlockSpec((tm, ka), lambda i, j: (i, 0)),
            pl.BlockSpec((tm, ka), lambda i, j: (i, 0)),
            pl.BlockSpec((ka, tn), lambda i, j: (0, j)),
            pl.BlockSpec((ka, tn), lambda i, j: (1, j)),
            pl.BlockSpec((tm, tn), lambda i, j: (i, j)),
            pl.BlockSpec((1, N_MOD, tn), lambda i, j: (i // nb, 0, j)),
        ],
        out_specs=pl.BlockSpec((tm, tn), lambda i, j: (i, j)),
        out_shape=jax.ShapeDtypeStruct((t, d), F32),
        compiler_params=pltpu.CompilerParams(
            dimension_semantics=("arbitrary", "arbitrary"),
            vmem_limit_bytes=_vmem_limit(est)),
        name="out_proj",
    )(attn_n, ssm_n, w_out, w_out, x2d, mod3)


def _ffn_kernel(x_ref, mod_ref, g2_ref, w1_hbm, w2_hbm, gf_ref, o_ref,
                h_scr, gain_scr, rs_scr, w1_buf, w2_buf, sem, *, tf):
    n_blocks = w1_hbm.shape[1] // tf

    def w_copies(k, slot):
        cols = pl.ds(pl.multiple_of(k * tf, tf), tf)
        return (pltpu.make_async_copy(w1_hbm.at[:, cols], w1_buf.at[slot], sem.at[0, slot]),
                pltpu.make_async_copy(w2_hbm.at[cols, :], w2_buf.at[slot], sem.at[1, slot]))

    def start(k, slot):
        for cp in w_copies(k, slot):
            cp.start()

    def wait(k, slot):
        for cp in w_copies(k, slot):
            cp.wait()

    start(0, 0)
    gain_scr[...] = g2_ref[...] * (1.0 + mod_ref[0, 4:5, :])
    _modulated_norm(x_ref, gain_scr, mod_ref.at[0, 3:4, :], h_scr)
    o_ref[...] = jnp.zeros_like(o_ref)

    def block(k, slot):
        @pl.when(k + 1 < n_blocks)
        def _():
            start(k + 1, (slot + 1) % N_WBUF)

        wait(k, slot)
        a = jnp.dot(h_scr[...], w1_buf[slot], preferred_element_type=F32)
        a = jnp.square(jnp.maximum(a, 0.0)).astype(BF16)
        for n0 in range(0, D_MODEL, tf):
            o_ref[:, n0:n0 + tf] += jnp.dot(a, w2_buf[slot, :, n0:n0 + tf], preferred_element_type=F32)

    def trip(kk, carry):
        for slot in range(N_WBUF):
            block(kk * N_WBUF + slot, slot)
        return carry

    lax.fori_loop(0, n_blocks // N_WBUF, trip, 0)

    gate = mod_ref.at[0, 5:6, :]

    def scale_rows(r, carry):
        rows = pl.ds(pl.multiple_of(r * V7X_SUBLANES, V7X_SUBLANES), V7X_SUBLANES)
        x2 = x_ref[rows, :] + gate[...] * o_ref[rows, :]
        rs_scr[rows, :] = jnp.broadcast_to(_rms_scale(x2, D_MODEL), (V7X_SUBLANES, V7X_LANES))
        return carry

    lax.fori_loop(0, o_ref.shape[0] // V7X_SUBLANES, scale_rows, 0, unroll=NORM_UNROLL)

    def rescale_rows(r, carry):
        rows = pl.ds(pl.multiple_of(r * V7X_SUBLANES, V7X_SUBLANES), V7X_SUBLANES)
        x2 = x_ref[rows, :] + gate[...] * o_ref[rows, :]
        rs = jnp.concatenate([rs_scr[rows, :]] * (D_MODEL // V7X_LANES), axis=1)
        o_ref[rows, :] = x2 * rs * gf_ref[...]
        return carry

    lax.fori_loop(0, o_ref.shape[0] // V7X_SUBLANES, rescale_rows, 0, unroll=NORM_UNROLL)


def _ffn(x1, mod3, g2, w1, w2, gf, seq, tm=512, tf=512):
    t, d = x1.shape
    nb = seq // tm
    f = w1.shape[1]
    assert (f // tf) % N_WBUF == 0
    est = (2 * tm * d * 4 + 2 * tm * d * 4 + tm * d * 2 + N_WBUF * 2 * d * tf * 2
           + tm * V7X_LANES * 4 + 4 * tm * tf * 4)
    return pl.pallas_call(
        functools.partial(_ffn_kernel, tf=tf),
        grid=(t // tm,),
        in_specs=[
            pl.BlockSpec((tm, d), lambda i: (i, 0)),
            pl.BlockSpec((1, N_MOD, d), lambda i: (i // nb, 0, 0)),
            pl.BlockSpec((1, d), lambda i: (0, 0)),
            pl.BlockSpec(memory_space=pl.ANY),
            pl.BlockSpec(memory_space=pl.ANY),
            pl.BlockSpec((1, d), lambda i: (0, 0)),
        ],
        out_specs=pl.BlockSpec((tm, d), lambda i: (i, 0)),
        out_shape=jax.ShapeDtypeStruct((t, d), F32),
        scratch_shapes=[
            pltpu.VMEM((tm, d), BF16),
            pltpu.VMEM((1, d), F32),
            pltpu.VMEM((tm, V7X_LANES), F32),
            pltpu.VMEM((N_WBUF, d, tf), BF16),
            pltpu.VMEM((N_WBUF, tf, d), BF16),
            pltpu.SemaphoreType.DMA((2, N_WBUF)),
        ],
        compiler_params=pltpu.CompilerParams(
            dimension_semantics=("arbitrary",),
            vmem_limit_bytes=_vmem_limit(est)),
        name="ffn",
    )(x1, mod3, g2, w1, w2, gf)


def _rope_tables(seq):
    half = HEAD_DIM // 2
    inv_freq = ROPE_THETA ** (-jnp.arange(half, dtype=F32) / half)
    ang = jnp.arange(seq, dtype=F32)[:, None] * inv_freq[None, :]
    cos = jnp.cos(ang)
    sin = jnp.sin(ang)
    reps = V7X_LANES // HEAD_DIM
    cos_t = jnp.tile(jnp.concatenate([cos, cos], axis=1), (1, reps))
    sin_t = jnp.tile(jnp.concatenate([-sin, sin], axis=1), (1, reps))
    return cos_t, sin_t


def kernel(x, c, w_ada, b_ada, norm1_g, w_in, sinks, ssm_lam_re, ssm_lam_im, ssm_log_step, ssm_b_re, ssm_b_im, ssm_c_re, ssm_c_im, ssm_d, w_glu, b_glu, attn_out_g, ssm_out_g, w_out, norm2_g, w_ff1, w_ff2, final_g):
    bsz, seq, d = x.shape
    t = bsz * seq
    x2d = x.reshape(t, d)

    c_pad = jnp.pad(c, ((0, V7X_SUBLANES - bsz), (0, 0)))
    mod = _adaln(c_pad, w_ada[0], b_ada[0].reshape(1, -1))
    mod3 = mod[:bsz].reshape(bsz, N_MOD, d)

    cos_t, sin_t = _rope_tables(seq)
    q, kvd, u_c = _in_proj(x2d, mod3, norm1_g[0].reshape(1, d), w_in[0].astype(BF16), cos_t, sin_t, seq)

    attn_n = _attention(q, kvd, sinks[0], attn_out_g[0].reshape(1, -1), bsz, seq)

    rev, bpow, cpow, mu = _ssm_operators(ssm_lam_re[0], ssm_lam_im[0], ssm_log_step[0], ssm_b_re[0],
                                         ssm_b_im[0], ssm_c_re[0], ssm_c_im[0], ssm_d[0])
    y = _ssm(u_c, rev, bpow, cpow, mu, bsz, seq)
    ssm_n = _glu(y, w_glu[0].astype(BF16), b_glu[0].reshape(1, -1), ssm_out_g[0].reshape(1, -1))

    x1 = _out_proj(attn_n, ssm_n, w_out[0].astype(BF16), x2d, mod3, seq)
    out = _ffn(x1, mod3, norm2_g[0].reshape(1, d), w_ff1[0].astype(BF16), w_ff2[0].astype(BF16),
               final_g.reshape(1, d), seq)
    return out.reshape(bsz, seq, d)
```

```python
import functools
import math

import jax
import jax.numpy as jnp
from jax import lax
from jax.experimental import pallas as pl
from jax.experimental.pallas import tpu as pltpu

D_MODEL = 4096
D_ATTN = 2048
D_SSM = 2048
HEAD_DIM = 64
N_Q_HEADS = 32
N_KV_HEADS = 4
Q_PER_KV = 8
D_KV = 256
WINDOW = 128
ROPE_THETA = 10000.0
SSM_GROUP = 16
N_SSM_GROUPS = 128
STATE = 64
D_FF = 4 * D_MODEL
N_MOD = 6
EPS = 1e-6

V7X_LANES = 128
V7X_SUBLANES = 8
V7X_VMEM_BYTES = 64 * 1024 * 1024

CHUNK = 16
N_SEG = 4
SSM_BATCH_SPLIT = 2
SLAB_GROUPS = V7X_LANES // SSM_GROUP
N_SLABS = N_SSM_GROUPS // SLAB_GROUPS
SLAB_COLS = CHUNK * V7X_LANES
SLAB_STATE = SLAB_GROUPS * STATE
TILE = 256
STEPS_PER_TILE = TILE // V7X_LANES
SCAN_UNROLL = 8

N_WBUF = 2

BF16 = jnp.bfloat16
F32 = jnp.float32


def _vmem_limit(nbytes):
    return int(min(nbytes + 8 * 1024 * 1024, V7X_VMEM_BYTES - 4 * 1024 * 1024))


def _rms_scale(xf, width):
    return lax.rsqrt(jnp.sum(xf * xf, axis=-1, keepdims=True) * (1.0 / width) + EPS)


NORM_ROWS = 16
NORM_UNROLL = 4


def _modulated_norm(x_ref, gain, shift, mod_scr, out_ref):
    width = x_ref.shape[1]
    mod_scr[0] = jnp.broadcast_to(gain, (V7X_SUBLANES, width))
    mod_scr[1] = jnp.broadcast_to(shift, (V7X_SUBLANES, width))
    reps = NORM_ROWS // V7X_SUBLANES

    def body(r, carry):
        rows = pl.ds(pl.multiple_of(r * NORM_ROWS, NORM_ROWS), NORM_ROWS)
        xf = x_ref[rows, :]
        g = jnp.concatenate([mod_scr[0]] * reps, axis=0)
        b = jnp.concatenate([mod_scr[1]] * reps, axis=0)
        out_ref[rows, :] = (xf * _rms_scale(xf, width) * g + b).astype(out_ref.dtype)
        return carry

    lax.fori_loop(0, x_ref.shape[0] // NORM_ROWS, body, 0, unroll=NORM_UNROLL)


def _adaln_kernel(c_ref, w_ref, b_ref, o_ref):
    c = c_ref[...]
    ca = (c * jax.nn.sigmoid(c)).astype(BF16)
    acc = jnp.dot(ca, w_ref[...].astype(BF16), preferred_element_type=F32)
    o_ref[...] = acc + b_ref[...]


def _adaln(c_pad, w_ada, b_ada, tn=1024):
    m, d = c_pad.shape
    n = w_ada.shape[1]
    return pl.pallas_call(
        _adaln_kernel,
        grid=(n // tn,),
        in_specs=[
            pl.BlockSpec((m, d), lambda j: (0, 0)),
            pl.BlockSpec((d, tn), lambda j: (0, j)),
            pl.BlockSpec((1, tn), lambda j: (0, j)),
        ],
        out_specs=pl.BlockSpec((m, tn), lambda j: (0, j)),
        out_shape=jax.ShapeDtypeStruct((m, n), F32),
        compiler_params=pltpu.CompilerParams(
            dimension_semantics=("arbitrary",),
            vmem_limit_bytes=_vmem_limit(2 * d * tn * 4 + d * tn * 2)),
        name="adaln",
    )(c_pad, w_ada, b_ada)


def _rope(acc, cos, sin_signed):
    width = acc.shape[1]
    lane = lax.broadcasted_iota(jnp.int32, acc.shape, 1)
    first_half = (lane % HEAD_DIM) < (HEAD_DIM // 2)
    partner = jnp.where(first_half,
                        pltpu.roll(acc, width - HEAD_DIM // 2, 1),
                        pltpu.roll(acc, HEAD_DIM // 2, 1))
    reps = width // cos.shape[1]
    cos_w = jnp.concatenate([cos] * reps, axis=1)
    sin_w = jnp.concatenate([sin_signed] * reps, axis=1)
    return acc * cos_w + partner * sin_w


def _in_proj_kernel(x_ref, mod_ref, g_ref, w_hbm, cos_ref, sin_ref,
                    q_ref, kv_ref, u_ref, h_scr, u_scr, gain_scr, w_buf, sem, *, tn):
    n_q = D_ATTN // tn
    n_blocks = w_hbm.shape[0]
    slabs_per_blk = tn // V7X_LANES
    chunk_rows = u_ref.shape[1]

    def w_copy(j, slot):
        return pltpu.make_async_copy(w_hbm.at[j], w_buf.at[slot], sem.at[slot])

    def finish(j, acc):
        if j < n_q:
            q = _rope(acc, cos_ref[...], sin_ref[...]) * (HEAD_DIM ** -0.5)
            q_ref[:, j * tn:(j + 1) * tn] = q.astype(BF16)
        elif j == n_q:
            k = _rope(acc[:, :D_KV], cos_ref[...], sin_ref[...])
            kv = jnp.concatenate([k, acc[:, D_KV:]], axis=1)
            lane = lax.broadcasted_iota(jnp.int32, (kv.shape[0], V7X_LANES), 1)
            left = lane < HEAD_DIM
            pieces = []
            for c0 in range(0, 2 * D_KV, V7X_LANES):
                a = kv[:, c0:c0 + V7X_LANES]
                s = pltpu.roll(a, HEAD_DIM, 1)
                pieces.append(jnp.where(left, a, s))
                pieces.append(jnp.where(left, s, a))
            kv_ref[...] = jnp.concatenate(pieces, axis=1).astype(BF16)
        else:
            slab0 = (j - n_q - 1) * slabs_per_blk
            for s in range(slabs_per_blk):
                u_scr[s] = acc[:, s * V7X_LANES:(s + 1) * V7X_LANES]
            for s in range(slabs_per_blk):
                for step in range(CHUNK):
                    rows = u_scr[s, pl.ds(step, chunk_rows, stride=CHUNK), :]
                    u_ref[slab0 + s, :, step * V7X_LANES:(step + 1) * V7X_LANES] = rows.astype(BF16)

    w_copy(0, 0).start()
    _modulated_norm(x_ref, g_ref[...] * (1.0 + mod_ref[0, 1:2, :]), mod_ref[0, 0:1, :], gain_scr, h_scr)

    prev = None
    for j in range(n_blocks):
        slot = j % N_WBUF
        if j + 1 < n_blocks:
            w_copy(j + 1, (j + 1) % N_WBUF).start()
        w_copy(j, slot).wait()
        acc = jnp.dot(h_scr[...], w_buf[slot], preferred_element_type=F32)
        if prev is not None:
            finish(j - 1, prev)
        prev = acc
    finish(n_blocks - 1, prev)


def _in_proj(x2d, mod3, g1, w_in, cos_t, sin_t, seq, tm=512, tn=512):
    t, d = x2d.shape
    assert 2 * D_KV == tn and w_in.shape == ((D_ATTN + tn + D_SSM) // tn, d, tn)
    slabs_per_blk = tn // V7X_LANES
    nb = seq // tm
    est = (2 * tm * d * 4 + tm * d * 2 + N_WBUF * d * tn * 2 + 2 * tm * D_ATTN * 2
           + 2 * tm * 4 * D_KV * 2 + 2 * tm * D_SSM * 2 + 4 * tm * V7X_LANES * 4
           + slabs_per_blk * tm * V7X_LANES * 4 + 6 * tm * tn * 4)
    return pl.pallas_call(
        functools.partial(_in_proj_kernel, tn=tn),
        grid=(t // tm,),
        in_specs=[
            pl.BlockSpec((tm, d), lambda i: (i, 0)),
            pl.BlockSpec((1, N_MOD, d), lambda i: (i // nb, 0, 0)),
            pl.BlockSpec((1, d), lambda i: (0, 0)),
            pl.BlockSpec(memory_space=pl.ANY),
            pl.BlockSpec((tm, V7X_LANES), lambda i: (i % nb, 0)),
            pl.BlockSpec((tm, V7X_LANES), lambda i: (i % nb, 0)),
        ],
        out_specs=[
            pl.BlockSpec((tm, D_ATTN), lambda i: (i, 0)),
            pl.BlockSpec((tm, 4 * D_KV), lambda i: (i, 0)),
            pl.BlockSpec((N_SLABS, tm // CHUNK, SLAB_COLS), lambda i: (0, i, 0)),
        ],
        out_shape=[
            jax.ShapeDtypeStruct((t, D_ATTN), BF16),
            jax.ShapeDtypeStruct((t, 4 * D_KV), BF16),
            jax.ShapeDtypeStruct((N_SLABS, t // CHUNK, SLAB_COLS), BF16),
        ],
        scratch_shapes=[
            pltpu.VMEM((tm, d), BF16),
            pltpu.VMEM((slabs_per_blk, tm, V7X_LANES), F32),
            pltpu.VMEM((2, V7X_SUBLANES, d), F32),
            pltpu.VMEM((N_WBUF, d, tn), BF16),
            pltpu.SemaphoreType.DMA((N_WBUF,)),
        ],
        compiler_params=pltpu.CompilerParams(
            dimension_semantics=("arbitrary",),
            vmem_limit_bytes=_vmem_limit(est)),
        name="in_proj",
    )(x2d, mod3, g1, w_in, cos_t, sin_t)


def _attn_kernel(sink_ref, q_ref, kvc_ref, kvp_ref, g_ref, o_ref, o_scr, cap_scr):
    n = pl.program_id(1)
    blk = WINDOW
    pair_w = 2 * HEAD_DIM
    n_keys = 2 * blk

    n_pairs = Q_PER_KV // 2
    rows = n_pairs * blk

    @pl.when((pl.program_id(0) == 0) & (n == 0))
    def _():
        qi = lax.broadcasted_iota(jnp.int32, (blk, 2 * n_keys), 0)
        key = lax.broadcasted_iota(jnp.int32, (blk, 2 * n_keys), 1) % n_keys
        rel = qi + blk - key
        band = (rel >= 0) & (rel < WINDOW)
        cap_scr[0] = jnp.where(band & (key >= blk), jnp.inf, F32(-1e30))
        cap_scr[1] = jnp.where(band, jnp.inf, F32(-1e30))

    cap = cap_scr[jnp.minimum(n, 1)]
    pair_of_row = lax.broadcasted_iota(jnp.int32, (rows, 1), 0) // blk

    lane = lax.broadcasted_iota(jnp.int32, (n_keys, pair_w), 1)
    left = lane < HEAD_DIM
    left_o = lax.broadcasted_iota(jnp.int32, (rows, pair_w), 1) < HEAD_DIM
    zero = jnp.zeros((n_keys, pair_w), BF16)

    for h in range(N_KV_HEADS):
        kcol = h * pair_w
        vcol = N_KV_HEADS * pair_w + h * pair_w
        kd = jnp.concatenate([kvp_ref[:, kcol:kcol + pair_w], kvc_ref[:, kcol:kcol + pair_w]], axis=0)
        vd = jnp.concatenate([kvp_ref[:, vcol:vcol + pair_w], kvc_ref[:, vcol:vcol + pair_w]], axis=0)
        k_bd = jnp.concatenate([jnp.where(left, kd, zero), jnp.where(left, zero, kd)], axis=0)
        v_bd = jnp.concatenate([jnp.where(left, vd, zero), jnp.where(left, zero, vd)], axis=0)
        head0 = h * Q_PER_KV
        q4 = jnp.concatenate([q_ref[:, (head0 + 2 * p) * HEAD_DIM:(head0 + 2 * p + 2) * HEAD_DIM]
                              for p in range(n_pairs)], axis=0)
        sink_a = jnp.zeros((rows, 1), F32)
        sink_b = jnp.zeros((rows, 1), F32)
        for p in range(n_pairs):
            sink_a = jnp.where(pair_of_row == p, sink_ref[head0 + 2 * p], sink_a)
            sink_b = jnp.where(pair_of_row == p, sink_ref[head0 + 2 * p + 1], sink_b)
        s = lax.dot_general(q4, k_bd, (((1,), (1,)), ((), ())), preferred_element_type=F32)
        s = jnp.minimum(s.reshape(n_pairs, blk, 2 * n_keys), cap[None]).reshape(rows, 2 * n_keys)
        m_a = jnp.maximum(jnp.max(s[:, :n_keys], axis=-1, keepdims=True), sink_a)
        m_b = jnp.maximum(jnp.max(s[:, n_keys:], axis=-1, keepdims=True), sink_b)
        p_a = jnp.exp(s[:, :n_keys] - m_a)
        p_b = jnp.exp(s[:, n_keys:] - m_b)
        l_a = jnp.sum(p_a, axis=-1, keepdims=True) + jnp.exp(sink_a - m_a)
        l_b = jnp.sum(p_b, axis=-1, keepdims=True) + jnp.exp(sink_b - m_b)
        pp = jnp.concatenate([p_a, p_b], axis=1).astype(BF16)
        o4 = jnp.dot(pp, v_bd, preferred_element_type=F32) * jnp.where(left_o, 1.0 / l_a, 1.0 / l_b)
        for p in range(n_pairs):
            qcol = (head0 + 2 * p) * HEAD_DIM
            o_scr[:, qcol:qcol + pair_w] = o4[p * blk:(p + 1) * blk, :]

    o = o_scr[...]
    o_ref[...] = (o * _rms_scale(o, D_ATTN) * g_ref[...]).astype(BF16)


def _attention(q, kvd, sinks, g_attn, batch, seq):
    t = q.shape[0]
    nb = seq // WINDOW
    kvw = kvd.shape[1]
    return pl.pallas_call(
        _attn_kernel,
        grid=(batch, nb),
        in_specs=[
            pl.BlockSpec(memory_space=pltpu.SMEM),
            pl.BlockSpec((WINDOW, D_ATTN), lambda b, n: (b * nb + n, 0)),
            pl.BlockSpec((WINDOW, kvw), lambda b, n: (b * nb + n, 0)),
            pl.BlockSpec((WINDOW, kvw), lambda b, n: (b * nb + jnp.maximum(n - 1, 0), 0)),
            pl.BlockSpec((1, D_ATTN), lambda b, n: (0, 0)),
        ],
        out_specs=pl.BlockSpec((WINDOW, D_ATTN), lambda b, n: (b * nb + n, 0)),
        out_shape=jax.ShapeDtypeStruct((t, D_ATTN), BF16),
        scratch_shapes=[pltpu.VMEM((WINDOW, D_ATTN), F32), pltpu.VMEM((2, WINDOW, 4 * WINDOW), F32)],
        compiler_params=pltpu.CompilerParams(
            dimension_semantics=("arbitrary", "arbitrary"),
            vmem_limit_bytes=_vmem_limit(8 * 1024 * 1024)),
        name="attention",
    )(sinks, q, kvd, kvd, g_attn)


def _cmul(a_re, a_im, b_re, b_im):
    return a_re * b_re - a_im * b_im, a_re * b_im + a_im * b_re


def _ssm_kernel(u_ref, rev_ref, bpow_ref, cpow_ref, mu_ref, o_ref, z_scr, sp_scr, *, seg_chunks):
    n_k = SLAB_STATE // V7X_LANES
    rows = u_ref.shape[1]
    seqs = rows // seg_chunks
    shape = (seqs, V7X_LANES)

    pitch = z_scr.shape[1] // seqs

    for nb in range(2 * SLAB_STATE // TILE):
        zz = jnp.dot(u_ref[0], bpow_ref[0, :, nb * TILE:(nb + 1) * TILE], preferred_element_type=F32)
        for q in range(seqs):
            src = slice(q * seg_chunks, (q + 1) * seg_chunks)
            dst = slice(q * pitch, q * pitch + seg_chunks)
            z_scr[2 * nb, dst, :] = zz[src, :V7X_LANES]
            z_scr[2 * nb + 1, dst, :] = zz[src, V7X_LANES:]

    mu_re = [jnp.broadcast_to(mu_ref[0, 0:1, k * V7X_LANES:(k + 1) * V7X_LANES], shape) for k in range(n_k)]
    mu_im = [jnp.broadcast_to(mu_ref[0, 1:2, k * V7X_LANES:(k + 1) * V7X_LANES], shape) for k in range(n_k)]

    def rows_at(c):
        return pl.ds(c, seqs, stride=pitch)

    def scan_step(c, carry):
        new = []
        for k in range(n_k):
            s_re, s_im = carry[2 * k], carry[2 * k + 1]
            sp_scr[k, rows_at(c), :] = s_re
            sp_scr[n_k + k, rows_at(c), :] = s_im
            p_re, p_im = _cmul(mu_re[k], mu_im[k], s_re, s_im)
            new += [p_re + z_scr[k, rows_at(c), :], p_im + z_scr[n_k + k, rows_at(c), :]]
        return tuple(new)

    zeros = jnp.zeros(shape, F32)
    final = lax.fori_loop(0, seg_chunks, scan_step, (zeros,) * (2 * n_k), unroll=SCAN_UNROLL)

    first_seg = (lax.broadcasted_iota(jnp.int32, shape, 0) % N_SEG) == 0
    init = []
    for k in range(n_k):
        m_re, m_im = mu_re[k], mu_im[k]
        for _ in range(int(math.log2(seg_chunks))):
            m_re, m_im = _cmul(m_re, m_im, m_re, m_im)
        i_re, i_im = zeros, zeros
        for _ in range(N_SEG - 1):
            t_re, t_im = _cmul(m_re, m_im, i_re, i_im)
            i_re = jnp.where(first_seg, 0.0, pltpu.roll(t_re + final[2 * k], 1, 0))
            i_im = jnp.where(first_seg, 0.0, pltpu.roll(t_im + final[2 * k + 1], 1, 0))
        init += [i_re, i_im]

    def fix_step(c, carry):
        new = []
        for k in range(n_k):
            c_re, c_im = carry[2 * k], carry[2 * k + 1]
            sp_scr[k, rows_at(c), :] += c_re
            sp_scr[n_k + k, rows_at(c), :] += c_im
            new += list(_cmul(mu_re[k], mu_im[k], c_re, c_im))
        return tuple(new)

    lax.fori_loop(0, seg_chunks, fix_step, tuple(init), unroll=SCAN_UNROLL)

    sp = jnp.concatenate(
        [jnp.concatenate([sp_scr[k, q * pitch:q * pitch + seg_chunks, :] for q in range(seqs)], axis=0)
         for k in range(2 * n_k)], axis=1).astype(BF16)
    for jt in range(SLAB_COLS // TILE):
        k_len = (jt + 1) * TILE
        y = jnp.dot(u_ref[0, :, :k_len], rev_ref[0, SLAB_COLS - k_len:, :], preferred_element_type=F32)
        y = y + jnp.dot(sp, cpow_ref[0, :, jt * TILE:(jt + 1) * TILE], preferred_element_type=F32)
        y = 0.5 * y * (1.0 + lax.erf(y * (2.0 ** -0.5)))
        for q in range(STEPS_PER_TILE):
            step = jt * STEPS_PER_TILE + q
            o_ref[pl.ds(step, rows, stride=CHUNK), :] = y[:, q * V7X_LANES:(q + 1) * V7X_LANES]


def _ssm(u_c, rev, bpow, cpow, mu, bsz, seq):
    n_slabs, chunk_rows, _ = u_c.shape
    t = chunk_rows * CHUNK
    rows = chunk_rows // SSM_BATCH_SPLIT
    seg_chunks = seq // (CHUNK * N_SEG)
    assert bsz % SSM_BATCH_SPLIT == 0 and rows // seg_chunks == V7X_SUBLANES
    assert seg_chunks & (seg_chunks - 1) == 0
    assert (seg_chunks // V7X_SUBLANES) % 2 == 0
    scan_rows = V7X_SUBLANES * (seg_chunks + V7X_SUBLANES)
    kern = functools.partial(_ssm_kernel, seg_chunks=seg_chunks)
    est = (2 * rows * SLAB_COLS * 2 + 2 * SLAB_COLS * TILE * 2 + 4 * SLAB_COLS * 2 * SLAB_STATE * 2
           + 2 * rows * CHUNK * V7X_LANES * 4 + 2 * rows * 2 * SLAB_STATE * 4 + 8 * rows * TILE * 4
           + rows * 2 * SLAB_STATE * 2)
    return pl.pallas_call(
        kern,
        grid=(n_slabs, SSM_BATCH_SPLIT),
        in_specs=[
            pl.BlockSpec((1, rows, SLAB_COLS), lambda s, h: (s, h, 0)),
            pl.BlockSpec((1, SLAB_COLS, TILE), lambda s, h: (s, 0, 0)),
            pl.BlockSpec((1, SLAB_COLS, 2 * SLAB_STATE), lambda s, h: (s, 0, 0)),
            pl.BlockSpec((1, 2 * SLAB_STATE, SLAB_COLS), lambda s, h: (s, 0, 0)),
            pl.BlockSpec((1, 2, SLAB_STATE), lambda s, h: (s, 0, 0)),
        ],
        out_specs=pl.BlockSpec((rows * CHUNK, V7X_LANES), lambda s, h: (h, s)),
        out_shape=jax.ShapeDtypeStruct((t, n_slabs * V7X_LANES), F32),
        scratch_shapes=[pltpu.VMEM((2 * SLAB_STATE // V7X_LANES, scan_rows, V7X_LANES), F32),
                        pltpu.VMEM((2 * SLAB_STATE // V7X_LANES, scan_rows, V7X_LANES), F32)],
        compiler_params=pltpu.CompilerParams(
            dimension_semantics=("arbitrary", "arbitrary"),
            vmem_limit_bytes=_vmem_limit(est)),
        name="ssm",
    )(u_c, rev, bpow, cpow, mu)


def _zoh(lam_re, lam_im, log_step):
    step = jnp.exp(log_step)
    mag = jnp.exp(lam_re * step)
    bar_re = mag * jnp.cos(lam_im * step)
    bar_im = mag * jnp.sin(lam_im * step)
    num_re, num_im = bar_re - 1.0, bar_im
    inv_den = 1.0 / (lam_re * lam_re + lam_im * lam_im)
    coef_re = (num_re * lam_re + num_im * lam_im) * inv_den
    coef_im = (num_im * lam_re - num_re * lam_im) * inv_den
    return bar_re, bar_im, coef_re, coef_im


def _ssm_ops_kernel(lam_r_ref, bt_re_ref, bt_im_ref, ct_re_ref, ct_im_ref, d_ref,
                    rev_ref, bpow_ref, cpow_ref, mu_ref, km_scr):
    lanes = V7X_LANES
    bar_re, bar_im, coef_re, coef_im = _zoh(lam_r_ref[0, 0:1, :], lam_r_ref[0, 1:2, :], lam_r_ref[0, 2:3, :])
    bb_re, bb_im = _cmul(bt_re_ref[0], bt_im_ref[0], coef_re, coef_im)
    p_re, p_im = jnp.ones_like(bar_re), jnp.zeros_like(bar_re)
    for jp in range(CHUNK - 1, -1, -1):
        r0 = jp * lanes
        blk_re, blk_im = _cmul(bb_re, bb_im, p_re, p_im)
        bpow_ref[0, r0:r0 + lanes, :SLAB_STATE] = blk_re.astype(BF16)
        bpow_ref[0, r0:r0 + lanes, SLAB_STATE:] = blk_im.astype(BF16)
        p_re, p_im = _cmul(p_re, p_im, bar_re, bar_im)
    mu_ref[0, 0:1, :] = p_re
    mu_ref[0, 1:2, :] = p_im

    cbar_re = jnp.broadcast_to(bar_re, (lanes, SLAB_STATE)).T
    cbar_im = jnp.broadcast_to(bar_im, (lanes, SLAB_STATE)).T
    w_re, w_im = ct_re_ref[0], ct_im_ref[0]
    row_i = lax.broadcasted_iota(jnp.int32, (lanes, lanes), 0)
    col_i = lax.broadcasted_iota(jnp.int32, (lanes, lanes), 1)
    skip = jnp.where(row_i == col_i, d_ref[0], 0.0)
    bb_cat = jnp.concatenate([bb_re, -bb_im], axis=1).astype(BF16)
    for m in range(CHUNK):
        w_cat = jnp.concatenate([w_re, w_im], axis=0).astype(BF16)
        km = jnp.dot(bb_cat, w_cat, preferred_element_type=F32)
        km_scr[m] = km + skip if m == 0 else km
        w_re, w_im = _cmul(w_re, w_im, cbar_re, cbar_im)
        cpow_ref[0, :SLAB_STATE, m * lanes:(m + 1) * lanes] = w_re.astype(BF16)
        cpow_ref[0, SLAB_STATE:, m * lanes:(m + 1) * lanes] = (-w_im).astype(BF16)

    assert STEPS_PER_TILE == 2
    n_tiles = SLAB_COLS // TILE
    for i in range(n_tiles):
        d = n_tiles - 1 - i
        r0 = i * TILE
        below = km_scr[2 * d - 1] if d > 0 else jnp.zeros((lanes, lanes), F32)
        rev_ref[0, r0:r0 + lanes, :lanes] = km_scr[2 * d].astype(BF16)
        rev_ref[0, r0:r0 + lanes, lanes:] = km_scr[2 * d + 1].astype(BF16)
        rev_ref[0, r0 + lanes:r0 + TILE, :lanes] = below.astype(BF16)
        rev_ref[0, r0 + lanes:r0 + TILE, lanes:] = km_scr[2 * d].astype(BF16)


def _ssm_operators(lam_re, lam_im, log_step, b_re, b_im, c_re, c_im, d_skip):
    s, sg = N_SLABS, SLAB_GROUPS
    eye_g = jnp.eye(sg, dtype=F32)

    def per_state(a):
        return a.reshape(s, SLAB_STATE)

    step_gp = jnp.broadcast_to(log_step[:, None], lam_re.shape)
    lam_rows = jnp.stack([per_state(lam_re), per_state(lam_im), per_state(step_gp)], axis=1)

    def block_diag_bt(b):
        bt = b.reshape(s, sg, STATE, SSM_GROUP).transpose(0, 1, 3, 2)
        return (bt[:, :, :, None, :] * eye_g[None, :, None, :, None]).reshape(s, V7X_LANES, SLAB_STATE)

    def block_diag_ct(c):
        ct = c.reshape(s, sg, SSM_GROUP, STATE).transpose(0, 1, 3, 2)
        return (ct[:, :, :, None, :] * eye_g[None, :, None, :, None]).reshape(s, SLAB_STATE, V7X_LANES)

    d_rows = d_skip.reshape(s, 1, V7X_LANES)
    operands = (lam_rows, block_diag_bt(b_re), block_diag_bt(b_im),
                block_diag_ct(c_re), block_diag_ct(c_im), d_rows)
    return pl.pallas_call(
        _ssm_ops_kernel,
        grid=(s,),
        in_specs=[
            pl.BlockSpec((1, 3, SLAB_STATE), lambda i: (i, 0, 0)),
            pl.BlockSpec((1, V7X_LANES, SLAB_STATE), lambda i: (i, 0, 0)),
            pl.BlockSpec((1, V7X_LANES, SLAB_STATE), lambda i: (i, 0, 0)),
            pl.BlockSpec((1, SLAB_STATE, V7X_LANES), lambda i: (i, 0, 0)),
            pl.BlockSpec((1, SLAB_STATE, V7X_LANES), lambda i: (i, 0, 0)),
            pl.BlockSpec((1, 1, V7X_LANES), lambda i: (i, 0, 0)),
        ],
        out_specs=[
            pl.BlockSpec((1, SLAB_COLS, TILE), lambda i: (i, 0, 0)),
            pl.BlockSpec((1, SLAB_COLS, 2 * SLAB_STATE), lambda i: (i, 0, 0)),
            pl.BlockSpec((1, 2 * SLAB_STATE, SLAB_COLS), lambda i: (i, 0, 0)),
            pl.BlockSpec((1, 2, SLAB_STATE), lambda i: (i, 0, 0)),
        ],
        out_shape=[
            jax.ShapeDtypeStruct((s, SLAB_COLS, TILE), BF16),
            jax.ShapeDtypeStruct((s, SLAB_COLS, 2 * SLAB_STATE), BF16),
            jax.ShapeDtypeStruct((s, 2 * SLAB_STATE, SLAB_COLS), BF16),
            jax.ShapeDtypeStruct((s, 2, SLAB_STATE), F32),
        ],
        scratch_shapes=[pltpu.VMEM((CHUNK, V7X_LANES, V7X_LANES), F32)],
        compiler_params=pltpu.CompilerParams(
            dimension_semantics=("arbitrary",),
            vmem_limit_bytes=_vmem_limit(2 * (SLAB_COLS * TILE + 4 * SLAB_COLS * SLAB_STATE) * 2
                                         + 24 * SLAB_STATE * V7X_LANES * 4)),
        name="ssm_ops",
    )(*operands)


def _glu_kernel(y_ref, w_ref, b_ref, g_ref, o_ref):
    y = y_ref[...]
    z = jnp.dot(y.astype(BF16), w_ref[...], preferred_element_type=F32) + b_ref[...]
    out = y * jax.nn.sigmoid(z)
    o_ref[...] = (out * _rms_scale(out, D_SSM) * g_ref[...]).astype(BF16)


def _glu(y, w_glu, b_glu, g_ssm, tm=512):
    t, d = y.shape
    est = 2 * tm * d * 4 + 2 * tm * d * 2 + 2 * d * d * 2 + 4 * tm * d * 4
    return pl.pallas_call(
        _glu_kernel,
        grid=(t // tm,),
        in_specs=[
            pl.BlockSpec((tm, d), lambda i: (i, 0)),
            pl.BlockSpec((d, d), lambda i: (0, 0)),
            pl.BlockSpec((1, d), lambda i: (0, 0)),
            pl.BlockSpec((1, d), lambda i: (0, 0)),
        ],
        out_specs=pl.BlockSpec((tm, d), lambda i: (i, 0)),
        out_shape=jax.ShapeDtypeStruct((t, d), BF16),
        compiler_params=pltpu.CompilerParams(
            dimension_semantics=("arbitrary",),
            vmem_limit_bytes=_vmem_limit(est)),
        name="glu",
    )(y, w_glu, b_glu, g_ssm)


def _out_proj_kernel(a_ref, s_ref, wa_ref, ws_ref, x_ref, mod_ref, o_ref):
    acc = jnp.dot(a_ref[...], wa_ref[...], preferred_element_type=F32)
    acc = acc + jnp.dot(s_ref[...], ws_ref[...], preferred_element_type=F32)
    o_ref[...] = x_ref[...] + mod_ref[0, 2:3, :] * acc


def _out_proj(attn_n, ssm_n, w_out, x2d, mod3, seq, tm=512, tn=1024):
    t, d = x2d.shape
    nb = seq // tm
    ka = attn_n.shape[1]
    est = 2 * 2 * tm * ka * 2 + 2 * 2 * ka * tn * 2 + 4 * tm * tn * 4 + 2 * tm * tn * 4
    return pl.pallas_call(
        _out_proj_kernel,
        grid=(t // tm, d // tn),
        in_specs=[
            pl.BlockSpec((tm, ka), lambda i, j: (i, 0)),
            pl.BlockSpec((tm, ka), lambda i, j: (i, 0)),
            pl.BlockSpec((ka, tn), lambda i, j: (0, j)),
            pl.BlockSpec((ka, tn), lambda i, j: (1, j)),
            pl.BlockSpec((tm, tn), lambda i, j: (i, j)),
            pl.BlockSpec((1, N_MOD, tn), lambda i, j: (i // nb, 0, j)),
        ],
        out_specs=pl.BlockSpec((tm, tn), lambda i, j: (i, j)),
        out_shape=jax.ShapeDtypeStruct((t, d), F32),
        compiler_params=pltpu.CompilerParams(
            dimension_semantics=("arbitrary", "arbitrary"),
            vmem_limit_bytes=_vmem_limit(est)),
        name="out_proj",
    )(attn_n, ssm_n, w_out, w_out, x2d, mod3)


def _ffn_kernel(x_ref, mod_ref, g2_ref, w1_hbm, w2_hbm, gf_ref, o_ref,
                h_scr, gain_scr, rs_scr, w1_buf, w2_buf, sem, *, tf):
    n_blocks = w1_hbm.shape[0]

    def w_copies(k, slot):
        rows = pl.ds(pl.multiple_of(k * tf, tf), tf)
        return (pltpu.make_async_copy(w1_hbm.at[k], w1_buf.at[slot], sem.at[0, slot]),
                pltpu.make_async_copy(w2_hbm.at[rows, :], w2_buf.at[slot], sem.at[1, slot]))

    def start(k, slot):
        for cp in w_copies(k, slot):
            cp.start()

    def wait(k, slot):
        for cp in w_copies(k, slot):
            cp.wait()

    start(0, 0)
    _modulated_norm(x_ref, g2_ref[...] * (1.0 + mod_ref[0, 4:5, :]), mod_ref[0, 3:4, :], gain_scr, h_scr)
    o_ref[...] = jnp.zeros_like(o_ref)

    def block(k, slot):
        @pl.when(k + 1 < n_blocks)
        def _():
            start(k + 1, (slot + 1) % N_WBUF)

        wait(k, slot)
        a = jnp.dot(h_scr[...], w1_buf[slot], preferred_element_type=F32)
        a = jnp.square(jnp.maximum(a, 0.0)).astype(BF16)
        for n0 in range(0, D_MODEL, tf):
            o_ref[:, n0:n0 + tf] += jnp.dot(a, w2_buf[slot, :, n0:n0 + tf], preferred_element_type=F32)

    def trip(kk, carry):
        for slot in range(N_WBUF):
            block(kk * N_WBUF + slot, slot)
        return carry

    lax.fori_loop(0, n_blocks // N_WBUF, trip, 0)

    gain_scr[0] = jnp.broadcast_to(mod_ref[0, 5:6, :], (V7X_SUBLANES, D_MODEL))
    gain_scr[1] = jnp.broadcast_to(gf_ref[...], (V7X_SUBLANES, D_MODEL))

    def scale_rows(r, carry):
        rows = pl.ds(pl.multiple_of(r * V7X_SUBLANES, V7X_SUBLANES), V7X_SUBLANES)
        x2 = x_ref[rows, :] + gain_scr[0] * o_ref[rows, :]
        rs_scr[rows, :] = jnp.broadcast_to(_rms_scale(x2, D_MODEL), (V7X_SUBLANES, V7X_LANES))
        return carry

    lax.fori_loop(0, o_ref.shape[0] // V7X_SUBLANES, scale_rows, 0, unroll=NORM_UNROLL)

    def rescale_rows(r, carry):
        rows = pl.ds(pl.multiple_of(r * V7X_SUBLANES, V7X_SUBLANES), V7X_SUBLANES)
        x2 = x_ref[rows, :] + gain_scr[0] * o_ref[rows, :]
        rs = jnp.concatenate([rs_scr[rows, :]] * (D_MODEL // V7X_LANES), axis=1)
        o_ref[rows, :] = x2 * rs * gain_scr[1]
        return carry

    lax.fori_loop(0, o_ref.shape[0] // V7X_SUBLANES, rescale_rows, 0, unroll=NORM_UNROLL)


def _ffn(x1, mod3, g2, w1, w2, gf, seq, tm=512, tf=512):
    t, d = x1.shape
    nb = seq // tm
    assert w1.shape[1:] == (d, tf) and w1.shape[0] % N_WBUF == 0 and w2.shape == (w1.shape[0] * tf, d)
    est = (2 * tm * d * 4 + 2 * tm * d * 4 + tm * d * 2 + N_WBUF * 2 * d * tf * 2
           + tm * V7X_LANES * 4 + 4 * tm * tf * 4)
    return pl.pallas_call(
        functools.partial(_ffn_kernel, tf=tf),
        grid=(t // tm,),
        in_specs=[
            pl.BlockSpec((tm, d), lambda i: (i, 0)),
            pl.BlockSpec((1, N_MOD, d), lambda i: (i // nb, 0, 0)),
            pl.BlockSpec((1, d), lambda i: (0, 0)),
            pl.BlockSpec(memory_space=pl.ANY),
            pl.BlockSpec(memory_space=pl.ANY),
            pl.BlockSpec((1, d), lambda i: (0, 0)),
        ],
        out_specs=pl.BlockSpec((tm, d), lambda i: (i, 0)),
        out_shape=jax.ShapeDtypeStruct((t, d), F32),
        scratch_shapes=[
            pltpu.VMEM((tm, d), BF16),
            pltpu.VMEM((2, V7X_SUBLANES, d), F32),
            pltpu.VMEM((tm, V7X_LANES), F32),
            pltpu.VMEM((N_WBUF, d, tf), BF16),
            pltpu.VMEM((N_WBUF, tf, d), BF16),
            pltpu.SemaphoreType.DMA((2, N_WBUF)),
        ],
        compiler_params=pltpu.CompilerParams(
            dimension_semantics=("arbitrary",),
            vmem_limit_bytes=_vmem_limit(est)),
        name="ffn",
    )(x1, mod3, g2, w1, w2, gf)


W_BLOCK = 512


def _block_major(w, tn):
    d, n = w.shape
    return w.reshape(d, n // tn, tn).transpose(1, 0, 2).astype(BF16)


def _rope_tables(seq):
    half = HEAD_DIM // 2
    inv_freq = ROPE_THETA ** (-jnp.arange(half, dtype=F32) / half)
    ang = jnp.arange(seq, dtype=F32)[:, None] * inv_freq[None, :]
    cos = jnp.cos(ang)
    sin = jnp.sin(ang)
    reps = V7X_LANES // HEAD_DIM
    cos_t = jnp.tile(jnp.concatenate([cos, cos], axis=1), (1, reps))
    sin_t = jnp.tile(jnp.concatenate([-sin, sin], axis=1), (1, reps))
    return cos_t, sin_t


def kernel(x, c, w_ada, b_ada, norm1_g, w_in, sinks, ssm_lam_re, ssm_lam_im, ssm_log_step, ssm_b_re, ssm_b_im, ssm_c_re, ssm_c_im, ssm_d, w_glu, b_glu, attn_out_g, ssm_out_g, w_out, norm2_g, w_ff1, w_ff2, final_g):
    bsz, seq, d = x.shape
    t = bsz * seq
    x2d = x.reshape(t, d)

    c_pad = jnp.pad(c, ((0, V7X_SUBLANES - bsz), (0, 0)))
    mod = _adaln(c_pad, w_ada[0], b_ada[0].reshape(1, -1))
    mod3 = mod[:bsz].reshape(bsz, N_MOD, d)

    cos_t, sin_t = _rope_tables(seq)
    q, kvd, u_c = _in_proj(x2d, mod3, norm1_g[0].reshape(1, d), _block_major(w_in[0], W_BLOCK), cos_t, sin_t, seq,
                           tn=W_BLOCK)

    attn_n = _attention(q, kvd, sinks[0], attn_out_g[0].reshape(1, -1), bsz, seq)

    rev, bpow, cpow, mu = _ssm_operators(ssm_lam_re[0], ssm_lam_im[0], ssm_log_step[0], ssm_b_re[0],
                                         ssm_b_im[0], ssm_c_re[0], ssm_c_im[0], ssm_d[0])
    y = _ssm(u_c, rev, bpow, cpow, mu, bsz, seq)
    ssm_n = _glu(y, w_glu[0].astype(BF16), b_glu[0].reshape(1, -1), ssm_out_g[0].reshape(1, -1))

    x1 = _out_proj(attn_n, ssm_n, w_out[0].astype(BF16), x2d, mod3, seq)
    out = _ffn(x1, mod3, norm2_g[0].reshape(1, d), _block_major(w_ff1[0], W_BLOCK), w_ff2[0].astype(BF16),
               final_g.reshape(1, d), seq, tf=W_BLOCK)
    return out.reshape(bsz, seq, d)
```

```python
import functools
import math

import jax
import jax.numpy as jnp
from jax import lax
from jax.experimental import pallas as pl
from jax.experimental.pallas import tpu as pltpu

D_MODEL = 4096
D_ATTN = 2048
D_SSM = 2048
HEAD_DIM = 64
N_Q_HEADS = 32
N_KV_HEADS = 4
Q_PER_KV = 8
D_KV = 256
WINDOW = 128
ROPE_THETA = 10000.0
SSM_GROUP = 16
N_SSM_GROUPS = 128
STATE = 64
D_FF = 4 * D_MODEL
N_MOD = 6
EPS = 1e-6

V7X_LANES = 128
V7X_SUBLANES = 8
V7X_VMEM_BYTES = 64 * 1024 * 1024

CHUNK = 16
N_SEG = 4
SSM_BATCH_SPLIT = 2
SLAB_GROUPS = V7X_LANES // SSM_GROUP
N_SLABS = N_SSM_GROUPS // SLAB_GROUPS
SLAB_COLS = CHUNK * V7X_LANES
SLAB_STATE = SLAB_GROUPS * STATE
TILE = 256
STEPS_PER_TILE = TILE // V7X_LANES
SCAN_UNROLL = 8

N_WBUF = 2

BF16 = jnp.bfloat16
F32 = jnp.float32


def _vmem_limit(nbytes):
    return int(min(nbytes + 8 * 1024 * 1024, V7X_VMEM_BYTES - 4 * 1024 * 1024))


def _rms_scale(xf, width):
    return lax.rsqrt(jnp.sum(xf * xf, axis=-1, keepdims=True) * (1.0 / width) + EPS)


NORM_ROWS = 16
NORM_UNROLL = 4


def _modulated_norm(x_ref, gain, shift, mod_scr, out_ref):
    width = x_ref.shape[1]
    mod_scr[0] = jnp.broadcast_to(gain, (V7X_SUBLANES, width))
    mod_scr[1] = jnp.broadcast_to(shift, (V7X_SUBLANES, width))
    reps = NORM_ROWS // V7X_SUBLANES

    def body(r, carry):
        rows = pl.ds(pl.multiple_of(r * NORM_ROWS, NORM_ROWS), NORM_ROWS)
        xf = x_ref[rows, :]
        g = jnp.concatenate([mod_scr[0]] * reps, axis=0)
        b = jnp.concatenate([mod_scr[1]] * reps, axis=0)
        out_ref[rows, :] = (xf * _rms_scale(xf, width) * g + b).astype(out_ref.dtype)
        return carry

    lax.fori_loop(0, x_ref.shape[0] // NORM_ROWS, body, 0, unroll=NORM_UNROLL)


def _adaln_kernel(c_ref, w_ref, b_ref, o_ref):
    c = c_ref[...]
    ca = (c * jax.nn.sigmoid(c)).astype(BF16)
    acc = jnp.dot(ca, w_ref[...].astype(BF16), preferred_element_type=F32)
    o_ref[...] = acc + b_ref[...]


def _adaln(c_pad, w_ada, b_ada, tn=1024):
    m, d = c_pad.shape
    n = w_ada.shape[1]
    return pl.pallas_call(
        _adaln_kernel,
        grid=(n // tn,),
        in_specs=[
            pl.BlockSpec((m, d), lambda j: (0, 0)),
            pl.BlockSpec((d, tn), lambda j: (0, j)),
            pl.BlockSpec((1, tn), lambda j: (0, j)),
        ],
        out_specs=pl.BlockSpec((m, tn), lambda j: (0, j)),
        out_shape=jax.ShapeDtypeStruct((m, n), F32),
        compiler_params=pltpu.CompilerParams(
            dimension_semantics=("arbitrary",),
            vmem_limit_bytes=_vmem_limit(2 * d * tn * 4 + d * tn * 2)),
        name="adaln",
    )(c_pad, w_ada, b_ada)


def _rope(acc, cos, sin_signed):
    width = acc.shape[1]
    lane = lax.broadcasted_iota(jnp.int32, acc.shape, 1)
    first_half = (lane % HEAD_DIM) < (HEAD_DIM // 2)
    partner = jnp.where(first_half,
                        pltpu.roll(acc, width - HEAD_DIM // 2, 1),
                        pltpu.roll(acc, HEAD_DIM // 2, 1))
    reps = width // cos.shape[1]
    cos_w = jnp.concatenate([cos] * reps, axis=1)
    sin_w = jnp.concatenate([sin_signed] * reps, axis=1)
    return acc * cos_w + partner * sin_w


def _in_proj_kernel(x_ref, mod_ref, g_ref, w_hbm, cos_ref, sin_ref, f1_hbm, f2_hbm,
                    q_ref, kv_ref, u_ref, f1b_hbm, f2b_hbm,
                    h_scr, u_scr, gain_scr, w_buf, sem, cast_in_buf, cast_out_buf, cast_sem, *, tn, n_pieces):
    n_q = D_ATTN // tn
    n_blocks = w_hbm.shape[1] // tn
    slabs_per_blk = tn // V7X_LANES
    chunk_rows = u_ref.shape[1]
    piece_rows = cast_in_buf.shape[2]
    cast_src = (f1_hbm, f2_hbm)
    cast_dst = (f1b_hbm, f2b_hbm)

    def w_copy(j, slot):
        return pltpu.make_async_copy(w_hbm.at[:, j * tn:(j + 1) * tn], w_buf.at[slot], sem.at[slot])

    def piece(p):
        first = (pl.program_id(0) * n_pieces + p) * piece_rows
        return pl.ds(pl.multiple_of(first, piece_rows), piece_rows)

    def cast_in(p):
        return [pltpu.make_async_copy(cast_src[w].at[piece(p), :], cast_in_buf.at[w, p % N_WBUF],
                                      cast_sem.at[0, w, p % N_WBUF]) for w in range(2)]

    def cast_out(p):
        return [pltpu.make_async_copy(cast_out_buf.at[w, p % N_WBUF], cast_dst[w].at[piece(p), :],
                                      cast_sem.at[1, w, p % N_WBUF]) for w in range(2)]

    def finish(j, acc):
        if j < n_q:
            q = _rope(acc, cos_ref[...], sin_ref[...]) * (HEAD_DIM ** -0.5)
            q_ref[:, j * tn:(j + 1) * tn] = q.astype(BF16)
        elif j == n_q:
            k = _rope(acc[:, :D_KV], cos_ref[...], sin_ref[...])
            kv = jnp.concatenate([k, acc[:, D_KV:]], axis=1)
            lane = lax.broadcasted_iota(jnp.int32, (kv.shape[0], V7X_LANES), 1)
            left = lane < HEAD_DIM
            pieces = []
            for c0 in range(0, 2 * D_KV, V7X_LANES):
                a = kv[:, c0:c0 + V7X_LANES]
                s = pltpu.roll(a, HEAD_DIM, 1)
                pieces.append(jnp.where(left, a, s))
                pieces.append(jnp.where(left, s, a))
            kv_ref[...] = jnp.concatenate(pieces, axis=1).astype(BF16)
        else:
            slab0 = (j - n_q - 1) * slabs_per_blk
            for s in range(slabs_per_blk):
                u_scr[s] = acc[:, s * V7X_LANES:(s + 1) * V7X_LANES]
            for s in range(slabs_per_blk):
                for step in range(CHUNK):
                    rows = u_scr[s, pl.ds(step, chunk_rows, stride=CHUNK), :]
                    u_ref[slab0 + s, :, step * V7X_LANES:(step + 1) * V7X_LANES] = rows.astype(BF16)

    assert N_WBUF <= n_pieces <= n_blocks - 1
    w_copy(0, 0).start()
    for cp in cast_in(0):
        cp.start()
    _modulated_norm(x_ref, g_ref[...] * (1.0 + mod_ref[0, 1:2, :]), mod_ref[0, 0:1, :], gain_scr, h_scr)

    prev = None
    for j in range(n_blocks):
        slot = j % N_WBUF
        if 1 <= j <= n_pieces:
            for cp in cast_out(j - 1):
                cp.start()
        if j + 1 < n_blocks:
            w_copy(j + 1, (j + 1) % N_WBUF).start()
        if j + 1 < n_pieces:
            for cp in cast_in(j + 1):
                cp.start()
        w_copy(j, slot).wait()
        if j < n_pieces:
            for cp in cast_in(j):
                cp.wait()
            if j >= N_WBUF:
                for cp in cast_out(j - N_WBUF):
                    cp.wait()
        acc = jnp.dot(h_scr[...], w_buf[slot], preferred_element_type=F32)
        if prev is not None:
            finish(j - 1, prev)
        if j < n_pieces:
            for w in range(2):
                cast_out_buf[w, slot] = cast_in_buf[w, slot].astype(BF16)
        prev = acc
    finish(n_blocks - 1, prev)
    for p in range(n_pieces - N_WBUF, n_pieces):
        for cp in cast_out(p):
            cp.wait()


CAST_PIECES = 8


def _in_proj(x2d, mod3, g1, w_in, cos_t, sin_t, f1, f2, seq, tm=512, tn=512):
    t, d = x2d.shape
    assert 2 * D_KV == tn and w_in.shape[1] == D_ATTN + tn + D_SSM
    slabs_per_blk = tn // V7X_LANES
    nb = seq // tm
    n_steps = t // tm
    cast_rows, cast_cols = f1.shape
    assert f2.shape == f1.shape and cast_rows % (n_steps * CAST_PIECES) == 0
    piece_rows = cast_rows // (n_steps * CAST_PIECES)
    est = (2 * tm * d * 4 + tm * d * 2 + N_WBUF * d * tn * 2 + 2 * tm * D_ATTN * 2
           + 2 * tm * 4 * D_KV * 2 + 2 * tm * D_SSM * 2 + 4 * tm * V7X_LANES * 4
           + slabs_per_blk * tm * V7X_LANES * 4 + 6 * tm * tn * 4
           + 2 * N_WBUF * piece_rows * cast_cols * (4 + 2))
    any_spec = pl.BlockSpec(memory_space=pl.ANY)
    return pl.pallas_call(
        functools.partial(_in_proj_kernel, tn=tn, n_pieces=CAST_PIECES),
        grid=(n_steps,),
        in_specs=[
            pl.BlockSpec((tm, d), lambda i: (i, 0)),
            pl.BlockSpec((1, N_MOD, d), lambda i: (i // nb, 0, 0)),
            pl.BlockSpec((1, d), lambda i: (0, 0)),
            any_spec,
            pl.BlockSpec((tm, V7X_LANES), lambda i: (i % nb, 0)),
            pl.BlockSpec((tm, V7X_LANES), lambda i: (i % nb, 0)),
            any_spec,
            any_spec,
        ],
        out_specs=[
            pl.BlockSpec((tm, D_ATTN), lambda i: (i, 0)),
            pl.BlockSpec((tm, 4 * D_KV), lambda i: (i, 0)),
            pl.BlockSpec((N_SLABS, tm // CHUNK, SLAB_COLS), lambda i: (0, i, 0)),
            any_spec,
            any_spec,
        ],
        out_shape=[
            jax.ShapeDtypeStruct((t, D_ATTN), BF16),
            jax.ShapeDtypeStruct((t, 4 * D_KV), BF16),
            jax.ShapeDtypeStruct((N_SLABS, t // CHUNK, SLAB_COLS), BF16),
            jax.ShapeDtypeStruct(f1.shape, BF16),
            jax.ShapeDtypeStruct(f2.shape, BF16),
        ],
        scratch_shapes=[
            pltpu.VMEM((tm, d), BF16),
            pltpu.VMEM((slabs_per_blk, tm, V7X_LANES), F32),
            pltpu.VMEM((2, V7X_SUBLANES, d), F32),
            pltpu.VMEM((N_WBUF, d, tn), BF16),
            pltpu.SemaphoreType.DMA((N_WBUF,)),
            pltpu.VMEM((2, N_WBUF, piece_rows, cast_cols), F32),
            pltpu.VMEM((2, N_WBUF, piece_rows, cast_cols), BF16),
            pltpu.SemaphoreType.DMA((2, 2, N_WBUF)),
        ],
        compiler_params=pltpu.CompilerParams(
            dimension_semantics=("arbitrary",),
            vmem_limit_bytes=_vmem_limit(est)),
        name="in_proj",
    )(x2d, mod3, g1, w_in, cos_t, sin_t, f1, f2)


def _attn_kernel(sink_ref, q_ref, kvc_ref, kvp_ref, g_ref, o_ref, o_scr, cap_scr):
    n = pl.program_id(1)
    blk = WINDOW
    pair_w = 2 * HEAD_DIM
    n_keys = 2 * blk

    n_pairs = Q_PER_KV // 2
    rows = n_pairs * blk

    @pl.when((pl.program_id(0) == 0) & (n == 0))
    def _():
        qi = lax.broadcasted_iota(jnp.int32, (blk, 2 * n_keys), 0)
        key = lax.broadcasted_iota(jnp.int32, (blk, 2 * n_keys), 1) % n_keys
        rel = qi + blk - key
        band = (rel >= 0) & (rel < WINDOW)
        cap_scr[0] = jnp.where(band & (key >= blk), jnp.inf, F32(-1e30))
        cap_scr[1] = jnp.where(band, jnp.inf, F32(-1e30))

    cap = cap_scr[jnp.minimum(n, 1)]
    pair_of_row = lax.broadcasted_iota(jnp.int32, (rows, 1), 0) // blk

    lane = lax.broadcasted_iota(jnp.int32, (n_keys, pair_w), 1)
    left = lane < HEAD_DIM
    left_o = lax.broadcasted_iota(jnp.int32, (rows, pair_w), 1) < HEAD_DIM
    zero = jnp.zeros((n_keys, pair_w), BF16)

    for h in range(N_KV_HEADS):
        kcol = h * pair_w
        vcol = N_KV_HEADS * pair_w + h * pair_w
        kd = jnp.concatenate([kvp_ref[:, kcol:kcol + pair_w], kvc_ref[:, kcol:kcol + pair_w]], axis=0)
        vd = jnp.concatenate([kvp_ref[:, vcol:vcol + pair_w], kvc_ref[:, vcol:vcol + pair_w]], axis=0)
        k_bd = jnp.concatenate([jnp.where(left, kd, zero), jnp.where(left, zero, kd)], axis=0)
        v_bd = jnp.concatenate([jnp.where(left, vd, zero), jnp.where(left, zero, vd)], axis=0)
        head0 = h * Q_PER_KV
        q4 = jnp.concatenate([q_ref[:, (head0 + 2 * p) * HEAD_DIM:(head0 + 2 * p + 2) * HEAD_DIM]
                              for p in range(n_pairs)], axis=0)
        sink_a = jnp.zeros((rows, 1), F32)
        sink_b = jnp.zeros((rows, 1), F32)
        for p in range(n_pairs):
            sink_a = jnp.where(pair_of_row == p, sink_ref[head0 + 2 * p], sink_a)
            sink_b = jnp.where(pair_of_row == p, sink_ref[head0 + 2 * p + 1], sink_b)
        s = lax.dot_general(q4, k_bd, (((1,), (1,)), ((), ())), preferred_element_type=F32)
        s = jnp.minimum(s.reshape(n_pairs, blk, 2 * n_keys), cap[None]).reshape(rows, 2 * n_keys)
        m_a = jnp.maximum(jnp.max(s[:, :n_keys], axis=-1, keepdims=True), sink_a)
        m_b = jnp.maximum(jnp.max(s[:, n_keys:], axis=-1, keepdims=True), sink_b)
        p_a = jnp.exp(s[:, :n_keys] - m_a)
        p_b = jnp.exp(s[:, n_keys:] - m_b)
        l_a = jnp.sum(p_a, axis=-1, keepdims=True) + jnp.exp(sink_a - m_a)
        l_b = jnp.sum(p_b, axis=-1, keepdims=True) + jnp.exp(sink_b - m_b)
        pp = jnp.concatenate([p_a, p_b], axis=1).astype(BF16)
        o4 = jnp.dot(pp, v_bd, preferred_element_type=F32) * jnp.where(left_o, 1.0 / l_a, 1.0 / l_b)
        for p in range(n_pairs):
            qcol = (head0 + 2 * p) * HEAD_DIM
            o_scr[:, qcol:qcol + pair_w] = o4[p * blk:(p + 1) * blk, :]

    o = o_scr[...]
    o_ref[...] = (o * _rms_scale(o, D_ATTN) * g_ref[...]).astype(BF16)


def _attention(q, kvd, sinks, g_attn, batch, seq):
    t = q.shape[0]
    nb = seq // WINDOW
    kvw = kvd.shape[1]
    return pl.pallas_call(
        _attn_kernel,
        grid=(batch, nb),
        in_specs=[
            pl.BlockSpec(memory_space=pltpu.SMEM),
            pl.BlockSpec((WINDOW, D_ATTN), lambda b, n: (b * nb + n, 0)),
            pl.BlockSpec((WINDOW, kvw), lambda b, n: (b * nb + n, 0)),
            pl.BlockSpec((WINDOW, kvw), lambda b, n: (b * nb + jnp.maximum(n - 1, 0), 0)),
            pl.BlockSpec((1, D_ATTN), lambda b, n: (0, 0)),
        ],
        out_specs=pl.BlockSpec((WINDOW, D_ATTN), lambda b, n: (b * nb + n, 0)),
        out_shape=jax.ShapeDtypeStruct((t, D_ATTN), BF16),
        scratch_shapes=[pltpu.VMEM((WINDOW, D_ATTN), F32), pltpu.VMEM((2, WINDOW, 4 * WINDOW), F32)],
        compiler_params=pltpu.CompilerParams(
            dimension_semantics=("arbitrary", "arbitrary"),
            vmem_limit_bytes=_vmem_limit(8 * 1024 * 1024)),
        name="attention",
    )(sinks, q, kvd, kvd, g_attn)


def _cmul(a_re, a_im, b_re, b_im):
    return a_re * b_re - a_im * b_im, a_re * b_im + a_im * b_re


def _ssm_kernel(u_ref, rev_ref, bpow_ref, cpow_ref, mu_ref, o_ref, z_scr, sp_scr, *, seg_chunks):
    n_k = SLAB_STATE // V7X_LANES
    rows = u_ref.shape[1]
    seqs = rows // seg_chunks
    shape = (seqs, V7X_LANES)

    pitch = z_scr.shape[1] // seqs

    for nb in range(2 * SLAB_STATE // TILE):
        zz = jnp.dot(u_ref[0], bpow_ref[0, :, nb * TILE:(nb + 1) * TILE], preferred_element_type=F32)
        for q in range(seqs):
            src = slice(q * seg_chunks, (q + 1) * seg_chunks)
            dst = slice(q * pitch, q * pitch + seg_chunks)
            z_scr[2 * nb, dst, :] = zz[src, :V7X_LANES]
            z_scr[2 * nb + 1, dst, :] = zz[src, V7X_LANES:]

    mu_re = [jnp.broadcast_to(mu_ref[0, 0:1, k * V7X_LANES:(k + 1) * V7X_LANES], shape) for k in range(n_k)]
    mu_im = [jnp.broadcast_to(mu_ref[0, 1:2, k * V7X_LANES:(k + 1) * V7X_LANES], shape) for k in range(n_k)]

    def rows_at(c):
        return pl.ds(c, seqs, stride=pitch)

    def scan_step(c, carry):
        new = []
        for k in range(n_k):
            s_re, s_im = carry[2 * k], carry[2 * k + 1]
            sp_scr[k, rows_at(c), :] = s_re
            sp_scr[n_k + k, rows_at(c), :] = s_im
            p_re, p_im = _cmul(mu_re[k], mu_im[k], s_re, s_im)
            new += [p_re + z_scr[k, rows_at(c), :], p_im + z_scr[n_k + k, rows_at(c), :]]
        return tuple(new)

    zeros = jnp.zeros(shape, F32)
    final = lax.fori_loop(0, seg_chunks, scan_step, (zeros,) * (2 * n_k), unroll=SCAN_UNROLL)

    first_seg = (lax.broadcasted_iota(jnp.int32, shape, 0) % N_SEG) == 0
    init = []
    for k in range(n_k):
        m_re, m_im = mu_re[k], mu_im[k]
        for _ in range(int(math.log2(seg_chunks))):
            m_re, m_im = _cmul(m_re, m_im, m_re, m_im)
        i_re, i_im = zeros, zeros
        for _ in range(N_SEG - 1):
            t_re, t_im = _cmul(m_re, m_im, i_re, i_im)
            i_re = jnp.where(first_seg, 0.0, pltpu.roll(t_re + final[2 * k], 1, 0))
            i_im = jnp.where(first_seg, 0.0, pltpu.roll(t_im + final[2 * k + 1], 1, 0))
        init += [i_re, i_im]

    def fix_step(c, carry):
        new = []
        for k in range(n_k):
            c_re, c_im = carry[2 * k], carry[2 * k + 1]
            sp_scr[k, rows_at(c), :] += c_re
            sp_scr[n_k + k, rows_at(c), :] += c_im
            new += list(_cmul(mu_re[k], mu_im[k], c_re, c_im))
        return tuple(new)

    lax.fori_loop(0, seg_chunks, fix_step, tuple(init), unroll=SCAN_UNROLL)

    sp = jnp.concatenate(
        [jnp.concatenate([sp_scr[k, q * pitch:q * pitch + seg_chunks, :] for q in range(seqs)], axis=0)
         for k in range(2 * n_k)], axis=1).astype(BF16)
    for jt in range(SLAB_COLS // TILE):
        k_len = (jt + 1) * TILE
        y = jnp.dot(u_ref[0, :, :k_len], rev_ref[0, SLAB_COLS - k_len:, :], preferred_element_type=F32)
        y = y + jnp.dot(sp, cpow_ref[0, :, jt * TILE:(jt + 1) * TILE], preferred_element_type=F32)
        y = 0.5 * y * (1.0 + lax.erf(y * (2.0 ** -0.5)))
        for q in range(STEPS_PER_TILE):
            step = jt * STEPS_PER_TILE + q
            o_ref[pl.ds(step, rows, stride=CHUNK), :] = y[:, q * V7X_LANES:(q + 1) * V7X_LANES]


def _ssm(u_c, rev, bpow, cpow, mu, bsz, seq):
    n_slabs, chunk_rows, _ = u_c.shape
    t = chunk_rows * CHUNK
    rows = chunk_rows // SSM_BATCH_SPLIT
    seg_chunks = seq // (CHUNK * N_SEG)
    assert bsz % SSM_BATCH_SPLIT == 0 and rows // seg_chunks == V7X_SUBLANES
    assert seg_chunks & (seg_chunks - 1) == 0
    assert (seg_chunks // V7X_SUBLANES) % 2 == 0
    scan_rows = V7X_SUBLANES * (seg_chunks + V7X_SUBLANES)
    kern = functools.partial(_ssm_kernel, seg_chunks=seg_chunks)
    est = (2 * rows * SLAB_COLS * 2 + 2 * SLAB_COLS * TILE * 2 + 4 * SLAB_COLS * 2 * SLAB_STATE * 2
           + 2 * rows * CHUNK * V7X_LANES * 4 + 2 * rows * 2 * SLAB_STATE * 4 + 8 * rows * TILE * 4
           + rows * 2 * SLAB_STATE * 2)
    return pl.pallas_call(
        kern,
        grid=(n_slabs, SSM_BATCH_SPLIT),
        in_specs=[
            pl.BlockSpec((1, rows, SLAB_COLS), lambda s, h: (s, h, 0)),
            pl.BlockSpec((1, SLAB_COLS, TILE), lambda s, h: (s, 0, 0)),
            pl.BlockSpec((1, SLAB_COLS, 2 * SLAB_STATE), lambda s, h: (s, 0, 0)),
            pl.BlockSpec((1, 2 * SLAB_STATE, SLAB_COLS), lambda s, h: (s, 0, 0)),
            pl.BlockSpec((1, 2, SLAB_STATE), lambda s, h: (s, 0, 0)),
        ],
        out_specs=pl.BlockSpec((rows * CHUNK, V7X_LANES), lambda s, h: (h, s)),
        out_shape=jax.ShapeDtypeStruct((t, n_slabs * V7X_LANES), F32),
        scratch_shapes=[pltpu.VMEM((2 * SLAB_STATE // V7X_LANES, scan_rows, V7X_LANES), F32),
                        pltpu.VMEM((2 * SLAB_STATE // V7X_LANES, scan_rows, V7X_LANES), F32)],
        compiler_params=pltpu.CompilerParams(
            dimension_semantics=("arbitrary", "arbitrary"),
            vmem_limit_bytes=_vmem_limit(est)),
        name="ssm",
    )(u_c, rev, bpow, cpow, mu)


def _zoh(lam_re, lam_im, log_step):
    step = jnp.exp(log_step)
    mag = jnp.exp(lam_re * step)
    bar_re = mag * jnp.cos(lam_im * step)
    bar_im = mag * jnp.sin(lam_im * step)
    num_re, num_im = bar_re - 1.0, bar_im
    inv_den = 1.0 / (lam_re * lam_re + lam_im * lam_im)
    coef_re = (num_re * lam_re + num_im * lam_im) * inv_den
    coef_im = (num_im * lam_re - num_re * lam_im) * inv_den
    return bar_re, bar_im, coef_re, coef_im


def _ssm_ops_kernel(lam_r_ref, bt_re_ref, bt_im_ref, ct_re_ref, ct_im_ref, d_ref,
                    rev_ref, bpow_ref, cpow_ref, mu_ref, km_scr):
    lanes = V7X_LANES
    bar_re, bar_im, coef_re, coef_im = _zoh(lam_r_ref[0, 0:1, :], lam_r_ref[0, 1:2, :], lam_r_ref[0, 2:3, :])
    bb_re, bb_im = _cmul(bt_re_ref[0], bt_im_ref[0], coef_re, coef_im)
    p_re, p_im = jnp.ones_like(bar_re), jnp.zeros_like(bar_re)
    for jp in range(CHUNK - 1, -1, -1):
        r0 = jp * lanes
        blk_re, blk_im = _cmul(bb_re, bb_im, p_re, p_im)
        bpow_ref[0, r0:r0 + lanes, :SLAB_STATE] = blk_re.astype(BF16)
        bpow_ref[0, r0:r0 + lanes, SLAB_STATE:] = blk_im.astype(BF16)
        p_re, p_im = _cmul(p_re, p_im, bar_re, bar_im)
    mu_ref[0, 0:1, :] = p_re
    mu_ref[0, 1:2, :] = p_im

    cbar_re = jnp.broadcast_to(bar_re, (lanes, SLAB_STATE)).T
    cbar_im = jnp.broadcast_to(bar_im, (lanes, SLAB_STATE)).T
    w_re, w_im = ct_re_ref[0], ct_im_ref[0]
    row_i = lax.broadcasted_iota(jnp.int32, (lanes, lanes), 0)
    col_i = lax.broadcasted_iota(jnp.int32, (lanes, lanes), 1)
    skip = jnp.where(row_i == col_i, d_ref[0], 0.0)
    bb_cat = jnp.concatenate([bb_re, -bb_im], axis=1).astype(BF16)
    for m in range(CHUNK):
        w_cat = jnp.concatenate([w_re, w_im], axis=0).astype(BF16)
        km = jnp.dot(bb_cat, w_cat, preferred_element_type=F32)
        km_scr[m] = km + skip if m == 0 else km
        w_re, w_im = _cmul(w_re, w_im, cbar_re, cbar_im)
        cpow_ref[0, :SLAB_STATE, m * lanes:(m + 1) * lanes] = w_re.astype(BF16)
        cpow_ref[0, SLAB_STATE:, m * lanes:(m + 1) * lanes] = (-w_im).astype(BF16)

    assert STEPS_PER_TILE == 2
    n_tiles = SLAB_COLS // TILE
    for i in range(n_tiles):
        d = n_tiles - 1 - i
        r0 = i * TILE
        below = km_scr[2 * d - 1] if d > 0 else jnp.zeros((lanes, lanes), F32)
        rev_ref[0, r0:r0 + lanes, :lanes] = km_scr[2 * d].astype(BF16)
        rev_ref[0, r0:r0 + lanes, lanes:] = km_scr[2 * d + 1].astype(BF16)
        rev_ref[0, r0 + lanes:r0 + TILE, :lanes] = below.astype(BF16)
        rev_ref[0, r0 + lanes:r0 + TILE, lanes:] = km_scr[2 * d].astype(BF16)


def _ssm_operators(lam_re, lam_im, log_step, b_re, b_im, c_re, c_im, d_skip):
    s, sg = N_SLABS, SLAB_GROUPS
    eye_g = jnp.eye(sg, dtype=F32)

    def per_state(a):
        return a.reshape(s, SLAB_STATE)

    step_gp = jnp.broadcast_to(log_step[:, None], lam_re.shape)
    lam_rows = jnp.stack([per_state(lam_re), per_state(lam_im), per_state(step_gp)], axis=1)

    def block_diag_bt(b):
        bt = b.reshape(s, sg, STATE, SSM_GROUP).transpose(0, 1, 3, 2)
        return (bt[:, :, :, None, :] * eye_g[None, :, None, :, None]).reshape(s, V7X_LANES, SLAB_STATE)

    def block_diag_ct(c):
        ct = c.reshape(s, sg, SSM_GROUP, STATE).transpose(0, 1, 3, 2)
        return (ct[:, :, :, None, :] * eye_g[None, :, None, :, None]).reshape(s, SLAB_STATE, V7X_LANES)

    d_rows = d_skip.reshape(s, 1, V7X_LANES)
    operands = (lam_rows, block_diag_bt(b_re), block_diag_bt(b_im),
                block_diag_ct(c_re), block_diag_ct(c_im), d_rows)
    return pl.pallas_call(
        _ssm_ops_kernel,
        grid=(s,),
        in_specs=[
            pl.BlockSpec((1, 3, SLAB_STATE), lambda i: (i, 0, 0)),
            pl.BlockSpec((1, V7X_LANES, SLAB_STATE), lambda i: (i, 0, 0)),
            pl.BlockSpec((1, V7X_LANES, SLAB_STATE), lambda i: (i, 0, 0)),
            pl.BlockSpec((1, SLAB_STATE, V7X_LANES), lambda i: (i, 0, 0)),
            pl.BlockSpec((1, SLAB_STATE, V7X_LANES), lambda i: (i, 0, 0)),
            pl.BlockSpec((1, 1, V7X_LANES), lambda i: (i, 0, 0)),
        ],
        out_specs=[
            pl.BlockSpec((1, SLAB_COLS, TILE), lambda i: (i, 0, 0)),
            pl.BlockSpec((1, SLAB_COLS, 2 * SLAB_STATE), lambda i: (i, 0, 0)),
            pl.BlockSpec((1, 2 * SLAB_STATE, SLAB_COLS), lambda i: (i, 0, 0)),
            pl.BlockSpec((1, 2, SLAB_STATE), lambda i: (i, 0, 0)),
        ],
        out_shape=[
            jax.ShapeDtypeStruct((s, SLAB_COLS, TILE), BF16),
            jax.ShapeDtypeStruct((s, SLAB_COLS, 2 * SLAB_STATE), BF16),
            jax.ShapeDtypeStruct((s, 2 * SLAB_STATE, SLAB_COLS), BF16),
            jax.ShapeDtypeStruct((s, 2, SLAB_STATE), F32),
        ],
        scratch_shapes=[pltpu.VMEM((CHUNK, V7X_LANES, V7X_LANES), F32)],
        compiler_params=pltpu.CompilerParams(
            dimension_semantics=("arbitrary",),
            vmem_limit_bytes=_vmem_limit(2 * (SLAB_COLS * TILE + 4 * SLAB_COLS * SLAB_STATE) * 2
                                         + 24 * SLAB_STATE * V7X_LANES * 4)),
        name="ssm_ops",
    )(*operands)


def _glu_kernel(y_ref, w_ref, b_ref, g_ref, o_ref):
    y = y_ref[...]
    z = jnp.dot(y.astype(BF16), w_ref[...], preferred_element_type=F32) + b_ref[...]
    out = y * jax.nn.sigmoid(z)
    o_ref[...] = (out * _rms_scale(out, D_SSM) * g_ref[...]).astype(BF16)


def _glu(y, w_glu, b_glu, g_ssm, tm=512):
    t, d = y.shape
    est = 2 * tm * d * 4 + 2 * tm * d * 2 + 2 * d * d * 2 + 4 * tm * d * 4
    return pl.pallas_call(
        _glu_kernel,
        grid=(t // tm,),
        in_specs=[
            pl.BlockSpec((tm, d), lambda i: (i, 0)),
            pl.BlockSpec((d, d), lambda i: (0, 0)),
            pl.BlockSpec((1, d), lambda i: (0, 0)),
            pl.BlockSpec((1, d), lambda i: (0, 0)),
        ],
        out_specs=pl.BlockSpec((tm, d), lambda i: (i, 0)),
        out_shape=jax.ShapeDtypeStruct((t, d), BF16),
        compiler_params=pltpu.CompilerParams(
            dimension_semantics=("arbitrary",),
            vmem_limit_bytes=_vmem_limit(est)),
        name="glu",
    )(y, w_glu, b_glu, g_ssm)


def _out_proj_kernel(a_ref, s_ref, wa_ref, ws_ref, x_ref, mod_ref, o_ref):
    acc = jnp.dot(a_ref[...], wa_ref[...], preferred_element_type=F32)
    acc = acc + jnp.dot(s_ref[...], ws_ref[...], preferred_element_type=F32)
    o_ref[...] = x_ref[...] + mod_ref[0, 2:3, :] * acc


def _out_proj(attn_n, ssm_n, w_out, x2d, mod3, seq, tm=512, tn=1024):
    t, d = x2d.shape
    nb = seq // tm
    ka = attn_n.shape[1]
    est = 2 * 2 * tm * ka * 2 + 2 * 2 * ka * tn * 2 + 4 * tm * tn * 4 + 2 * tm * tn * 4
    return pl.pallas_call(
        _out_proj_kernel,
        grid=(t // tm, d // tn),
        in_specs=[
            pl.BlockSpec((tm, ka), lambda i, j: (i, 0)),
            pl.BlockSpec((tm, ka), lambda i, j: (i, 0)),
            pl.BlockSpec((ka, tn), lambda i, j: (0, j)),
            pl.BlockSpec((ka, tn), lambda i, j: (1, j)),
            pl.BlockSpec((tm, tn), lambda i, j: (i, j)),
            pl.BlockSpec((1, N_MOD, tn), lambda i, j: (i // nb, 0, j)),
        ],
        out_specs=pl.BlockSpec((tm, tn), lambda i, j: (i, j)),
        out_shape=jax.ShapeDtypeStruct((t, d), F32),
        compiler_params=pltpu.CompilerParams(
            dimension_semantics=("arbitrary", "arbitrary"),
            vmem_limit_bytes=_vmem_limit(est)),
        name="out_proj",
    )(attn_n, ssm_n, w_out, w_out, x2d, mod3)


def _ffn_kernel(x_ref, mod_ref, g2_ref, w1_hbm, w2_hbm, gf_ref, o_ref,
                h_scr, gain_scr, rs_scr, w1_buf, w2_buf, sem, *, tf):
    n_blocks = w1_hbm.shape[1] // tf

    def w_copies(k, slot):
        cols = pl.ds(pl.multiple_of(k * tf, tf), tf)
        return (pltpu.make_async_copy(w1_hbm.at[:, cols], w1_buf.at[slot], sem.at[0, slot]),
                pltpu.make_async_copy(w2_hbm.at[cols, :], w2_buf.at[slot], sem.at[1, slot]))

    def start(k, slot):
        for cp in w_copies(k, slot):
            cp.start()

    def wait(k, slot):
        for cp in w_copies(k, slot):
            cp.wait()

    start(0, 0)
    _modulated_norm(x_ref, g2_ref[...] * (1.0 + mod_ref[0, 4:5, :]), mod_ref[0, 3:4, :], gain_scr, h_scr)
    o_ref[...] = jnp.zeros_like(o_ref)

    def block(k, slot):
        @pl.when(k + 1 < n_blocks)
        def _():
            start(k + 1, (slot + 1) % N_WBUF)

        wait(k, slot)
        a = jnp.dot(h_scr[...], w1_buf[slot], preferred_element_type=F32)
        a = jnp.square(jnp.maximum(a, 0.0)).astype(BF16)
        for n0 in range(0, D_MODEL, tf):
            o_ref[:, n0:n0 + tf] += jnp.dot(a, w2_buf[slot, :, n0:n0 + tf], preferred_element_type=F32)

    def trip(kk, carry):
        for slot in range(N_WBUF):
            block(kk * N_WBUF + slot, slot)
        return carry

    lax.fori_loop(0, n_blocks // N_WBUF, trip, 0)

    gain_scr[0] = jnp.broadcast_to(mod_ref[0, 5:6, :], (V7X_SUBLANES, D_MODEL))
    gain_scr[1] = jnp.broadcast_to(gf_ref[...], (V7X_SUBLANES, D_MODEL))

    def scale_rows(r, carry):
        rows = pl.ds(pl.multiple_of(r * V7X_SUBLANES, V7X_SUBLANES), V7X_SUBLANES)
        x2 = x_ref[rows, :] + gain_scr[0] * o_ref[rows, :]
        rs_scr[rows, :] = jnp.broadcast_to(_rms_scale(x2, D_MODEL), (V7X_SUBLANES, V7X_LANES))
        return carry

    lax.fori_loop(0, o_ref.shape[0] // V7X_SUBLANES, scale_rows, 0, unroll=NORM_UNROLL)

    def rescale_rows(r, carry):
        rows = pl.ds(pl.multiple_of(r * V7X_SUBLANES, V7X_SUBLANES), V7X_SUBLANES)
        x2 = x_ref[rows, :] + gain_scr[0] * o_ref[rows, :]
        rs = jnp.concatenate([rs_scr[rows, :]] * (D_MODEL // V7X_LANES), axis=1)
        o_ref[rows, :] = x2 * rs * gain_scr[1]
        return carry

    lax.fori_loop(0, o_ref.shape[0] // V7X_SUBLANES, rescale_rows, 0, unroll=NORM_UNROLL)


def _ffn(x1, mod3, g2, w1, w2, gf, seq, tm=512, tf=512):
    t, d = x1.shape
    nb = seq // tm
    assert (w1.shape[1] // tf) % N_WBUF == 0 and w2.shape == (w1.shape[1], d)
    est = (2 * tm * d * 4 + 2 * tm * d * 4 + tm * d * 2 + N_WBUF * 2 * d * tf * 2
           + tm * V7X_LANES * 4 + 4 * tm * tf * 4)
    return pl.pallas_call(
        functools.partial(_ffn_kernel, tf=tf),
        grid=(t // tm,),
        in_specs=[
            pl.BlockSpec((tm, d), lambda i: (i, 0)),
            pl.BlockSpec((1, N_MOD, d), lambda i: (i // nb, 0, 0)),
            pl.BlockSpec((1, d), lambda i: (0, 0)),
            pl.BlockSpec(memory_space=pl.ANY),
            pl.BlockSpec(memory_space=pl.ANY),
            pl.BlockSpec((1, d), lambda i: (0, 0)),
        ],
        out_specs=pl.BlockSpec((tm, d), lambda i: (i, 0)),
        out_shape=jax.ShapeDtypeStruct((t, d), F32),
        scratch_shapes=[
            pltpu.VMEM((tm, d), BF16),
            pltpu.VMEM((2, V7X_SUBLANES, d), F32),
            pltpu.VMEM((tm, V7X_LANES), F32),
            pltpu.VMEM((N_WBUF, d, tf), BF16),
            pltpu.VMEM((N_WBUF, tf, d), BF16),
            pltpu.SemaphoreType.DMA((2, N_WBUF)),
        ],
        compiler_params=pltpu.CompilerParams(
            dimension_semantics=("arbitrary",),
            vmem_limit_bytes=_vmem_limit(est)),
        name="ffn",
    )(x1, mod3, g2, w1, w2, gf)


def _rope_tables(seq):
    half = HEAD_DIM // 2
    inv_freq = ROPE_THETA ** (-jnp.arange(half, dtype=F32) / half)
    ang = jnp.arange(seq, dtype=F32)[:, None] * inv_freq[None, :]
    cos = jnp.cos(ang)
    sin = jnp.sin(ang)
    reps = V7X_LANES // HEAD_DIM
    cos_t = jnp.tile(jnp.concatenate([cos, cos], axis=1), (1, reps))
    sin_t = jnp.tile(jnp.concatenate([-sin, sin], axis=1), (1, reps))
    return cos_t, sin_t


def kernel(x, c, w_ada, b_ada, norm1_g, w_in, sinks, ssm_lam_re, ssm_lam_im, ssm_log_step, ssm_b_re, ssm_b_im, ssm_c_re, ssm_c_im, ssm_d, w_glu, b_glu, attn_out_g, ssm_out_g, w_out, norm2_g, w_ff1, w_ff2, final_g):
    bsz, seq, d = x.shape
    t = bsz * seq
    x2d = x.reshape(t, d)

    c_pad = jnp.pad(c, ((0, V7X_SUBLANES - bsz), (0, 0)))
    mod = _adaln(c_pad, w_ada[0], b_ada[0].reshape(1, -1))
    mod3 = mod[:bsz].reshape(bsz, N_MOD, d)

    cos_t, sin_t = _rope_tables(seq)
    d_ff = w_ff1.shape[2]
    q, kvd, u_c, w1b, w2b = _in_proj(x2d, mod3, norm1_g[0].reshape(1, d), w_in[0].astype(BF16), cos_t, sin_t,
                                     w_ff1[0].reshape(d_ff, d), w_ff2[0], seq)

    attn_n = _attention(q, kvd, sinks[0], attn_out_g[0].reshape(1, -1), bsz, seq)

    rev, bpow, cpow, mu = _ssm_operators(ssm_lam_re[0], ssm_lam_im[0], ssm_log_step[0], ssm_b_re[0],
                                         ssm_b_im[0], ssm_c_re[0], ssm_c_im[0], ssm_d[0])
    y = _ssm(u_c, rev, bpow, cpow, mu, bsz, seq)
    ssm_n = _glu(y, w_glu[0].astype(BF16), b_glu[0].reshape(1, -1), ssm_out_g[0].reshape(1, -1))

    x1 = _out_proj(attn_n, ssm_n, w_out[0].astype(BF16), x2d, mod3, seq)
    out = _ffn(x1, mod3, norm2_g[0].reshape(1, d), w1b.reshape(d, d_ff), w2b, final_g.reshape(1, d), seq)
    return out.reshape(bsz, seq, d)
```

```python
import functools
import math

import jax
import jax.numpy as jnp
from jax import lax
from jax.experimental import pallas as pl
from jax.experimental.pallas import tpu as pltpu

D_MODEL = 4096
D_ATTN = 2048
D_SSM = 2048
HEAD_DIM = 64
N_Q_HEADS = 32
N_KV_HEADS = 4
Q_PER_KV = 8
D_KV = 256
WINDOW = 128
ROPE_THETA = 10000.0
SSM_GROUP = 16
N_SSM_GROUPS = 128
STATE = 64
D_FF = 4 * D_MODEL
N_MOD = 6
EPS = 1e-6

V7X_LANES = 128
V7X_SUBLANES = 8
V7X_VMEM_BYTES = 64 * 1024 * 1024

CHUNK = 16
N_SEG = 4
SSM_BATCH_SPLIT = 2
SLAB_GROUPS = V7X_LANES // SSM_GROUP
N_SLABS = N_SSM_GROUPS // SLAB_GROUPS
SLAB_COLS = CHUNK * V7X_LANES
SLAB_STATE = SLAB_GROUPS * STATE
TILE = 256
STEPS_PER_TILE = TILE // V7X_LANES
SCAN_UNROLL = 8

N_WBUF = 2

BF16 = jnp.bfloat16
F32 = jnp.float32


def _vmem_limit(nbytes):
    return int(min(nbytes + 8 * 1024 * 1024, V7X_VMEM_BYTES - 4 * 1024 * 1024))


def _rms_scale(xf, width):
    return lax.rsqrt(jnp.sum(xf * xf, axis=-1, keepdims=True) * (1.0 / width) + EPS)


NORM_ROWS = 16
NORM_UNROLL = 4


def _modulated_norm(x_ref, gain, shift, mod_scr, out_ref):
    width = x_ref.shape[1]
    mod_scr[0] = jnp.broadcast_to(gain, (V7X_SUBLANES, width))
    mod_scr[1] = jnp.broadcast_to(shift, (V7X_SUBLANES, width))
    reps = NORM_ROWS // V7X_SUBLANES

    def body(r, carry):
        rows = pl.ds(pl.multiple_of(r * NORM_ROWS, NORM_ROWS), NORM_ROWS)
        xf = x_ref[rows, :]
        g = jnp.concatenate([mod_scr[0]] * reps, axis=0)
        b = jnp.concatenate([mod_scr[1]] * reps, axis=0)
        out_ref[rows, :] = (xf * _rms_scale(xf, width) * g + b).astype(out_ref.dtype)
        return carry

    lax.fori_loop(0, x_ref.shape[0] // NORM_ROWS, body, 0, unroll=NORM_UNROLL)


def _adaln_kernel(c_ref, w_ref, b_ref, o_ref):
    c = c_ref[...]
    ca = (c * jax.nn.sigmoid(c)).astype(BF16)
    acc = jnp.dot(ca, w_ref[...].astype(BF16), preferred_element_type=F32)
    o_ref[...] = acc + b_ref[...]


def _adaln(c_pad, w_ada, b_ada, tn=1024):
    m, d = c_pad.shape
    n = w_ada.shape[1]
    return pl.pallas_call(
        _adaln_kernel,
        grid=(n // tn,),
        in_specs=[
            pl.BlockSpec((m, d), lambda j: (0, 0)),
            pl.BlockSpec((d, tn), lambda j: (0, j)),
            pl.BlockSpec((1, tn), lambda j: (0, j)),
        ],
        out_specs=pl.BlockSpec((m, tn), lambda j: (0, j)),
        out_shape=jax.ShapeDtypeStruct((m, n), F32),
        compiler_params=pltpu.CompilerParams(
            dimension_semantics=("arbitrary",),
            vmem_limit_bytes=_vmem_limit(2 * d * tn * 4 + d * tn * 2)),
        name="adaln",
    )(c_pad, w_ada, b_ada)


def _rope(acc, cos, sin_signed):
    width = acc.shape[1]
    lane = lax.broadcasted_iota(jnp.int32, acc.shape, 1)
    first_half = (lane % HEAD_DIM) < (HEAD_DIM // 2)
    partner = jnp.where(first_half,
                        pltpu.roll(acc, width - HEAD_DIM // 2, 1),
                        pltpu.roll(acc, HEAD_DIM // 2, 1))
    reps = width // cos.shape[1]
    cos_w = jnp.concatenate([cos] * reps, axis=1)
    sin_w = jnp.concatenate([sin_signed] * reps, axis=1)
    return acc * cos_w + partner * sin_w


def _in_proj_kernel(x_ref, mod_ref, g_ref, w_hbm, cos_ref, sin_ref, f1_hbm, f2_hbm,
                    q_ref, kv_ref, u_ref, f1b_hbm, f2b_hbm,
                    h_scr, u_scr, gain_scr, w_buf, sem, f1_in, f1_out, f2_in, f2_out, cast_sem, *, tn, n_pieces):
    n_q = D_ATTN // tn
    n_blocks = w_hbm.shape[1] // tn
    slabs_per_blk = tn // V7X_LANES
    chunk_rows = u_ref.shape[1]
    casts = ((f1_hbm, f1b_hbm, f1_in, f1_out), (f2_hbm, f2b_hbm, f2_in, f2_out))

    def w_copy(j, slot):
        return pltpu.make_async_copy(w_hbm.at[:, j * tn:(j + 1) * tn], w_buf.at[slot], sem.at[slot])

    def piece(p, rows):
        first = (pl.program_id(0) * n_pieces + p) * rows
        return pl.ds(pl.multiple_of(first, rows), rows)

    def cast_in(p):
        return [pltpu.make_async_copy(src.at[0, piece(p, land.shape[1]), :], land.at[p % N_WBUF],
                                      cast_sem.at[0, w, p % N_WBUF])
                for w, (src, _, land, _) in enumerate(casts)]

    def cast_out(p):
        return [pltpu.make_async_copy(stage.at[p % N_WBUF], dst.at[piece(p, stage.shape[1]), :],
                                      cast_sem.at[1, w, p % N_WBUF])
                for w, (_, dst, _, stage) in enumerate(casts)]

    def finish(j, acc):
        if j < n_q:
            q = _rope(acc, cos_ref[...], sin_ref[...]) * (HEAD_DIM ** -0.5)
            q_ref[:, j * tn:(j + 1) * tn] = q.astype(BF16)
        elif j == n_q:
            k = _rope(acc[:, :D_KV], cos_ref[...], sin_ref[...])
            kv = jnp.concatenate([k, acc[:, D_KV:]], axis=1)
            lane = lax.broadcasted_iota(jnp.int32, (kv.shape[0], V7X_LANES), 1)
            left = lane < HEAD_DIM
            pieces = []
            for c0 in range(0, 2 * D_KV, V7X_LANES):
                a = kv[:, c0:c0 + V7X_LANES]
                s = pltpu.roll(a, HEAD_DIM, 1)
                pieces.append(jnp.where(left, a, s))
                pieces.append(jnp.where(left, s, a))
            kv_ref[...] = jnp.concatenate(pieces, axis=1).astype(BF16)
        else:
            slab0 = (j - n_q - 1) * slabs_per_blk
            for s in range(slabs_per_blk):
                u_scr[s] = acc[:, s * V7X_LANES:(s + 1) * V7X_LANES]
            for s in range(slabs_per_blk):
                for step in range(CHUNK):
                    rows = u_scr[s, pl.ds(step, chunk_rows, stride=CHUNK), :]
                    u_ref[slab0 + s, :, step * V7X_LANES:(step + 1) * V7X_LANES] = rows.astype(BF16)

    assert N_WBUF <= n_pieces <= n_blocks - 1
    w_copy(0, 0).start()
    for cp in cast_in(0):
        cp.start()
    _modulated_norm(x_ref, g_ref[...] * (1.0 + mod_ref[0, 1:2, :]), mod_ref[0, 0:1, :], gain_scr, h_scr)

    prev = None
    for j in range(n_blocks):
        slot = j % N_WBUF
        if 1 <= j <= n_pieces:
            for cp in cast_out(j - 1):
                cp.start()
        if j + 1 < n_blocks:
            w_copy(j + 1, (j + 1) % N_WBUF).start()
        if j + 1 < n_pieces:
            for cp in cast_in(j + 1):
                cp.start()
        w_copy(j, slot).wait()
        if j < n_pieces:
            for cp in cast_in(j):
                cp.wait()
            if j >= N_WBUF:
                for cp in cast_out(j - N_WBUF):
                    cp.wait()
        acc = jnp.dot(h_scr[...], w_buf[slot], preferred_element_type=F32)
        if prev is not None:
            finish(j - 1, prev)
        if j < n_pieces:
            for _, _, land, stage in casts:
                stage[slot] = land[slot].astype(BF16)
        prev = acc
    finish(n_blocks - 1, prev)
    for p in range(n_pieces - N_WBUF, n_pieces):
        for cp in cast_out(p):
            cp.wait()


CAST_PIECES = 8


def _in_proj(x2d, mod3, g1, w_in, cos_t, sin_t, f1, f2, seq, tm=512, tn=512):
    t, d = x2d.shape
    assert 2 * D_KV == tn and w_in.shape[1] == D_ATTN + tn + D_SSM
    slabs_per_blk = tn // V7X_LANES
    nb = seq // tm
    n_steps = t // tm
    cast_scratch = []
    cast_bytes = 0
    for f in (f1, f2):
        _, rows, cols = f.shape
        assert rows % (n_steps * CAST_PIECES) == 0
        piece_rows = rows // (n_steps * CAST_PIECES)
        assert piece_rows % (2 * V7X_SUBLANES) == 0
        cast_scratch += [pltpu.VMEM((N_WBUF, piece_rows, cols), F32), pltpu.VMEM((N_WBUF, piece_rows, cols), BF16)]
        cast_bytes += N_WBUF * piece_rows * cols * (4 + 2)
    est = (2 * tm * d * 4 + tm * d * 2 + N_WBUF * d * tn * 2 + 2 * tm * D_ATTN * 2
           + 2 * tm * 4 * D_KV * 2 + 2 * tm * D_SSM * 2 + 4 * tm * V7X_LANES * 4
           + slabs_per_blk * tm * V7X_LANES * 4 + 6 * tm * tn * 4 + cast_bytes)
    any_spec = pl.BlockSpec(memory_space=pl.ANY)
    return pl.pallas_call(
        functools.partial(_in_proj_kernel, tn=tn, n_pieces=CAST_PIECES),
        grid=(n_steps,),
        in_specs=[
            pl.BlockSpec((tm, d), lambda i: (i, 0)),
            pl.BlockSpec((1, N_MOD, d), lambda i: (i // nb, 0, 0)),
            pl.BlockSpec((1, d), lambda i: (0, 0)),
            any_spec,
            pl.BlockSpec((tm, V7X_LANES), lambda i: (i % nb, 0)),
            pl.BlockSpec((tm, V7X_LANES), lambda i: (i % nb, 0)),
            any_spec,
            any_spec,
        ],
        out_specs=[
            pl.BlockSpec((tm, D_ATTN), lambda i: (i, 0)),
            pl.BlockSpec((tm, 4 * D_KV), lambda i: (i, 0)),
            pl.BlockSpec((N_SLABS, tm // CHUNK, SLAB_COLS), lambda i: (0, i, 0)),
            any_spec,
            any_spec,
        ],
        out_shape=[
            jax.ShapeDtypeStruct((t, D_ATTN), BF16),
            jax.ShapeDtypeStruct((t, 4 * D_KV), BF16),
            jax.ShapeDtypeStruct((N_SLABS, t // CHUNK, SLAB_COLS), BF16),
            jax.ShapeDtypeStruct(f1.shape[1:], BF16),
            jax.ShapeDtypeStruct(f2.shape[1:], BF16),
        ],
        scratch_shapes=[
            pltpu.VMEM((tm, d), BF16),
            pltpu.VMEM((slabs_per_blk, tm, V7X_LANES), F32),
            pltpu.VMEM((2, V7X_SUBLANES, d), F32),
            pltpu.VMEM((N_WBUF, d, tn), BF16),
            pltpu.SemaphoreType.DMA((N_WBUF,)),
            *cast_scratch,
            pltpu.SemaphoreType.DMA((2, 2, N_WBUF)),
        ],
        compiler_params=pltpu.CompilerParams(
            dimension_semantics=("arbitrary",),
            vmem_limit_bytes=_vmem_limit(est)),
        name="in_proj",
    )(x2d, mod3, g1, w_in, cos_t, sin_t, f1, f2)


def _attn_kernel(sink_ref, q_ref, kvc_ref, kvp_ref, g_ref, o_ref, o_scr, cap_scr):
    n = pl.program_id(1)
    blk = WINDOW
    pair_w = 2 * HEAD_DIM
    n_keys = 2 * blk

    n_pairs = Q_PER_KV // 2
    rows = n_pairs * blk

    @pl.when((pl.program_id(0) == 0) & (n == 0))
    def _():
        qi = lax.broadcasted_iota(jnp.int32, (blk, 2 * n_keys), 0)
        key = lax.broadcasted_iota(jnp.int32, (blk, 2 * n_keys), 1) % n_keys
        rel = qi + blk - key
        band = (rel >= 0) & (rel < WINDOW)
        cap_scr[0] = jnp.where(band & (key >= blk), jnp.inf, F32(-1e30))
        cap_scr[1] = jnp.where(band, jnp.inf, F32(-1e30))

    cap = cap_scr[jnp.minimum(n, 1)]
    pair_of_row = lax.broadcasted_iota(jnp.int32, (rows, 1), 0) // blk

    lane = lax.broadcasted_iota(jnp.int32, (n_keys, pair_w), 1)
    left = lane < HEAD_DIM
    left_o = lax.broadcasted_iota(jnp.int32, (rows, pair_w), 1) < HEAD_DIM
    zero = jnp.zeros((n_keys, pair_w), BF16)

    for h in range(N_KV_HEADS):
        kcol = h * pair_w
        vcol = N_KV_HEADS * pair_w + h * pair_w
        kd = jnp.concatenate([kvp_ref[:, kcol:kcol + pair_w], kvc_ref[:, kcol:kcol + pair_w]], axis=0)
        vd = jnp.concatenate([kvp_ref[:, vcol:vcol + pair_w], kvc_ref[:, vcol:vcol + pair_w]], axis=0)
        k_bd = jnp.concatenate([jnp.where(left, kd, zero), jnp.where(left, zero, kd)], axis=0)
        v_bd = jnp.concatenate([jnp.where(left, vd, zero), jnp.where(left, zero, vd)], axis=0)
        head0 = h * Q_PER_KV
        q4 = jnp.concatenate([q_ref[:, (head0 + 2 * p) * HEAD_DIM:(head0 + 2 * p + 2) * HEAD_DIM]
                              for p in range(n_pairs)], axis=0)
        sink_a = jnp.zeros((rows, 1), F32)
        sink_b = jnp.zeros((rows, 1), F32)
        for p in range(n_pairs):
            sink_a = jnp.where(pair_of_row == p, sink_ref[head0 + 2 * p], sink_a)
            sink_b = jnp.where(pair_of_row == p, sink_ref[head0 + 2 * p + 1], sink_b)
        s = lax.dot_general(q4, k_bd, (((1,), (1,)), ((), ())), preferred_element_type=F32)
        s = jnp.minimum(s.reshape(n_pairs, blk, 2 * n_keys), cap[None]).reshape(rows, 2 * n_keys)
        m_a = jnp.maximum(jnp.max(s[:, :n_keys], axis=-1, keepdims=True), sink_a)
        m_b = jnp.maximum(jnp.max(s[:, n_keys:], axis=-1, keepdims=True), sink_b)
        p_a = jnp.exp(s[:, :n_keys] - m_a)
        p_b = jnp.exp(s[:, n_keys:] - m_b)
        l_a = jnp.sum(p_a, axis=-1, keepdims=True) + jnp.exp(sink_a - m_a)
        l_b = jnp.sum(p_b, axis=-1, keepdims=True) + jnp.exp(sink_b - m_b)
        pp = jnp.concatenate([p_a, p_b], axis=1).astype(BF16)
        o4 = jnp.dot(pp, v_bd, preferred_element_type=F32) * jnp.where(left_o, 1.0 / l_a, 1.0 / l_b)
        for p in range(n_pairs):
            qcol = (head0 + 2 * p) * HEAD_DIM
            o_scr[:, qcol:qcol + pair_w] = o4[p * blk:(p + 1) * blk, :]

    o = o_scr[...]
    o_ref[...] = (o * _rms_scale(o, D_ATTN) * g_ref[...]).astype(BF16)


def _attention(q, kvd, sinks, g_attn, batch, seq):
    t = q.shape[0]
    nb = seq // WINDOW
    kvw = kvd.shape[1]
    return pl.pallas_call(
        _attn_kernel,
        grid=(batch, nb),
        in_specs=[
            pl.BlockSpec(memory_space=pltpu.SMEM),
            pl.BlockSpec((WINDOW, D_ATTN), lambda b, n: (b * nb + n, 0)),
            pl.BlockSpec((WINDOW, kvw), lambda b, n: (b * nb + n, 0)),
            pl.BlockSpec((WINDOW, kvw), lambda b, n: (b * nb + jnp.maximum(n - 1, 0), 0)),
            pl.BlockSpec((1, D_ATTN), lambda b, n: (0, 0)),
        ],
        out_specs=pl.BlockSpec((WINDOW, D_ATTN), lambda b, n: (b * nb + n, 0)),
        out_shape=jax.ShapeDtypeStruct((t, D_ATTN), BF16),
        scratch_shapes=[pltpu.VMEM((WINDOW, D_ATTN), F32), pltpu.VMEM((2, WINDOW, 4 * WINDOW), F32)],
        compiler_params=pltpu.CompilerParams(
            dimension_semantics=("arbitrary", "arbitrary"),
            vmem_limit_bytes=_vmem_limit(8 * 1024 * 1024)),
        name="attention",
    )(sinks, q, kvd, kvd, g_attn)


def _cmul(a_re, a_im, b_re, b_im):
    return a_re * b_re - a_im * b_im, a_re * b_im + a_im * b_re


def _ssm_kernel(u_ref, rev_ref, bpow_ref, cpow_ref, mu_ref, o_ref, z_scr, sp_scr, *, seg_chunks):
    n_k = SLAB_STATE // V7X_LANES
    rows = u_ref.shape[1]
    seqs = rows // seg_chunks
    shape = (seqs, V7X_LANES)

    pitch = z_scr.shape[1] // seqs

    for nb in range(2 * SLAB_STATE // TILE):
        zz = jnp.dot(u_ref[0], bpow_ref[0, :, nb * TILE:(nb + 1) * TILE], preferred_element_type=F32)
        for q in range(seqs):
            src = slice(q * seg_chunks, (q + 1) * seg_chunks)
            dst = slice(q * pitch, q * pitch + seg_chunks)
            z_scr[2 * nb, dst, :] = zz[src, :V7X_LANES]
            z_scr[2 * nb + 1, dst, :] = zz[src, V7X_LANES:]

    mu_re = [jnp.broadcast_to(mu_ref[0, 0:1, k * V7X_LANES:(k + 1) * V7X_LANES], shape) for k in range(n_k)]
    mu_im = [jnp.broadcast_to(mu_ref[0, 1:2, k * V7X_LANES:(k + 1) * V7X_LANES], shape) for k in range(n_k)]

    def rows_at(c):
        return pl.ds(c, seqs, stride=pitch)

    def scan_step(c, carry):
        new = []
        for k in range(n_k):
            s_re, s_im = carry[2 * k], carry[2 * k + 1]
            sp_scr[k, rows_at(c), :] = s_re
            sp_scr[n_k + k, rows_at(c), :] = s_im
            p_re, p_im = _cmul(mu_re[k], mu_im[k], s_re, s_im)
            new += [p_re + z_scr[k, rows_at(c), :], p_im + z_scr[n_k + k, rows_at(c), :]]
        return tuple(new)

    zeros = jnp.zeros(shape, F32)
    final = lax.fori_loop(0, seg_chunks, scan_step, (zeros,) * (2 * n_k), unroll=SCAN_UNROLL)

    first_seg = (lax.broadcasted_iota(jnp.int32, shape, 0) % N_SEG) == 0
    init = []
    for k in range(n_k):
        m_re, m_im = mu_re[k], mu_im[k]
        for _ in range(int(math.log2(seg_chunks))):
            m_re, m_im = _cmul(m_re, m_im, m_re, m_im)
        i_re, i_im = zeros, zeros
        for _ in range(N_SEG - 1):
            t_re, t_im = _cmul(m_re, m_im, i_re, i_im)
            i_re = jnp.where(first_seg, 0.0, pltpu.roll(t_re + final[2 * k], 1, 0))
            i_im = jnp.where(first_seg, 0.0, pltpu.roll(t_im + final[2 * k + 1], 1, 0))
        init += [i_re, i_im]

    def fix_step(c, carry):
        new = []
        for k in range(n_k):
            c_re, c_im = carry[2 * k], carry[2 * k + 1]
            sp_scr[k, rows_at(c), :] += c_re
            sp_scr[n_k + k, rows_at(c), :] += c_im
            new += list(_cmul(mu_re[k], mu_im[k], c_re, c_im))
        return tuple(new)

    lax.fori_loop(0, seg_chunks, fix_step, tuple(init), unroll=SCAN_UNROLL)

    sp = jnp.concatenate(
        [jnp.concatenate([sp_scr[k, q * pitch:q * pitch + seg_chunks, :] for q in range(seqs)], axis=0)
         for k in range(2 * n_k)], axis=1).astype(BF16)
    for jt in range(SLAB_COLS // TILE):
        k_len = (jt + 1) * TILE
        y = jnp.dot(u_ref[0, :, :k_len], rev_ref[0, SLAB_COLS - k_len:, :], preferred_element_type=F32)
        y = y + jnp.dot(sp, cpow_ref[0, :, jt * TILE:(jt + 1) * TILE], preferred_element_type=F32)
        y = 0.5 * y * (1.0 + lax.erf(y * (2.0 ** -0.5)))
        for q in range(STEPS_PER_TILE):
            step = jt * STEPS_PER_TILE + q
            o_ref[pl.ds(step, rows, stride=CHUNK), :] = y[:, q * V7X_LANES:(q + 1) * V7X_LANES]


def _ssm(u_c, rev, bpow, cpow, mu, bsz, seq):
    n_slabs, chunk_rows, _ = u_c.shape
    t = chunk_rows * CHUNK
    rows = chunk_rows // SSM_BATCH_SPLIT
    seg_chunks = seq // (CHUNK * N_SEG)
    assert bsz % SSM_BATCH_SPLIT == 0 and rows // seg_chunks == V7X_SUBLANES
    assert seg_chunks & (seg_chunks - 1) == 0
    assert (seg_chunks // V7X_SUBLANES) % 2 == 0
    scan_rows = V7X_SUBLANES * (seg_chunks + V7X_SUBLANES)
    kern = functools.partial(_ssm_kernel, seg_chunks=seg_chunks)
    est = (2 * rows * SLAB_COLS * 2 + 2 * SLAB_COLS * TILE * 2 + 4 * SLAB_COLS * 2 * SLAB_STATE * 2
           + 2 * rows * CHUNK * V7X_LANES * 4 + 2 * rows * 2 * SLAB_STATE * 4 + 8 * rows * TILE * 4
           + rows * 2 * SLAB_STATE * 2)
    return pl.pallas_call(
        kern,
        grid=(n_slabs, SSM_BATCH_SPLIT),
        in_specs=[
            pl.BlockSpec((1, rows, SLAB_COLS), lambda s, h: (s, h, 0)),
            pl.BlockSpec((1, SLAB_COLS, TILE), lambda s, h: (s, 0, 0)),
            pl.BlockSpec((1, SLAB_COLS, 2 * SLAB_STATE), lambda s, h: (s, 0, 0)),
            pl.BlockSpec((1, 2 * SLAB_STATE, SLAB_COLS), lambda s, h: (s, 0, 0)),
            pl.BlockSpec((1, 2, SLAB_STATE), lambda s, h: (s, 0, 0)),
        ],
        out_specs=pl.BlockSpec((rows * CHUNK, V7X_LANES), lambda s, h: (h, s)),
        out_shape=jax.ShapeDtypeStruct((t, n_slabs * V7X_LANES), F32),
        scratch_shapes=[pltpu.VMEM((2 * SLAB_STATE // V7X_LANES, scan_rows, V7X_LANES), F32),
                        pltpu.VMEM((2 * SLAB_STATE // V7X_LANES, scan_rows, V7X_LANES), F32)],
        compiler_params=pltpu.CompilerParams(
            dimension_semantics=("arbitrary", "arbitrary"),
            vmem_limit_bytes=_vmem_limit(est)),
        name="ssm",
    )(u_c, rev, bpow, cpow, mu)


def _zoh(lam_re, lam_im, log_step):
    step = jnp.exp(log_step)
    mag = jnp.exp(lam_re * step)
    bar_re = mag * jnp.cos(lam_im * step)
    bar_im = mag * jnp.sin(lam_im * step)
    num_re, num_im = bar_re - 1.0, bar_im
    inv_den = 1.0 / (lam_re * lam_re + lam_im * lam_im)
    coef_re = (num_re * lam_re + num_im * lam_im) * inv_den
    coef_im = (num_im * lam_re - num_re * lam_im) * inv_den
    return bar_re, bar_im, coef_re, coef_im


def _ssm_ops_kernel(lam_r_ref, bt_re_ref, bt_im_ref, ct_re_ref, ct_im_ref, d_ref,
                    rev_ref, bpow_ref, cpow_ref, mu_ref, km_scr):
    lanes = V7X_LANES
    bar_re, bar_im, coef_re, coef_im = _zoh(lam_r_ref[0, 0:1, :], lam_r_ref[0, 1:2, :], lam_r_ref[0, 2:3, :])
    bb_re, bb_im = _cmul(bt_re_ref[0], bt_im_ref[0], coef_re, coef_im)
    p_re, p_im = jnp.ones_like(bar_re), jnp.zeros_like(bar_re)
    for jp in range(CHUNK - 1, -1, -1):
        r0 = jp * lanes
        blk_re, blk_im = _cmul(bb_re, bb_im, p_re, p_im)
        bpow_ref[0, r0:r0 + lanes, :SLAB_STATE] = blk_re.astype(BF16)
        bpow_ref[0, r0:r0 + lanes, SLAB_STATE:] = blk_im.astype(BF16)
        p_re, p_im = _cmul(p_re, p_im, bar_re, bar_im)
    mu_ref[0, 0:1, :] = p_re
    mu_ref[0, 1:2, :] = p_im

    cbar_re = jnp.broadcast_to(bar_re, (lanes, SLAB_STATE)).T
    cbar_im = jnp.broadcast_to(bar_im, (lanes, SLAB_STATE)).T
    w_re, w_im = ct_re_ref[0], ct_im_ref[0]
    row_i = lax.broadcasted_iota(jnp.int32, (lanes, lanes), 0)
    col_i = lax.broadcasted_iota(jnp.int32, (lanes, lanes), 1)
    skip = jnp.where(row_i == col_i, d_ref[0], 0.0)
    bb_cat = jnp.concatenate([bb_re, -bb_im], axis=1).astype(BF16)
    for m in range(CHUNK):
        w_cat = jnp.concatenate([w_re, w_im], axis=0).astype(BF16)
        km = jnp.dot(bb_cat, w_cat, preferred_element_type=F32)
        km_scr[m] = km + skip if m == 0 else km
        w_re, w_im = _cmul(w_re, w_im, cbar_re, cbar_im)
        cpow_ref[0, :SLAB_STATE, m * lanes:(m + 1) * lanes] = w_re.astype(BF16)
        cpow_ref[0, SLAB_STATE:, m * lanes:(m + 1) * lanes] = (-w_im).astype(BF16)

    assert STEPS_PER_TILE == 2
    n_tiles = SLAB_COLS // TILE
    for i in range(n_tiles):
        d = n_tiles - 1 - i
        r0 = i * TILE
        below = km_scr[2 * d - 1] if d > 0 else jnp.zeros((lanes, lanes), F32)
        rev_ref[0, r0:r0 + lanes, :lanes] = km_scr[2 * d].astype(BF16)
        rev_ref[0, r0:r0 + lanes, lanes:] = km_scr[2 * d + 1].astype(BF16)
        rev_ref[0, r0 + lanes:r0 + TILE, :lanes] = below.astype(BF16)
        rev_ref[0, r0 + lanes:r0 + TILE, lanes:] = km_scr[2 * d].astype(BF16)


def _ssm_operators(lam_re, lam_im, log_step, b_re, b_im, c_re, c_im, d_skip):
    s, sg = N_SLABS, SLAB_GROUPS
    eye_g = jnp.eye(sg, dtype=F32)

    def per_state(a):
        return a.reshape(s, SLAB_STATE)

    step_gp = jnp.broadcast_to(log_step[:, None], lam_re.shape)
    lam_rows = jnp.stack([per_state(lam_re), per_state(lam_im), per_state(step_gp)], axis=1)

    def block_diag_bt(b):
        bt = b.reshape(s, sg, STATE, SSM_GROUP).transpose(0, 1, 3, 2)
        return (bt[:, :, :, None, :] * eye_g[None, :, None, :, None]).reshape(s, V7X_LANES, SLAB_STATE)

    def block_diag_ct(c):
        ct = c.reshape(s, sg, SSM_GROUP, STATE).transpose(0, 1, 3, 2)
        return (ct[:, :, :, None, :] * eye_g[None, :, None, :, None]).reshape(s, SLAB_STATE, V7X_LANES)

    d_rows = d_skip.reshape(s, 1, V7X_LANES)
    operands = (lam_rows, block_diag_bt(b_re), block_diag_bt(b_im),
                block_diag_ct(c_re), block_diag_ct(c_im), d_rows)
    return pl.pallas_call(
        _ssm_ops_kernel,
        grid=(s,),
        in_specs=[
            pl.BlockSpec((1, 3, SLAB_STATE), lambda i: (i, 0, 0)),
            pl.BlockSpec((1, V7X_LANES, SLAB_STATE), lambda i: (i, 0, 0)),
            pl.BlockSpec((1, V7X_LANES, SLAB_STATE), lambda i: (i, 0, 0)),
            pl.BlockSpec((1, SLAB_STATE, V7X_LANES), lambda i: (i, 0, 0)),
            pl.BlockSpec((1, SLAB_STATE, V7X_LANES), lambda i: (i, 0, 0)),
            pl.BlockSpec((1, 1, V7X_LANES), lambda i: (i, 0, 0)),
        ],
        out_specs=[
            pl.BlockSpec((1, SLAB_COLS, TILE), lambda i: (i, 0, 0)),
            pl.BlockSpec((1, SLAB_COLS, 2 * SLAB_STATE), lambda i: (i, 0, 0)),
            pl.BlockSpec((1, 2 * SLAB_STATE, SLAB_COLS), lambda i: (i, 0, 0)),
            pl.BlockSpec((1, 2, SLAB_STATE), lambda i: (i, 0, 0)),
        ],
        out_shape=[
            jax.ShapeDtypeStruct((s, SLAB_COLS, TILE), BF16),
            jax.ShapeDtypeStruct((s, SLAB_COLS, 2 * SLAB_STATE), BF16),
            jax.ShapeDtypeStruct((s, 2 * SLAB_STATE, SLAB_COLS), BF16),
            jax.ShapeDtypeStruct((s, 2, SLAB_STATE), F32),
        ],
        scratch_shapes=[pltpu.VMEM((CHUNK, V7X_LANES, V7X_LANES), F32)],
        compiler_params=pltpu.CompilerParams(
            dimension_semantics=("arbitrary",),
            vmem_limit_bytes=_vmem_limit(2 * (SLAB_COLS * TILE + 4 * SLAB_COLS * SLAB_STATE) * 2
                                         + 24 * SLAB_STATE * V7X_LANES * 4)),
        name="ssm_ops",
    )(*operands)


def _glu_kernel(y_ref, w_ref, b_ref, g_ref, o_ref):
    y = y_ref[...]
    z = jnp.dot(y.astype(BF16), w_ref[...], preferred_element_type=F32) + b_ref[...]
    out = y * jax.nn.sigmoid(z)
    o_ref[...] = (out * _rms_scale(out, D_SSM) * g_ref[...]).astype(BF16)


def _glu(y, w_glu, b_glu, g_ssm, tm=512):
    t, d = y.shape
    est = 2 * tm * d * 4 + 2 * tm * d * 2 + 2 * d * d * 2 + 4 * tm * d * 4
    return pl.pallas_call(
        _glu_kernel,
        grid=(t // tm,),
        in_specs=[
            pl.BlockSpec((tm, d), lambda i: (i, 0)),
            pl.BlockSpec((d, d), lambda i: (0, 0)),
            pl.BlockSpec((1, d), lambda i: (0, 0)),
            pl.BlockSpec((1, d), lambda i: (0, 0)),
        ],
        out_specs=pl.BlockSpec((tm, d), lambda i: (i, 0)),
        out_shape=jax.ShapeDtypeStruct((t, d), BF16),
        compiler_params=pltpu.CompilerParams(
            dimension_semantics=("arbitrary",),
            vmem_limit_bytes=_vmem_limit(est)),
        name="glu",
    )(y, w_glu, b_glu, g_ssm)


def _out_proj_kernel(a_ref, s_ref, wa_ref, ws_ref, x_ref, mod_ref, o_ref):
    acc = jnp.dot(a_ref[...], wa_ref[...], preferred_element_type=F32)
    acc = acc + jnp.dot(s_ref[...], ws_ref[...], preferred_element_type=F32)
    o_ref[...] = x_ref[...] + mod_ref[0, 2:3, :] * acc


def _out_proj(attn_n, ssm_n, w_out, x2d, mod3, seq, tm=512, tn=1024):
    t, d = x2d.shape
    nb = seq // tm
    ka = attn_n.shape[1]
    est = 2 * 2 * tm * ka * 2 + 2 * 2 * ka * tn * 2 + 4 * tm * tn * 4 + 2 * tm * tn * 4
    return pl.pallas_call(
        _out_proj_kernel,
        grid=(t // tm, d // tn),
        in_specs=[
            pl.BlockSpec((tm, ka), lambda i, j: (i, 0)),
            pl.BlockSpec((tm, ka), lambda i, j: (i, 0)),
            pl.BlockSpec((ka, tn), lambda i, j: (0, j)),
            pl.BlockSpec((ka, tn), lambda i, j: (1, j)),
            pl.BlockSpec((tm, tn), lambda i, j: (i, j)),
            pl.BlockSpec((1, N_MOD, tn), lambda i, j: (i // nb, 0, j)),
        ],
        out_specs=pl.BlockSpec((tm, tn), lambda i, j: (i, j)),
        out_shape=jax.ShapeDtypeStruct((t, d), F32),
        compiler_params=pltpu.CompilerParams(
            dimension_semantics=("arbitrary", "arbitrary"),
            vmem_limit_bytes=_vmem_limit(est)),
        name="out_proj",
    )(attn_n, ssm_n, w_out, w_out, x2d, mod3)


def _ffn_kernel(x_ref, mod_ref, g2_ref, w1_hbm, w2_hbm, gf_ref, o_ref,
                h_scr, gain_scr, rs_scr, w1_buf, w2_buf, sem, *, tf):
    n_blocks = w1_hbm.shape[1] // tf

    def w_copies(k, slot):
        cols = pl.ds(pl.multiple_of(k * tf, tf), tf)
        return (pltpu.make_async_copy(w1_hbm.at[:, cols], w1_buf.at[slot], sem.at[0, slot]),
                pltpu.make_async_copy(w2_hbm.at[cols, :], w2_buf.at[slot], sem.at[1, slot]))

    def start(k, slot):
        for cp in w_copies(k, slot):
            cp.start()

    def wait(k, slot):
        for cp in w_copies(k, slot):
            cp.wait()

    start(0, 0)
    _modulated_norm(x_ref, g2_ref[...] * (1.0 + mod_ref[0, 4:5, :]), mod_ref[0, 3:4, :], gain_scr, h_scr)
    o_ref[...] = jnp.zeros_like(o_ref)

    def block(k, slot):
        @pl.when(k + 1 < n_blocks)
        def _():
            start(k + 1, (slot + 1) % N_WBUF)

        wait(k, slot)
        a = jnp.dot(h_scr[...], w1_buf[slot], preferred_element_type=F32)
        a = jnp.square(jnp.maximum(a, 0.0)).astype(BF16)
        for n0 in range(0, D_MODEL, tf):
            o_ref[:, n0:n0 + tf] += jnp.dot(a, w2_buf[slot, :, n0:n0 + tf], preferred_element_type=F32)

    def trip(kk, carry):
        for slot in range(N_WBUF):
            block(kk * N_WBUF + slot, slot)
        return carry

    lax.fori_loop(0, n_blocks // N_WBUF, trip, 0)

    gain_scr[0] = jnp.broadcast_to(mod_ref[0, 5:6, :], (V7X_SUBLANES, D_MODEL))
    gain_scr[1] = jnp.broadcast_to(gf_ref[...], (V7X_SUBLANES, D_MODEL))

    def scale_rows(r, carry):
        rows = pl.ds(pl.multiple_of(r * V7X_SUBLANES, V7X_SUBLANES), V7X_SUBLANES)
        x2 = x_ref[rows, :] + gain_scr[0] * o_ref[rows, :]
        rs_scr[rows, :] = jnp.broadcast_to(_rms_scale(x2, D_MODEL), (V7X_SUBLANES, V7X_LANES))
        return carry

    lax.fori_loop(0, o_ref.shape[0] // V7X_SUBLANES, scale_rows, 0, unroll=NORM_UNROLL)

    def rescale_rows(r, carry):
        rows = pl.ds(pl.multiple_of(r * V7X_SUBLANES, V7X_SUBLANES), V7X_SUBLANES)
        x2 = x_ref[rows, :] + gain_scr[0] * o_ref[rows, :]
        rs = jnp.concatenate([rs_scr[rows, :]] * (D_MODEL // V7X_LANES), axis=1)
        o_ref[rows, :] = x2 * rs * gain_scr[1]
        return carry

    lax.fori_loop(0, o_ref.shape[0] // V7X_SUBLANES, rescale_rows, 0, unroll=NORM_UNROLL)


def _ffn(x1, mod3, g2, w1, w2, gf, seq, tm=512, tf=512):
    t, d = x1.shape
    nb = seq // tm
    assert (w1.shape[1] // tf) % N_WBUF == 0 and w2.shape == (w1.shape[1], d)
    est = (2 * tm * d * 4 + 2 * tm * d * 4 + tm * d * 2 + N_WBUF * 2 * d * tf * 2
           + tm * V7X_LANES * 4 + 4 * tm * tf * 4)
    return pl.pallas_call(
        functools.partial(_ffn_kernel, tf=tf),
        grid=(t // tm,),
        in_specs=[
            pl.BlockSpec((tm, d), lambda i: (i, 0)),
            pl.BlockSpec((1, N_MOD, d), lambda i: (i // nb, 0, 0)),
            pl.BlockSpec((1, d), lambda i: (0, 0)),
            pl.BlockSpec(memory_space=pl.ANY),
            pl.BlockSpec(memory_space=pl.ANY),
            pl.BlockSpec((1, d), lambda i: (0, 0)),
        ],
        out_specs=pl.BlockSpec((tm, d), lambda i: (i, 0)),
        out_shape=jax.ShapeDtypeStruct((t, d), F32),
        scratch_shapes=[
            pltpu.VMEM((tm, d), BF16),
            pltpu.VMEM((2, V7X_SUBLANES, d), F32),
            pltpu.VMEM((tm, V7X_LANES), F32),
            pltpu.VMEM((N_WBUF, d, tf), BF16),
            pltpu.VMEM((N_WBUF, tf, d), BF16),
            pltpu.SemaphoreType.DMA((2, N_WBUF)),
        ],
        compiler_params=pltpu.CompilerParams(
            dimension_semantics=("arbitrary",),
            vmem_limit_bytes=_vmem_limit(est)),
        name="ffn",
    )(x1, mod3, g2, w1, w2, gf)


def _rope_tables(seq):
    half = HEAD_DIM // 2
    inv_freq = ROPE_THETA ** (-jnp.arange(half, dtype=F32) / half)
    ang = jnp.arange(seq, dtype=F32)[:, None] * inv_freq[None, :]
    cos = jnp.cos(ang)
    sin = jnp.sin(ang)
    reps = V7X_LANES // HEAD_DIM
    cos_t = jnp.tile(jnp.concatenate([cos, cos], axis=1), (1, reps))
    sin_t = jnp.tile(jnp.concatenate([-sin, sin], axis=1), (1, reps))
    return cos_t, sin_t


def kernel(x, c, w_ada, b_ada, norm1_g, w_in, sinks, ssm_lam_re, ssm_lam_im, ssm_log_step, ssm_b_re, ssm_b_im, ssm_c_re, ssm_c_im, ssm_d, w_glu, b_glu, attn_out_g, ssm_out_g, w_out, norm2_g, w_ff1, w_ff2, final_g):
    bsz, seq, d = x.shape
    t = bsz * seq
    x2d = x.reshape(t, d)

    c_pad = jnp.pad(c, ((0, V7X_SUBLANES - bsz), (0, 0)))
    mod = _adaln(c_pad, w_ada[0], b_ada[0].reshape(1, -1))
    mod3 = mod[:bsz].reshape(bsz, N_MOD, d)

    cos_t, sin_t = _rope_tables(seq)
    q, kvd, u_c, w1b, w2b = _in_proj(x2d, mod3, norm1_g[0].reshape(1, d), w_in[0].astype(BF16), cos_t, sin_t,
                                     w_ff1, w_ff2, seq)

    attn_n = _attention(q, kvd, sinks[0], attn_out_g[0].reshape(1, -1), bsz, seq)

    rev, bpow, cpow, mu = _ssm_operators(ssm_lam_re[0], ssm_lam_im[0], ssm_log_step[0], ssm_b_re[0],
                                         ssm_b_im[0], ssm_c_re[0], ssm_c_im[0], ssm_d[0])
    y = _ssm(u_c, rev, bpow, cpow, mu, bsz, seq)
    ssm_n = _glu(y, w_glu[0].astype(BF16), b_glu[0].reshape(1, -1), ssm_out_g[0].reshape(1, -1))

    x1 = _out_proj(attn_n, ssm_n, w_out[0].astype(BF16), x2d, mod3, seq)
    out = _ffn(x1, mod3, norm2_g[0].reshape(1, d), w1b, w2b, final_g.reshape(1, d), seq)
    return out.reshape(bsz, seq, d)
```

```python
import functools
import math

import jax
import jax.numpy as jnp
from jax import lax
from jax.experimental import pallas as pl
from jax.experimental.pallas import tpu as pltpu

D_MODEL = 4096
D_ATTN = 2048
D_SSM = 2048
HEAD_DIM = 64
N_Q_HEADS = 32
N_KV_HEADS = 4
Q_PER_KV = 8
D_KV = 256
WINDOW = 128
ROPE_THETA = 10000.0
SSM_GROUP = 16
N_SSM_GROUPS = 128
STATE = 64
D_FF = 4 * D_MODEL
N_MOD = 6
EPS = 1e-6

V7X_LANES = 128
V7X_SUBLANES = 8
V7X_VMEM_BYTES = 64 * 1024 * 1024

CHUNK = 16
N_SEG = 4
SSM_BATCH_SPLIT = 2
SLAB_GROUPS = V7X_LANES // SSM_GROUP
N_SLABS = N_SSM_GROUPS // SLAB_GROUPS
SLAB_COLS = CHUNK * V7X_LANES
SLAB_STATE = SLAB_GROUPS * STATE
TILE = 256
STEPS_PER_TILE = TILE // V7X_LANES
SCAN_UNROLL = 8

N_WBUF = 2

BF16 = jnp.bfloat16
F32 = jnp.float32


def _vmem_limit(nbytes):
    return int(min(nbytes + 8 * 1024 * 1024, V7X_VMEM_BYTES - 4 * 1024 * 1024))


def _rms_scale(xf, width):
    return lax.rsqrt(jnp.sum(xf * xf, axis=-1, keepdims=True) * (1.0 / width) + EPS)


NORM_ROWS = 16
NORM_UNROLL = 4


def _modulated_norm(x_ref, gain, shift, mod_scr, out_ref):
    width = x_ref.shape[1]
    mod_scr[0] = jnp.broadcast_to(gain, (V7X_SUBLANES, width))
    mod_scr[1] = jnp.broadcast_to(shift, (V7X_SUBLANES, width))
    reps = NORM_ROWS // V7X_SUBLANES

    def body(r, carry):
        rows = pl.ds(pl.multiple_of(r * NORM_ROWS, NORM_ROWS), NORM_ROWS)
        xf = x_ref[rows, :]
        g = jnp.concatenate([mod_scr[0]] * reps, axis=0)
        b = jnp.concatenate([mod_scr[1]] * reps, axis=0)
        out_ref[rows, :] = (xf * _rms_scale(xf, width) * g + b).astype(out_ref.dtype)
        return carry

    lax.fori_loop(0, x_ref.shape[0] // NORM_ROWS, body, 0, unroll=NORM_UNROLL)


def _adaln_kernel(c_ref, w_ref, b_ref, o_ref):
    c = c_ref[...]
    ca = (c * jax.nn.sigmoid(c)).astype(BF16)
    acc = jnp.dot(ca, w_ref[...].astype(BF16), preferred_element_type=F32)
    o_ref[...] = acc + b_ref[...]


def _adaln(c_pad, w_ada, b_ada, tn=1024):
    m, d = c_pad.shape
    n = w_ada.shape[1]
    return pl.pallas_call(
        _adaln_kernel,
        grid=(n // tn,),
        in_specs=[
            pl.BlockSpec((m, d), lambda j: (0, 0)),
            pl.BlockSpec((d, tn), lambda j: (0, j)),
            pl.BlockSpec((1, tn), lambda j: (0, j)),
        ],
        out_specs=pl.BlockSpec((m, tn), lambda j: (0, j)),
        out_shape=jax.ShapeDtypeStruct((m, n), F32),
        compiler_params=pltpu.CompilerParams(
            dimension_semantics=("arbitrary",),
            vmem_limit_bytes=_vmem_limit(2 * d * tn * 4 + d * tn * 2)),
        name="adaln",
    )(c_pad, w_ada, b_ada)


def _rope(acc, cos, sin_signed):
    width = acc.shape[1]
    lane = lax.broadcasted_iota(jnp.int32, acc.shape, 1)
    first_half = (lane % HEAD_DIM) < (HEAD_DIM // 2)
    partner = jnp.where(first_half,
                        pltpu.roll(acc, width - HEAD_DIM // 2, 1),
                        pltpu.roll(acc, HEAD_DIM // 2, 1))
    reps = width // cos.shape[1]
    cos_w = jnp.concatenate([cos] * reps, axis=1)
    sin_w = jnp.concatenate([sin_signed] * reps, axis=1)
    return acc * cos_w + partner * sin_w


def _in_proj_kernel(x_ref, mod_ref, g_ref, w_hbm, cos_ref, sin_ref, f1_hbm, f2_hbm,
                    q_ref, kv_ref, u_ref, f1b_hbm, f2b_hbm,
                    h_scr, u_scr, gain_scr, w_buf, sem, f1_in, f1_out, f2_in, f2_out, cast_sem, *, tn, n_pieces):
    n_q = D_ATTN // tn
    n_blocks = w_hbm.shape[1] // tn
    slabs_per_blk = tn // V7X_LANES
    chunk_rows = u_ref.shape[1]
    casts = ((f1_hbm, f1b_hbm, f1_in, f1_out), (f2_hbm, f2b_hbm, f2_in, f2_out))

    def w_copy(j, slot):
        return pltpu.make_async_copy(w_hbm.at[:, j * tn:(j + 1) * tn], w_buf.at[slot], sem.at[slot])

    def piece(p, rows):
        first = (pl.program_id(0) * n_pieces + p) * rows
        return pl.ds(pl.multiple_of(first, rows), rows)

    def cast_in(p):
        return [pltpu.make_async_copy(src.at[0, piece(p, land.shape[1]), :], land.at[p % N_WBUF],
                                      cast_sem.at[0, w, p % N_WBUF])
                for w, (src, _, land, _) in enumerate(casts)]

    def cast_out(p):
        return [pltpu.make_async_copy(stage.at[p % N_WBUF], dst.at[piece(p, stage.shape[1]), :],
                                      cast_sem.at[1, w, p % N_WBUF])
                for w, (_, dst, _, stage) in enumerate(casts)]

    def finish(j, acc):
        if j < n_q:
            q = _rope(acc, cos_ref[...], sin_ref[...]) * (HEAD_DIM ** -0.5)
            q_ref[:, j * tn:(j + 1) * tn] = q.astype(BF16)
        elif j == n_q:
            k = _rope(acc[:, :D_KV], cos_ref[...], sin_ref[...])
            kv = jnp.concatenate([k, acc[:, D_KV:]], axis=1)
            lane = lax.broadcasted_iota(jnp.int32, (kv.shape[0], V7X_LANES), 1)
            left = lane < HEAD_DIM
            pieces = []
            for c0 in range(0, 2 * D_KV, V7X_LANES):
                a = kv[:, c0:c0 + V7X_LANES]
                s = pltpu.roll(a, HEAD_DIM, 1)
                pieces.append(jnp.where(left, a, s))
                pieces.append(jnp.where(left, s, a))
            kv_ref[...] = jnp.concatenate(pieces, axis=1).astype(BF16)
        else:
            slab0 = (j - n_q - 1) * slabs_per_blk
            for s in range(slabs_per_blk):
                u_scr[s] = acc[:, s * V7X_LANES:(s + 1) * V7X_LANES]
            for s in range(slabs_per_blk):
                for step in range(CHUNK):
                    rows = u_scr[s, pl.ds(step, chunk_rows, stride=CHUNK), :]
                    u_ref[slab0 + s, :, step * V7X_LANES:(step + 1) * V7X_LANES] = rows.astype(BF16)

    assert N_WBUF <= n_pieces <= n_blocks - 1
    w_copy(0, 0).start()
    for cp in cast_in(0):
        cp.start()
    _modulated_norm(x_ref, g_ref[...] * (1.0 + mod_ref[0, 1:2, :]), mod_ref[0, 0:1, :], gain_scr, h_scr)

    prev = None
    for j in range(n_blocks):
        slot = j % N_WBUF
        if 1 <= j <= n_pieces:
            for cp in cast_out(j - 1):
                cp.start()
        if j + 1 < n_blocks:
            w_copy(j + 1, (j + 1) % N_WBUF).start()
        if j + 1 < n_pieces:
            for cp in cast_in(j + 1):
                cp.start()
        w_copy(j, slot).wait()
        if j < n_pieces:
            for cp in cast_in(j):
                cp.wait()
            if j >= N_WBUF:
                for cp in cast_out(j - N_WBUF):
                    cp.wait()
        acc = jnp.dot(h_scr[...], w_buf[slot], preferred_element_type=F32)
        if prev is not None:
            finish(j - 1, prev)
        if j < n_pieces:
            for _, _, land, stage in casts:
                stage[slot] = land[slot].astype(BF16)
        prev = acc
    finish(n_blocks - 1, prev)
    for p in range(n_pieces - N_WBUF, n_pieces):
        for cp in cast_out(p):
            cp.wait()


CAST_PIECES = 4


def _in_proj(x2d, mod3, g1, w_in, cos_t, sin_t, f1, f2, seq, tm=512, tn=512):
    t, d = x2d.shape
    assert 2 * D_KV == tn and w_in.shape[1] == D_ATTN + tn + D_SSM
    slabs_per_blk = tn // V7X_LANES
    nb = seq // tm
    n_steps = t // tm
    cast_scratch = []
    cast_bytes = 0
    for f in (f1, f2):
        _, rows, cols = f.shape
        assert rows % (n_steps * CAST_PIECES) == 0
        piece_rows = rows // (n_steps * CAST_PIECES)
        assert piece_rows % (2 * V7X_SUBLANES) == 0
        cast_scratch += [pltpu.VMEM((N_WBUF, piece_rows, cols), F32), pltpu.VMEM((N_WBUF, piece_rows, cols), BF16)]
        cast_bytes += N_WBUF * piece_rows * cols * (4 + 2)
    est = (2 * tm * d * 4 + tm * d * 2 + N_WBUF * d * tn * 2 + 2 * tm * D_ATTN * 2
           + 2 * tm * 4 * D_KV * 2 + 2 * tm * D_SSM * 2 + 4 * tm * V7X_LANES * 4
           + slabs_per_blk * tm * V7X_LANES * 4 + 6 * tm * tn * 4 + cast_bytes)
    any_spec = pl.BlockSpec(memory_space=pl.ANY)
    return pl.pallas_call(
        functools.partial(_in_proj_kernel, tn=tn, n_pieces=CAST_PIECES),
        grid=(n_steps,),
        in_specs=[
            pl.BlockSpec((tm, d), lambda i: (i, 0)),
            pl.BlockSpec((1, N_MOD, d), lambda i: (i // nb, 0, 0)),
            pl.BlockSpec((1, d), lambda i: (0, 0)),
            any_spec,
            pl.BlockSpec((tm, V7X_LANES), lambda i: (i % nb, 0)),
            pl.BlockSpec((tm, V7X_LANES), lambda i: (i % nb, 0)),
            any_spec,
            any_spec,
        ],
        out_specs=[
            pl.BlockSpec((tm, D_ATTN), lambda i: (i, 0)),
            pl.BlockSpec((tm, 4 * D_KV), lambda i: (i, 0)),
            pl.BlockSpec((N_SLABS, tm // CHUNK, SLAB_COLS), lambda i: (0, i, 0)),
            any_spec,
            any_spec,
        ],
        out_shape=[
            jax.ShapeDtypeStruct((t, D_ATTN), BF16),
            jax.ShapeDtypeStruct((t, 4 * D_KV), BF16),
            jax.ShapeDtypeStruct((N_SLABS, t // CHUNK, SLAB_COLS), BF16),
            jax.ShapeDtypeStruct(f1.shape[1:], BF16),
            jax.ShapeDtypeStruct(f2.shape[1:], BF16),
        ],
        scratch_shapes=[
            pltpu.VMEM((tm, d), BF16),
            pltpu.VMEM((slabs_per_blk, tm, V7X_LANES), F32),
            pltpu.VMEM((2, V7X_SUBLANES, d), F32),
            pltpu.VMEM((N_WBUF, d, tn), BF16),
            pltpu.SemaphoreType.DMA((N_WBUF,)),
            *cast_scratch,
            pltpu.SemaphoreType.DMA((2, 2, N_WBUF)),
        ],
        compiler_params=pltpu.CompilerParams(
            dimension_semantics=("arbitrary",),
            vmem_limit_bytes=_vmem_limit(est)),
        name="in_proj",
    )(x2d, mod3, g1, w_in, cos_t, sin_t, f1, f2)


def _attn_kernel(sink_ref, q_ref, kvc_ref, kvp_ref, g_ref, o_ref, o_scr, cap_scr):
    n = pl.program_id(1)
    blk = WINDOW
    pair_w = 2 * HEAD_DIM
    n_keys = 2 * blk

    n_pairs = Q_PER_KV // 2
    rows = n_pairs * blk

    @pl.when((pl.program_id(0) == 0) & (n == 0))
    def _():
        qi = lax.broadcasted_iota(jnp.int32, (blk, 2 * n_keys), 0)
        key = lax.broadcasted_iota(jnp.int32, (blk, 2 * n_keys), 1) % n_keys
        rel = qi + blk - key
        band = (rel >= 0) & (rel < WINDOW)
        cap_scr[0] = jnp.where(band & (key >= blk), jnp.inf, F32(-1e30))
        cap_scr[1] = jnp.where(band, jnp.inf, F32(-1e30))

    cap = cap_scr[jnp.minimum(n, 1)]
    pair_of_row = lax.broadcasted_iota(jnp.int32, (rows, 1), 0) // blk

    lane = lax.broadcasted_iota(jnp.int32, (n_keys, pair_w), 1)
    left = lane < HEAD_DIM
    left_o = lax.broadcasted_iota(jnp.int32, (rows, pair_w), 1) < HEAD_DIM
    zero = jnp.zeros((n_keys, pair_w), BF16)

    for h in range(N_KV_HEADS):
        kcol = h * pair_w
        vcol = N_KV_HEADS * pair_w + h * pair_w
        kd = jnp.concatenate([kvp_ref[:, kcol:kcol + pair_w], kvc_ref[:, kcol:kcol + pair_w]], axis=0)
        vd = jnp.concatenate([kvp_ref[:, vcol:vcol + pair_w], kvc_ref[:, vcol:vcol + pair_w]], axis=0)
        k_bd = jnp.concatenate([jnp.where(left, kd, zero), jnp.where(left, zero, kd)], axis=0)
        v_bd = jnp.concatenate([jnp.where(left, vd, zero), jnp.where(left, zero, vd)], axis=0)
        head0 = h * Q_PER_KV
        q4 = jnp.concatenate([q_ref[:, (head0 + 2 * p) * HEAD_DIM:(head0 + 2 * p + 2) * HEAD_DIM]
                              for p in range(n_pairs)], axis=0)
        sink_a = jnp.zeros((rows, 1), F32)
        sink_b = jnp.zeros((rows, 1), F32)
        for p in range(n_pairs):
            sink_a = jnp.where(pair_of_row == p, sink_ref[head0 + 2 * p], sink_a)
            sink_b = jnp.where(pair_of_row == p, sink_ref[head0 + 2 * p + 1], sink_b)
        s = lax.dot_general(q4, k_bd, (((1,), (1,)), ((), ())), preferred_element_type=F32)
        s = jnp.minimum(s.reshape(n_pairs, blk, 2 * n_keys), cap[None]).reshape(rows, 2 * n_keys)
        m_a = jnp.maximum(jnp.max(s[:, :n_keys], axis=-1, keepdims=True), sink_a)
        m_b = jnp.maximum(jnp.max(s[:, n_keys:], axis=-1, keepdims=True), sink_b)
        p_a = jnp.exp(s[:, :n_keys] - m_a)
        p_b = jnp.exp(s[:, n_keys:] - m_b)
        l_a = jnp.sum(p_a, axis=-1, keepdims=True) + jnp.exp(sink_a - m_a)
        l_b = jnp.sum(p_b, axis=-1, keepdims=True) + jnp.exp(sink_b - m_b)
        pp = jnp.concatenate([p_a, p_b], axis=1).astype(BF16)
        o4 = jnp.dot(pp, v_bd, preferred_element_type=F32) * jnp.where(left_o, 1.0 / l_a, 1.0 / l_b)
        for p in range(n_pairs):
            qcol = (head0 + 2 * p) * HEAD_DIM
            o_scr[:, qcol:qcol + pair_w] = o4[p * blk:(p + 1) * blk, :]

    o = o_scr[...]
    o_ref[...] = (o * _rms_scale(o, D_ATTN) * g_ref[...]).astype(BF16)


def _attention(q, kvd, sinks, g_attn, batch, seq):
    t = q.shape[0]
    nb = seq // WINDOW
    kvw = kvd.shape[1]
    return pl.pallas_call(
        _attn_kernel,
        grid=(batch, nb),
        in_specs=[
            pl.BlockSpec(memory_space=pltpu.SMEM),
            pl.BlockSpec((WINDOW, D_ATTN), lambda b, n: (b * nb + n, 0)),
            pl.BlockSpec((WINDOW, kvw), lambda b, n: (b * nb + n, 0)),
            pl.BlockSpec((WINDOW, kvw), lambda b, n: (b * nb + jnp.maximum(n - 1, 0), 0)),
            pl.BlockSpec((1, D_ATTN), lambda b, n: (0, 0)),
        ],
        out_specs=pl.BlockSpec((WINDOW, D_ATTN), lambda b, n: (b * nb + n, 0)),
        out_shape=jax.ShapeDtypeStruct((t, D_ATTN), BF16),
        scratch_shapes=[pltpu.VMEM((WINDOW, D_ATTN), F32), pltpu.VMEM((2, WINDOW, 4 * WINDOW), F32)],
        compiler_params=pltpu.CompilerParams(
            dimension_semantics=("arbitrary", "arbitrary"),
            vmem_limit_bytes=_vmem_limit(8 * 1024 * 1024)),
        name="attention",
    )(sinks, q, kvd, kvd, g_attn)


def _cmul(a_re, a_im, b_re, b_im):
    return a_re * b_re - a_im * b_im, a_re * b_im + a_im * b_re


def _ssm_kernel(u_ref, rev_ref, bpow_ref, cpow_ref, mu_ref, o_ref, z_scr, sp_scr, *, seg_chunks):
    n_k = SLAB_STATE // V7X_LANES
    rows = u_ref.shape[1]
    seqs = rows // seg_chunks
    shape = (seqs, V7X_LANES)

    pitch = z_scr.shape[1] // seqs

    for nb in range(2 * SLAB_STATE // TILE):
        zz = jnp.dot(u_ref[0], bpow_ref[0, :, nb * TILE:(nb + 1) * TILE], preferred_element_type=F32)
        for q in range(seqs):
            src = slice(q * seg_chunks, (q + 1) * seg_chunks)
            dst = slice(q * pitch, q * pitch + seg_chunks)
            z_scr[2 * nb, dst, :] = zz[src, :V7X_LANES]
            z_scr[2 * nb + 1, dst, :] = zz[src, V7X_LANES:]

    mu_re = [jnp.broadcast_to(mu_ref[0, 0:1, k * V7X_LANES:(k + 1) * V7X_LANES], shape) for k in range(n_k)]
    mu_im = [jnp.broadcast_to(mu_ref[0, 1:2, k * V7X_LANES:(k + 1) * V7X_LANES], shape) for k in range(n_k)]

    def rows_at(c):
        return pl.ds(c, seqs, stride=pitch)

    def scan_step(c, carry):
        new = []
        for k in range(n_k):
            s_re, s_im = carry[2 * k], carry[2 * k + 1]
            sp_scr[k, rows_at(c), :] = s_re
            sp_scr[n_k + k, rows_at(c), :] = s_im
            p_re, p_im = _cmul(mu_re[k], mu_im[k], s_re, s_im)
            new += [p_re + z_scr[k, rows_at(c), :], p_im + z_scr[n_k + k, rows_at(c), :]]
        return tuple(new)

    zeros = jnp.zeros(shape, F32)
    final = lax.fori_loop(0, seg_chunks, scan_step, (zeros,) * (2 * n_k), unroll=SCAN_UNROLL)

    first_seg = (lax.broadcasted_iota(jnp.int32, shape, 0) % N_SEG) == 0
    init = []
    for k in range(n_k):
        m_re, m_im = mu_re[k], mu_im[k]
        for _ in range(int(math.log2(seg_chunks))):
            m_re, m_im = _cmul(m_re, m_im, m_re, m_im)
        i_re, i_im = zeros, zeros
        for _ in range(N_SEG - 1):
            t_re, t_im = _cmul(m_re, m_im, i_re, i_im)
            i_re = jnp.where(first_seg, 0.0, pltpu.roll(t_re + final[2 * k], 1, 0))
            i_im = jnp.where(first_seg, 0.0, pltpu.roll(t_im + final[2 * k + 1], 1, 0))
        init += [i_re, i_im]

    def fix_step(c, carry):
        new = []
        for k in range(n_k):
            c_re, c_im = carry[2 * k], carry[2 * k + 1]
            sp_scr[k, rows_at(c), :] += c_re
            sp_scr[n_k + k, rows_at(c), :] += c_im
            new += list(_cmul(mu_re[k], mu_im[k], c_re, c_im))
        return tuple(new)

    lax.fori_loop(0, seg_chunks, fix_step, tuple(init), unroll=SCAN_UNROLL)

    sp = jnp.concatenate(
        [jnp.concatenate([sp_scr[k, q * pitch:q * pitch + seg_chunks, :] for q in range(seqs)], axis=0)
         for k in range(2 * n_k)], axis=1).astype(BF16)
    for jt in range(SLAB_COLS // TILE):
        k_len = (jt + 1) * TILE
        y = jnp.dot(u_ref[0, :, :k_len], rev_ref[0, SLAB_COLS - k_len:, :], preferred_element_type=F32)
        y = y + jnp.dot(sp, cpow_ref[0, :, jt * TILE:(jt + 1) * TILE], preferred_element_type=F32)
        y = 0.5 * y * (1.0 + lax.erf(y * (2.0 ** -0.5)))
        for q in range(STEPS_PER_TILE):
            step = jt * STEPS_PER_TILE + q
            o_ref[pl.ds(step, rows, stride=CHUNK), :] = y[:, q * V7X_LANES:(q + 1) * V7X_LANES]


def _ssm(u_c, rev, bpow, cpow, mu, bsz, seq):
    n_slabs, chunk_rows, _ = u_c.shape
    t = chunk_rows * CHUNK
    rows = chunk_rows // SSM_BATCH_SPLIT
    seg_chunks = seq // (CHUNK * N_SEG)
    assert bsz % SSM_BATCH_SPLIT == 0 and rows // seg_chunks == V7X_SUBLANES
    assert seg_chunks & (seg_chunks - 1) == 0
    assert (seg_chunks // V7X_SUBLANES) % 2 == 0
    scan_rows = V7X_SUBLANES * (seg_chunks + V7X_SUBLANES)
    kern = functools.partial(_ssm_kernel, seg_chunks=seg_chunks)
    est = (2 * rows * SLAB_COLS * 2 + 2 * SLAB_COLS * TILE * 2 + 4 * SLAB_COLS * 2 * SLAB_STATE * 2
           + 2 * rows * CHUNK * V7X_LANES * 4 + 2 * rows * 2 * SLAB_STATE * 4 + 8 * rows * TILE * 4
           + rows * 2 * SLAB_STATE * 2)
    return pl.pallas_call(
        kern,
        grid=(n_slabs, SSM_BATCH_SPLIT),
        in_specs=[
            pl.BlockSpec((1, rows, SLAB_COLS), lambda s, h: (s, h, 0)),
            pl.BlockSpec((1, SLAB_COLS, TILE), lambda s, h: (s, 0, 0)),
            pl.BlockSpec((1, SLAB_COLS, 2 * SLAB_STATE), lambda s, h: (s, 0, 0)),
            pl.BlockSpec((1, 2 * SLAB_STATE, SLAB_COLS), lambda s, h: (s, 0, 0)),
            pl.BlockSpec((1, 2, SLAB_STATE), lambda s, h: (s, 0, 0)),
        ],
        out_specs=pl.BlockSpec((rows * CHUNK, V7X_LANES), lambda s, h: (h, s)),
        out_shape=jax.ShapeDtypeStruct((t, n_slabs * V7X_LANES), F32),
        scratch_shapes=[pltpu.VMEM((2 * SLAB_STATE // V7X_LANES, scan_rows, V7X_LANES), F32),
                        pltpu.VMEM((2 * SLAB_STATE // V7X_LANES, scan_rows, V7X_LANES), F32)],
        compiler_params=pltpu.CompilerParams(
            dimension_semantics=("arbitrary", "arbitrary"),
            vmem_limit_bytes=_vmem_limit(est)),
        name="ssm",
    )(u_c, rev, bpow, cpow, mu)


def _zoh(lam_re, lam_im, log_step):
    step = jnp.exp(log_step)
    mag = jnp.exp(lam_re * step)
    bar_re = mag * jnp.cos(lam_im * step)
    bar_im = mag * jnp.sin(lam_im * step)
    num_re, num_im = bar_re - 1.0, bar_im
    inv_den = 1.0 / (lam_re * lam_re + lam_im * lam_im)
    coef_re = (num_re * lam_re + num_im * lam_im) * inv_den
    coef_im = (num_im * lam_re - num_re * lam_im) * inv_den
    return bar_re, bar_im, coef_re, coef_im


def _ssm_ops_kernel(lam_r_ref, bt_re_ref, bt_im_ref, ct_re_ref, ct_im_ref, d_ref,
                    rev_ref, bpow_ref, cpow_ref, mu_ref, km_scr):
    lanes = V7X_LANES
    bar_re, bar_im, coef_re, coef_im = _zoh(lam_r_ref[0, 0:1, :], lam_r_ref[0, 1:2, :], lam_r_ref[0, 2:3, :])
    bb_re, bb_im = _cmul(bt_re_ref[0], bt_im_ref[0], coef_re, coef_im)
    p_re, p_im = jnp.ones_like(bar_re), jnp.zeros_like(bar_re)
    for jp in range(CHUNK - 1, -1, -1):
        r0 = jp * lanes
        blk_re, blk_im = _cmul(bb_re, bb_im, p_re, p_im)
        bpow_ref[0, r0:r0 + lanes, :SLAB_STATE] = blk_re.astype(BF16)
        bpow_ref[0, r0:r0 + lanes, SLAB_STATE:] = blk_im.astype(BF16)
        p_re, p_im = _cmul(p_re, p_im, bar_re, bar_im)
    mu_ref[0, 0:1, :] = p_re
    mu_ref[0, 1:2, :] = p_im

    cbar_re = jnp.broadcast_to(bar_re, (lanes, SLAB_STATE)).T
    cbar_im = jnp.broadcast_to(bar_im, (lanes, SLAB_STATE)).T
    w_re, w_im = ct_re_ref[0], ct_im_ref[0]
    row_i = lax.broadcasted_iota(jnp.int32, (lanes, lanes), 0)
    col_i = lax.broadcasted_iota(jnp.int32, (lanes, lanes), 1)
    skip = jnp.where(row_i == col_i, d_ref[0], 0.0)
    bb_cat = jnp.concatenate([bb_re, -bb_im], axis=1).astype(BF16)
    for m in range(CHUNK):
        w_cat = jnp.concatenate([w_re, w_im], axis=0).astype(BF16)
        km = jnp.dot(bb_cat, w_cat, preferred_element_type=F32)
        km_scr[m] = km + skip if m == 0 else km
        w_re, w_im = _cmul(w_re, w_im, cbar_re, cbar_im)
        cpow_ref[0, :SLAB_STATE, m * lanes:(m + 1) * lanes] = w_re.astype(BF16)
        cpow_ref[0, SLAB_STATE:, m * lanes:(m + 1) * lanes] = (-w_im).astype(BF16)

    assert STEPS_PER_TILE == 2
    n_tiles = SLAB_COLS // TILE
    for i in range(n_tiles):
        d = n_tiles - 1 - i
        r0 = i * TILE
        below = km_scr[2 * d - 1] if d > 0 else jnp.zeros((lanes, lanes), F32)
        rev_ref[0, r0:r0 + lanes, :lanes] = km_scr[2 * d].astype(BF16)
        rev_ref[0, r0:r0 + lanes, lanes:] = km_scr[2 * d + 1].astype(BF16)
        rev_ref[0, r0 + lanes:r0 + TILE, :lanes] = below.astype(BF16)
        rev_ref[0, r0 + lanes:r0 + TILE, lanes:] = km_scr[2 * d].astype(BF16)


def _ssm_operators(lam_re, lam_im, log_step, b_re, b_im, c_re, c_im, d_skip):
    s, sg = N_SLABS, SLAB_GROUPS
    eye_g = jnp.eye(sg, dtype=F32)

    def per_state(a):
        return a.reshape(s, SLAB_STATE)

    step_gp = jnp.broadcast_to(log_step[:, None], lam_re.shape)
    lam_rows = jnp.stack([per_state(lam_re), per_state(lam_im), per_state(step_gp)], axis=1)

    def block_diag_bt(b):
        bt = b.reshape(s, sg, STATE, SSM_GROUP).transpose(0, 1, 3, 2)
        return (bt[:, :, :, None, :] * eye_g[None, :, None, :, None]).reshape(s, V7X_LANES, SLAB_STATE)

    def block_diag_ct(c):
        ct = c.reshape(s, sg, SSM_GROUP, STATE).transpose(0, 1, 3, 2)
        return (ct[:, :, :, None, :] * eye_g[None, :, None, :, None]).reshape(s, SLAB_STATE, V7X_LANES)

    d_rows = d_skip.reshape(s, 1, V7X_LANES)
    operands = (lam_rows, block_diag_bt(b_re), block_diag_bt(b_im),
                block_diag_ct(c_re), block_diag_ct(c_im), d_rows)
    return pl.pallas_call(
        _ssm_ops_kernel,
        grid=(s,),
        in_specs=[
            pl.BlockSpec((1, 3, SLAB_STATE), lambda i: (i, 0, 0)),
            pl.BlockSpec((1, V7X_LANES, SLAB_STATE), lambda i: (i, 0, 0)),
            pl.BlockSpec((1, V7X_LANES, SLAB_STATE), lambda i: (i, 0, 0)),
            pl.BlockSpec((1, SLAB_STATE, V7X_LANES), lambda i: (i, 0, 0)),
            pl.BlockSpec((1, SLAB_STATE, V7X_LANES), lambda i: (i, 0, 0)),
            pl.BlockSpec((1, 1, V7X_LANES), lambda i: (i, 0, 0)),
        ],
        out_specs=[
            pl.BlockSpec((1, SLAB_COLS, TILE), lambda i: (i, 0, 0)),
            pl.BlockSpec((1, SLAB_COLS, 2 * SLAB_STATE), lambda i: (i, 0, 0)),
            pl.BlockSpec((1, 2 * SLAB_STATE, SLAB_COLS), lambda i: (i, 0, 0)),
            pl.BlockSpec((1, 2, SLAB_STATE), lambda i: (i, 0, 0)),
        ],
        out_shape=[
            jax.ShapeDtypeStruct((s, SLAB_COLS, TILE), BF16),
            jax.ShapeDtypeStruct((s, SLAB_COLS, 2 * SLAB_STATE), BF16),
            jax.ShapeDtypeStruct((s, 2 * SLAB_STATE, SLAB_COLS), BF16),
            jax.ShapeDtypeStruct((s, 2, SLAB_STATE), F32),
        ],
        scratch_shapes=[pltpu.VMEM((CHUNK, V7X_LANES, V7X_LANES), F32)],
        compiler_params=pltpu.CompilerParams(
            dimension_semantics=("arbitrary",),
            vmem_limit_bytes=_vmem_limit(2 * (SLAB_COLS * TILE + 4 * SLAB_COLS * SLAB_STATE) * 2
                                         + 24 * SLAB_STATE * V7X_LANES * 4)),
        name="ssm_ops",
    )(*operands)


def _glu_kernel(y_ref, w_ref, b_ref, g_ref, o_ref):
    y = y_ref[...]
    z = jnp.dot(y.astype(BF16), w_ref[...], preferred_element_type=F32) + b_ref[...]
    out = y * jax.nn.sigmoid(z)
    o_ref[...] = (out * _rms_scale(out, D_SSM) * g_ref[...]).astype(BF16)


def _glu(y, w_glu, b_glu, g_ssm, tm=512):
    t, d = y.shape
    est = 2 * tm * d * 4 + 2 * tm * d * 2 + 2 * d * d * 2 + 4 * tm * d * 4
    return pl.pallas_call(
        _glu_kernel,
        grid=(t // tm,),
        in_specs=[
            pl.BlockSpec((tm, d), lambda i: (i, 0)),
            pl.BlockSpec((d, d), lambda i: (0, 0)),
            pl.BlockSpec((1, d), lambda i: (0, 0)),
            pl.BlockSpec((1, d), lambda i: (0, 0)),
        ],
        out_specs=pl.BlockSpec((tm, d), lambda i: (i, 0)),
        out_shape=jax.ShapeDtypeStruct((t, d), BF16),
        compiler_params=pltpu.CompilerParams(
            dimension_semantics=("arbitrary",),
            vmem_limit_bytes=_vmem_limit(est)),
        name="glu",
    )(y, w_glu, b_glu, g_ssm)


def _out_proj_kernel(a_ref, s_ref, wa_ref, ws_ref, x_ref, mod_ref, o_ref):
    acc = jnp.dot(a_ref[...], wa_ref[...], preferred_element_type=F32)
    acc = acc + jnp.dot(s_ref[...], ws_ref[...], preferred_element_type=F32)
    o_ref[...] = x_ref[...] + mod_ref[0, 2:3, :] * acc


def _out_proj(attn_n, ssm_n, w_out, x2d, mod3, seq, tm=512, tn=1024):
    t, d = x2d.shape
    nb = seq // tm
    ka = attn_n.shape[1]
    est = 2 * 2 * tm * ka * 2 + 2 * 2 * ka * tn * 2 + 4 * tm * tn * 4 + 2 * tm * tn * 4
    return pl.pallas_call(
        _out_proj_kernel,
        grid=(t // tm, d // tn),
        in_specs=[
            pl.BlockSpec((tm, ka), lambda i, j: (i, 0)),
            pl.BlockSpec((tm, ka), lambda i, j: (i, 0)),
            pl.BlockSpec((ka, tn), lambda i, j: (0, j)),
            pl.BlockSpec((ka, tn), lambda i, j: (1, j)),
            pl.BlockSpec((tm, tn), lambda i, j: (i, j)),
            pl.BlockSpec((1, N_MOD, tn), lambda i, j: (i // nb, 0, j)),
        ],
        out_specs=pl.BlockSpec((tm, tn), lambda i, j: (i, j)),
        out_shape=jax.ShapeDtypeStruct((t, d), F32),
        compiler_params=pltpu.CompilerParams(
            dimension_semantics=("arbitrary", "arbitrary"),
            vmem_limit_bytes=_vmem_limit(est)),
        name="out_proj",
    )(attn_n, ssm_n, w_out, w_out, x2d, mod3)


def _ffn_kernel(x_ref, mod_ref, g2_ref, w1_hbm, w2_hbm, gf_ref, o_ref,
                h_scr, gain_scr, rs_scr, w1_buf, w2_buf, sem, *, tf):
    n_blocks = w1_hbm.shape[1] // tf

    def w_copies(k, slot):
        cols = pl.ds(pl.multiple_of(k * tf, tf), tf)
        return (pltpu.make_async_copy(w1_hbm.at[:, cols], w1_buf.at[slot], sem.at[0, slot]),
                pltpu.make_async_copy(w2_hbm.at[cols, :], w2_buf.at[slot], sem.at[1, slot]))

    def start(k, slot):
        for cp in w_copies(k, slot):
            cp.start()

    def wait(k, slot):
        for cp in w_copies(k, slot):
            cp.wait()

    start(0, 0)
    _modulated_norm(x_ref, g2_ref[...] * (1.0 + mod_ref[0, 4:5, :]), mod_ref[0, 3:4, :], gain_scr, h_scr)
    o_ref[...] = jnp.zeros_like(o_ref)

    def block(k, slot):
        @pl.when(k + 1 < n_blocks)
        def _():
            start(k + 1, (slot + 1) % N_WBUF)

        wait(k, slot)
        a = jnp.dot(h_scr[...], w1_buf[slot], preferred_element_type=F32)
        a = jnp.square(jnp.maximum(a, 0.0)).astype(BF16)
        for n0 in range(0, D_MODEL, tf):
            o_ref[:, n0:n0 + tf] += jnp.dot(a, w2_buf[slot, :, n0:n0 + tf], preferred_element_type=F32)

    def trip(kk, carry):
        for slot in range(N_WBUF):
            block(kk * N_WBUF + slot, slot)
        return carry

    lax.fori_loop(0, n_blocks // N_WBUF, trip, 0)

    gain_scr[0] = jnp.broadcast_to(mod_ref[0, 5:6, :], (V7X_SUBLANES, D_MODEL))
    gain_scr[1] = jnp.broadcast_to(gf_ref[...], (V7X_SUBLANES, D_MODEL))

    def scale_rows(r, carry):
        rows = pl.ds(pl.multiple_of(r * V7X_SUBLANES, V7X_SUBLANES), V7X_SUBLANES)
        x2 = x_ref[rows, :] + gain_scr[0] * o_ref[rows, :]
        rs_scr[rows, :] = jnp.broadcast_to(_rms_scale(x2, D_MODEL), (V7X_SUBLANES, V7X_LANES))
        return carry

    lax.fori_loop(0, o_ref.shape[0] // V7X_SUBLANES, scale_rows, 0, unroll=NORM_UNROLL)

    def rescale_rows(r, carry):
        rows = pl.ds(pl.multiple_of(r * V7X_SUBLANES, V7X_SUBLANES), V7X_SUBLANES)
        x2 = x_ref[rows, :] + gain_scr[0] * o_ref[rows, :]
        rs = jnp.concatenate([rs_scr[rows, :]] * (D_MODEL // V7X_LANES), axis=1)
        o_ref[rows, :] = x2 * rs * gain_scr[1]
        return carry

    lax.fori_loop(0, o_ref.shape[0] // V7X_SUBLANES, rescale_rows, 0, unroll=NORM_UNROLL)


def _ffn(x1, mod3, g2, w1, w2, gf, seq, tm=512, tf=512):
    t, d = x1.shape
    nb = seq // tm
    assert (w1.shape[1] // tf) % N_WBUF == 0 and w2.shape == (w1.shape[1], d)
    est = (2 * tm * d * 4 + 2 * tm * d * 4 + tm * d * 2 + N_WBUF * 2 * d * tf * 2
           + tm * V7X_LANES * 4 + 4 * tm * tf * 4)
    return pl.pallas_call(
        functools.partial(_ffn_kernel, tf=tf),
        grid=(t // tm,),
        in_specs=[
            pl.BlockSpec((tm, d), lambda i: (i, 0)),
            pl.BlockSpec((1, N_MOD, d), lambda i: (i // nb, 0, 0)),
            pl.BlockSpec((1, d), lambda i: (0, 0)),
            pl.BlockSpec(memory_space=pl.ANY),
            pl.BlockSpec(memory_space=pl.ANY),
            pl.BlockSpec((1, d), lambda i: (0, 0)),
        ],
        out_specs=pl.BlockSpec((tm, d), lambda i: (i, 0)),
        out_shape=jax.ShapeDtypeStruct((t, d), F32),
        scratch_shapes=[
            pltpu.VMEM((tm, d), BF16),
            pltpu.VMEM((2, V7X_SUBLANES, d), F32),
            pltpu.VMEM((tm, V7X_LANES), F32),
            pltpu.VMEM((N_WBUF, d, tf), BF16),
            pltpu.VMEM((N_WBUF, tf, d), BF16),
            pltpu.SemaphoreType.DMA((2, N_WBUF)),
        ],
        compiler_params=pltpu.CompilerParams(
            dimension_semantics=("arbitrary",),
            vmem_limit_bytes=_vmem_limit(est)),
        name="ffn",
    )(x1, mod3, g2, w1, w2, gf)


def _rope_tables(seq):
    half = HEAD_DIM // 2
    inv_freq = ROPE_THETA ** (-jnp.arange(half, dtype=F32) / half)
    ang = jnp.arange(seq, dtype=F32)[:, None] * inv_freq[None, :]
    cos = jnp.cos(ang)
    sin = jnp.sin(ang)
    reps = V7X_LANES // HEAD_DIM
    cos_t = jnp.tile(jnp.concatenate([cos, cos], axis=1), (1, reps))
    sin_t = jnp.tile(jnp.concatenate([-sin, sin], axis=1), (1, reps))
    return cos_t, sin_t


def kernel(x, c, w_ada, b_ada, norm1_g, w_in, sinks, ssm_lam_re, ssm_lam_im, ssm_log_step, ssm_b_re, ssm_b_im, ssm_c_re, ssm_c_im, ssm_d, w_glu, b_glu, attn_out_g, ssm_out_g, w_out, norm2_g, w_ff1, w_ff2, final_g):
    bsz, seq, d = x.shape
    t = bsz * seq
    x2d = x.reshape(t, d)

    c_pad = jnp.pad(c, ((0, V7X_SUBLANES - bsz), (0, 0)))
    mod = _adaln(c_pad, w_ada[0], b_ada[0].reshape(1, -1))
    mod3 = mod[:bsz].reshape(bsz, N_MOD, d)

    cos_t, sin_t = _rope_tables(seq)
    q, kvd, u_c, w1b, w2b = _in_proj(x2d, mod3, norm1_g[0].reshape(1, d), w_in[0].astype(BF16), cos_t, sin_t,
                                     w_ff1, w_ff2, seq)

    attn_n = _attention(q, kvd, sinks[0], attn_out_g[0].reshape(1, -1), bsz, seq)

    rev, bpow, cpow, mu = _ssm_operators(ssm_lam_re[0], ssm_lam_im[0], ssm_log_step[0], ssm_b_re[0],
                                         ssm_b_im[0], ssm_c_re[0], ssm_c_im[0], ssm_d[0])
    y = _ssm(u_c, rev, bpow, cpow, mu, bsz, seq)
    ssm_n = _glu(y, w_glu[0].astype(BF16), b_glu[0].reshape(1, -1), ssm_out_g[0].reshape(1, -1))

    x1 = _out_proj(attn_n, ssm_n, w_out[0].astype(BF16), x2d, mod3, seq)
    out = _ffn(x1, mod3, norm2_g[0].reshape(1, d), w1b, w2b, final_g.reshape(1, d), seq)
    return out.reshape(bsz, seq, d)
```

```python
import functools
import math

import jax
import jax.numpy as jnp
from jax import lax
from jax.experimental import pallas as pl
from jax.experimental.pallas import tpu as pltpu

D_MODEL = 4096
D_ATTN = 2048
D_SSM = 2048
HEAD_DIM = 64
N_Q_HEADS = 32
N_KV_HEADS = 4
Q_PER_KV = 8
D_KV = 256
WINDOW = 128
ROPE_THETA = 10000.0
SSM_GROUP = 16
N_SSM_GROUPS = 128
STATE = 64
D_FF = 4 * D_MODEL
N_MOD = 6
EPS = 1e-6

V7X_LANES = 128
V7X_SUBLANES = 8
V7X_VMEM_BYTES = 64 * 1024 * 1024

CHUNK = 16
N_SEG = 4
SSM_BATCH_SPLIT = 2
SLAB_GROUPS = V7X_LANES // SSM_GROUP
N_SLABS = N_SSM_GROUPS // SLAB_GROUPS
SLAB_COLS = CHUNK * V7X_LANES
SLAB_STATE = SLAB_GROUPS * STATE
TILE = 256
STEPS_PER_TILE = TILE // V7X_LANES
SCAN_UNROLL = 8

N_WBUF = 2

BACKGROUND_DMA = 1

BF16 = jnp.bfloat16
F32 = jnp.float32


def _vmem_limit(nbytes):
    return int(min(nbytes + 8 * 1024 * 1024, V7X_VMEM_BYTES - 4 * 1024 * 1024))


def _rms_scale(xf, width):
    return lax.rsqrt(jnp.sum(xf * xf, axis=-1, keepdims=True) * (1.0 / width) + EPS)


NORM_ROWS = 16
NORM_UNROLL = 4


def _modulated_norm(x_ref, gain, shift, mod_scr, out_ref):
    width = x_ref.shape[1]
    mod_scr[0] = jnp.broadcast_to(gain, (V7X_SUBLANES, width))
    mod_scr[1] = jnp.broadcast_to(shift, (V7X_SUBLANES, width))
    reps = NORM_ROWS // V7X_SUBLANES

    def body(r, carry):
        rows = pl.ds(pl.multiple_of(r * NORM_ROWS, NORM_ROWS), NORM_ROWS)
        xf = x_ref[rows, :]
        g = jnp.concatenate([mod_scr[0]] * reps, axis=0)
        b = jnp.concatenate([mod_scr[1]] * reps, axis=0)
        out_ref[rows, :] = (xf * _rms_scale(xf, width) * g + b).astype(out_ref.dtype)
        return carry

    lax.fori_loop(0, x_ref.shape[0] // NORM_ROWS, body, 0, unroll=NORM_UNROLL)


def _adaln_kernel(c_ref, w_ref, b_ref, o_ref):
    c = c_ref[...]
    ca = (c * jax.nn.sigmoid(c)).astype(BF16)
    acc = jnp.dot(ca, w_ref[...].astype(BF16), preferred_element_type=F32)
    o_ref[...] = acc + b_ref[...]


def _adaln(c_pad, w_ada, b_ada, tn=1024):
    m, d = c_pad.shape
    n = w_ada.shape[1]
    return pl.pallas_call(
        _adaln_kernel,
        grid=(n // tn,),
        in_specs=[
            pl.BlockSpec((m, d), lambda j: (0, 0)),
            pl.BlockSpec((d, tn), lambda j: (0, j)),
            pl.BlockSpec((1, tn), lambda j: (0, j)),
        ],
        out_specs=pl.BlockSpec((m, tn), lambda j: (0, j)),
        out_shape=jax.ShapeDtypeStruct((m, n), F32),
        compiler_params=pltpu.CompilerParams(
            dimension_semantics=("arbitrary",),
            vmem_limit_bytes=_vmem_limit(2 * d * tn * 4 + d * tn * 2)),
        name="adaln",
    )(c_pad, w_ada, b_ada)


def _rope(acc, cos, sin_signed):
    width = acc.shape[1]
    lane = lax.broadcasted_iota(jnp.int32, acc.shape, 1)
    first_half = (lane % HEAD_DIM) < (HEAD_DIM // 2)
    partner = jnp.where(first_half,
                        pltpu.roll(acc, width - HEAD_DIM // 2, 1),
                        pltpu.roll(acc, HEAD_DIM // 2, 1))
    reps = width // cos.shape[1]
    cos_w = jnp.concatenate([cos] * reps, axis=1)
    sin_w = jnp.concatenate([sin_signed] * reps, axis=1)
    return acc * cos_w + partner * sin_w


def _in_proj_kernel(x_ref, mod_ref, g_ref, w_hbm, cos_ref, sin_ref, f1_hbm, f2_hbm,
                    q_ref, kv_ref, u_ref, f1b_hbm, f2b_hbm,
                    h_scr, u_scr, gain_scr, w_buf, sem, f1_in, f1_out, f2_in, f2_out, cast_sem, *, tn, n_pieces):
    n_q = D_ATTN // tn
    n_blocks = w_hbm.shape[1] // tn
    slabs_per_blk = tn // V7X_LANES
    chunk_rows = u_ref.shape[1]
    casts = ((f1_hbm, f1b_hbm, f1_in, f1_out), (f2_hbm, f2b_hbm, f2_in, f2_out))

    def w_copy(j, slot):
        return pltpu.make_async_copy(w_hbm.at[:, j * tn:(j + 1) * tn], w_buf.at[slot], sem.at[slot])

    def piece(p, rows):
        first = (pl.program_id(0) * n_pieces + p) * rows
        return pl.ds(pl.multiple_of(first, rows), rows)

    def cast_in(p):
        return [pltpu.make_async_copy(src.at[0, piece(p, land.shape[1]), :], land.at[p % N_WBUF],
                                      cast_sem.at[0, w, p % N_WBUF])
                for w, (src, _, land, _) in enumerate(casts)]

    def cast_out(p):
        return [pltpu.make_async_copy(stage.at[p % N_WBUF], dst.at[piece(p, stage.shape[1]), :],
                                      cast_sem.at[1, w, p % N_WBUF])
                for w, (_, dst, _, stage) in enumerate(casts)]

    def finish(j, acc):
        if j < n_q:
            q = _rope(acc, cos_ref[...], sin_ref[...]) * (HEAD_DIM ** -0.5)
            q_ref[:, j * tn:(j + 1) * tn] = q.astype(BF16)
        elif j == n_q:
            k = _rope(acc[:, :D_KV], cos_ref[...], sin_ref[...])
            kv = jnp.concatenate([k, acc[:, D_KV:]], axis=1)
            lane = lax.broadcasted_iota(jnp.int32, (kv.shape[0], V7X_LANES), 1)
            left = lane < HEAD_DIM
            pieces = []
            for c0 in range(0, 2 * D_KV, V7X_LANES):
                a = kv[:, c0:c0 + V7X_LANES]
                s = pltpu.roll(a, HEAD_DIM, 1)
                pieces.append(jnp.where(left, a, s))
                pieces.append(jnp.where(left, s, a))
            kv_ref[...] = jnp.concatenate(pieces, axis=1).astype(BF16)
        else:
            slab0 = (j - n_q - 1) * slabs_per_blk
            for s in range(slabs_per_blk):
                u_scr[s] = acc[:, s * V7X_LANES:(s + 1) * V7X_LANES]
            for s in range(slabs_per_blk):
                for step in range(CHUNK):
                    rows = u_scr[s, pl.ds(step, chunk_rows, stride=CHUNK), :]
                    u_ref[slab0 + s, :, step * V7X_LANES:(step + 1) * V7X_LANES] = rows.astype(BF16)

    assert N_WBUF <= n_pieces <= n_blocks - 1
    w_copy(0, 0).start()
    for cp in cast_in(0):
        cp.start(priority=BACKGROUND_DMA)
    _modulated_norm(x_ref, g_ref[...] * (1.0 + mod_ref[0, 1:2, :]), mod_ref[0, 0:1, :], gain_scr, h_scr)

    prev = None
    for j in range(n_blocks):
        slot = j % N_WBUF
        if 1 <= j <= n_pieces:
            for cp in cast_out(j - 1):
                cp.start(priority=BACKGROUND_DMA)
        if j + 1 < n_blocks:
            w_copy(j + 1, (j + 1) % N_WBUF).start()
        if j + 1 < n_pieces:
            for cp in cast_in(j + 1):
                cp.start(priority=BACKGROUND_DMA)
        w_copy(j, slot).wait()
        if j < n_pieces:
            for cp in cast_in(j):
                cp.wait()
            if j >= N_WBUF:
                for cp in cast_out(j - N_WBUF):
                    cp.wait()
        acc = jnp.dot(h_scr[...], w_buf[slot], preferred_element_type=F32)
        if prev is not None:
            finish(j - 1, prev)
        if j < n_pieces:
            for _, _, land, stage in casts:
                stage[slot] = land[slot].astype(BF16)
        prev = acc
    finish(n_blocks - 1, prev)
    for p in range(n_pieces - N_WBUF, n_pieces):
        for cp in cast_out(p):
            cp.wait()


CAST_PIECES = 8


def _in_proj(x2d, mod3, g1, w_in, cos_t, sin_t, f1, f2, seq, tm=512, tn=512):
    t, d = x2d.shape
    assert 2 * D_KV == tn and w_in.shape[1] == D_ATTN + tn + D_SSM
    slabs_per_blk = tn // V7X_LANES
    nb = seq // tm
    n_steps = t // tm
    cast_scratch = []
    cast_bytes = 0
    for f in (f1, f2):
        _, rows, cols = f.shape
        assert rows % (n_steps * CAST_PIECES) == 0
        piece_rows = rows // (n_steps * CAST_PIECES)
        assert piece_rows % (2 * V7X_SUBLANES) == 0
        cast_scratch += [pltpu.VMEM((N_WBUF, piece_rows, cols), F32), pltpu.VMEM((N_WBUF, piece_rows, cols), BF16)]
        cast_bytes += N_WBUF * piece_rows * cols * (4 + 2)
    est = (2 * tm * d * 4 + tm * d * 2 + N_WBUF * d * tn * 2 + 2 * tm * D_ATTN * 2
           + 2 * tm * 4 * D_KV * 2 + 2 * tm * D_SSM * 2 + 4 * tm * V7X_LANES * 4
           + slabs_per_blk * tm * V7X_LANES * 4 + 6 * tm * tn * 4 + cast_bytes)
    any_spec = pl.BlockSpec(memory_space=pl.ANY)
    return pl.pallas_call(
        functools.partial(_in_proj_kernel, tn=tn, n_pieces=CAST_PIECES),
        grid=(n_steps,),
        in_specs=[
            pl.BlockSpec((tm, d), lambda i: (i, 0)),
            pl.BlockSpec((1, N_MOD, d), lambda i: (i // nb, 0, 0)),
            pl.BlockSpec((1, d), lambda i: (0, 0)),
            any_spec,
            pl.BlockSpec((tm, V7X_LANES), lambda i: (i % nb, 0)),
            pl.BlockSpec((tm, V7X_LANES), lambda i: (i % nb, 0)),
            any_spec,
            any_spec,
        ],
        out_specs=[
            pl.BlockSpec((tm, D_ATTN), lambda i: (i, 0)),
            pl.BlockSpec((tm, 4 * D_KV), lambda i: (i, 0)),
            pl.BlockSpec((N_SLABS, tm // CHUNK, SLAB_COLS), lambda i: (0, i, 0)),
            any_spec,
            any_spec,
        ],
        out_shape=[
            jax.ShapeDtypeStruct((t, D_ATTN), BF16),
            jax.ShapeDtypeStruct((t, 4 * D_KV), BF16),
            jax.ShapeDtypeStruct((N_SLABS, t // CHUNK, SLAB_COLS), BF16),
            jax.ShapeDtypeStruct(f1.shape[1:], BF16),
            jax.ShapeDtypeStruct(f2.shape[1:], BF16),
        ],
        scratch_shapes=[
            pltpu.VMEM((tm, d), BF16),
            pltpu.VMEM((slabs_per_blk, tm, V7X_LANES), F32),
            pltpu.VMEM((2, V7X_SUBLANES, d), F32),
            pltpu.VMEM((N_WBUF, d, tn), BF16),
            pltpu.SemaphoreType.DMA((N_WBUF,)),
            *cast_scratch,
            pltpu.SemaphoreType.DMA((2, 2, N_WBUF)),
        ],
        compiler_params=pltpu.CompilerParams(
            dimension_semantics=("arbitrary",),
            vmem_limit_bytes=_vmem_limit(est)),
        name="in_proj",
    )(x2d, mod3, g1, w_in, cos_t, sin_t, f1, f2)


def _attn_kernel(sink_ref, q_ref, kvc_ref, kvp_ref, g_ref, o_ref, o_scr, cap_scr):
    n = pl.program_id(1)
    blk = WINDOW
    pair_w = 2 * HEAD_DIM
    n_keys = 2 * blk

    n_pairs = Q_PER_KV // 2
    rows = n_pairs * blk

    @pl.when((pl.program_id(0) == 0) & (n == 0))
    def _():
        qi = lax.broadcasted_iota(jnp.int32, (blk, 2 * n_keys), 0)
        key = lax.broadcasted_iota(jnp.int32, (blk, 2 * n_keys), 1) % n_keys
        rel = qi + blk - key
        band = (rel >= 0) & (rel < WINDOW)
        cap_scr[0] = jnp.where(band & (key >= blk), jnp.inf, F32(-1e30))
        cap_scr[1] = jnp.where(band, jnp.inf, F32(-1e30))

    cap = cap_scr[jnp.minimum(n, 1)]
    pair_of_row = lax.broadcasted_iota(jnp.int32, (rows, 1), 0) // blk

    lane = lax.broadcasted_iota(jnp.int32, (n_keys, pair_w), 1)
    left = lane < HEAD_DIM
    left_o = lax.broadcasted_iota(jnp.int32, (rows, pair_w), 1) < HEAD_DIM
    zero = jnp.zeros((n_keys, pair_w), BF16)

    for h in range(N_KV_HEADS):
        kcol = h * pair_w
        vcol = N_KV_HEADS * pair_w + h * pair_w
        kd = jnp.concatenate([kvp_ref[:, kcol:kcol + pair_w], kvc_ref[:, kcol:kcol + pair_w]], axis=0)
        vd = jnp.concatenate([kvp_ref[:, vcol:vcol + pair_w], kvc_ref[:, vcol:vcol + pair_w]], axis=0)
        k_bd = jnp.concatenate([jnp.where(left, kd, zero), jnp.where(left, zero, kd)], axis=0)
        v_bd = jnp.concatenate([jnp.where(left, vd, zero), jnp.where(left, zero, vd)], axis=0)
        head0 = h * Q_PER_KV
        q4 = jnp.concatenate([q_ref[:, (head0 + 2 * p) * HEAD_DIM:(head0 + 2 * p + 2) * HEAD_DIM]
                              for p in range(n_pairs)], axis=0)
        sink_a = jnp.zeros((rows, 1), F32)
        sink_b = jnp.zeros((rows, 1), F32)
        for p in range(n_pairs):
            sink_a = jnp.where(pair_of_row == p, sink_ref[head0 + 2 * p], sink_a)
            sink_b = jnp.where(pair_of_row == p, sink_ref[head0 + 2 * p + 1], sink_b)
        s = lax.dot_general(q4, k_bd, (((1,), (1,)), ((), ())), preferred_element_type=F32)
        s = jnp.minimum(s.reshape(n_pairs, blk, 2 * n_keys), cap[None]).reshape(rows, 2 * n_keys)
        m_a = jnp.maximum(jnp.max(s[:, :n_keys], axis=-1, keepdims=True), sink_a)
        m_b = jnp.maximum(jnp.max(s[:, n_keys:], axis=-1, keepdims=True), sink_b)
        p_a = jnp.exp(s[:, :n_keys] - m_a)
        p_b = jnp.exp(s[:, n_keys:] - m_b)
        l_a = jnp.sum(p_a, axis=-1, keepdims=True) + jnp.exp(sink_a - m_a)
        l_b = jnp.sum(p_b, axis=-1, keepdims=True) + jnp.exp(sink_b - m_b)
        pp = jnp.concatenate([p_a, p_b], axis=1).astype(BF16)
        o4 = jnp.dot(pp, v_bd, preferred_element_type=F32) * jnp.where(left_o, 1.0 / l_a, 1.0 / l_b)
        for p in range(n_pairs):
            qcol = (head0 + 2 * p) * HEAD_DIM
            o_scr[:, qcol:qcol + pair_w] = o4[p * blk:(p + 1) * blk, :]

    o = o_scr[...]
    o_ref[...] = (o * _rms_scale(o, D_ATTN) * g_ref[...]).astype(BF16)


def _attention(q, kvd, sinks, g_attn, batch, seq):
    t = q.shape[0]
    nb = seq // WINDOW
    kvw = kvd.shape[1]
    return pl.pallas_call(
        _attn_kernel,
        grid=(batch, nb),
        in_specs=[
            pl.BlockSpec(memory_space=pltpu.SMEM),
            pl.BlockSpec((WINDOW, D_ATTN), lambda b, n: (b * nb + n, 0)),
            pl.BlockSpec((WINDOW, kvw), lambda b, n: (b * nb + n, 0)),
            pl.BlockSpec((WINDOW, kvw), lambda b, n: (b * nb + jnp.maximum(n - 1, 0), 0)),
            pl.BlockSpec((1, D_ATTN), lambda b, n: (0, 0)),
        ],
        out_specs=pl.BlockSpec((WINDOW, D_ATTN), lambda b, n: (b * nb + n, 0)),
        out_shape=jax.ShapeDtypeStruct((t, D_ATTN), BF16),
        scratch_shapes=[pltpu.VMEM((WINDOW, D_ATTN), F32), pltpu.VMEM((2, WINDOW, 4 * WINDOW), F32)],
        compiler_params=pltpu.CompilerParams(
            dimension_semantics=("arbitrary", "arbitrary"),
            vmem_limit_bytes=_vmem_limit(8 * 1024 * 1024)),
        name="attention",
    )(sinks, q, kvd, kvd, g_attn)


def _cmul(a_re, a_im, b_re, b_im):
    return a_re * b_re - a_im * b_im, a_re * b_im + a_im * b_re


def _ssm_kernel(u_ref, rev_ref, bpow_ref, cpow_ref, mu_ref, o_ref, z_scr, sp_scr, *, seg_chunks):
    n_k = SLAB_STATE // V7X_LANES
    rows = u_ref.shape[1]
    seqs = rows // seg_chunks
    shape = (seqs, V7X_LANES)

    pitch = z_scr.shape[1] // seqs

    for nb in range(2 * SLAB_STATE // TILE):
        zz = jnp.dot(u_ref[0], bpow_ref[0, :, nb * TILE:(nb + 1) * TILE], preferred_element_type=F32)
        for q in range(seqs):
            src = slice(q * seg_chunks, (q + 1) * seg_chunks)
            dst = slice(q * pitch, q * pitch + seg_chunks)
            z_scr[2 * nb, dst, :] = zz[src, :V7X_LANES]
            z_scr[2 * nb + 1, dst, :] = zz[src, V7X_LANES:]

    mu_re = [jnp.broadcast_to(mu_ref[0, 0:1, k * V7X_LANES:(k + 1) * V7X_LANES], shape) for k in range(n_k)]
    mu_im = [jnp.broadcast_to(mu_ref[0, 1:2, k * V7X_LANES:(k + 1) * V7X_LANES], shape) for k in range(n_k)]

    def rows_at(c):
        return pl.ds(c, seqs, stride=pitch)

    def scan_step(c, carry):
        new = []
        for k in range(n_k):
            s_re, s_im = carry[2 * k], carry[2 * k + 1]
            sp_scr[k, rows_at(c), :] = s_re
            sp_scr[n_k + k, rows_at(c), :] = s_im
            p_re, p_im = _cmul(mu_re[k], mu_im[k], s_re, s_im)
            new += [p_re + z_scr[k, rows_at(c), :], p_im + z_scr[n_k + k, rows_at(c), :]]
        return tuple(new)

    zeros = jnp.zeros(shape, F32)
    final = lax.fori_loop(0, seg_chunks, scan_step, (zeros,) * (2 * n_k), unroll=SCAN_UNROLL)

    first_seg = (lax.broadcasted_iota(jnp.int32, shape, 0) % N_SEG) == 0
    init = []
    for k in range(n_k):
        m_re, m_im = mu_re[k], mu_im[k]
        for _ in range(int(math.log2(seg_chunks))):
            m_re, m_im = _cmul(m_re, m_im, m_re, m_im)
        i_re, i_im = zeros, zeros
        for _ in range(N_SEG - 1):
            t_re, t_im = _cmul(m_re, m_im, i_re, i_im)
            i_re = jnp.where(first_seg, 0.0, pltpu.roll(t_re + final[2 * k], 1, 0))
            i_im = jnp.where(first_seg, 0.0, pltpu.roll(t_im + final[2 * k + 1], 1, 0))
        init += [i_re, i_im]

    def fix_step(c, carry):
        new = []
        for k in range(n_k):
            c_re, c_im = carry[2 * k], carry[2 * k + 1]
            sp_scr[k, rows_at(c), :] += c_re
            sp_scr[n_k + k, rows_at(c), :] += c_im
            new += list(_cmul(mu_re[k], mu_im[k], c_re, c_im))
        return tuple(new)

    lax.fori_loop(0, seg_chunks, fix_step, tuple(init), unroll=SCAN_UNROLL)

    sp = jnp.concatenate(
        [jnp.concatenate([sp_scr[k, q * pitch:q * pitch + seg_chunks, :] for q in range(seqs)], axis=0)
         for k in range(2 * n_k)], axis=1).astype(BF16)
    for jt in range(SLAB_COLS // TILE):
        k_len = (jt + 1) * TILE
        y = jnp.dot(u_ref[0, :, :k_len], rev_ref[0, SLAB_COLS - k_len:, :], preferred_element_type=F32)
        y = y + jnp.dot(sp, cpow_ref[0, :, jt * TILE:(jt + 1) * TILE], preferred_element_type=F32)
        y = 0.5 * y * (1.0 + lax.erf(y * (2.0 ** -0.5)))
        for q in range(STEPS_PER_TILE):
            step = jt * STEPS_PER_TILE + q
            o_ref[pl.ds(step, rows, stride=CHUNK), :] = y[:, q * V7X_LANES:(q + 1) * V7X_LANES]


def _ssm(u_c, rev, bpow, cpow, mu, bsz, seq):
    n_slabs, chunk_rows, _ = u_c.shape
    t = chunk_rows * CHUNK
    rows = chunk_rows // SSM_BATCH_SPLIT
    seg_chunks = seq // (CHUNK * N_SEG)
    assert bsz % SSM_BATCH_SPLIT == 0 and rows // seg_chunks == V7X_SUBLANES
    assert seg_chunks & (seg_chunks - 1) == 0
    assert (seg_chunks // V7X_SUBLANES) % 2 == 0
    scan_rows = V7X_SUBLANES * (seg_chunks + V7X_SUBLANES)
    kern = functools.partial(_ssm_kernel, seg_chunks=seg_chunks)
    est = (2 * rows * SLAB_COLS * 2 + 2 * SLAB_COLS * TILE * 2 + 4 * SLAB_COLS * 2 * SLAB_STATE * 2
           + 2 * rows * CHUNK * V7X_LANES * 4 + 2 * rows * 2 * SLAB_STATE * 4 + 8 * rows * TILE * 4
           + rows * 2 * SLAB_STATE * 2)
    return pl.pallas_call(
        kern,
        grid=(n_slabs, SSM_BATCH_SPLIT),
        in_specs=[
            pl.BlockSpec((1, rows, SLAB_COLS), lambda s, h: (s, h, 0)),
            pl.BlockSpec((1, SLAB_COLS, TILE), lambda s, h: (s, 0, 0)),
            pl.BlockSpec((1, SLAB_COLS, 2 * SLAB_STATE), lambda s, h: (s, 0, 0)),
            pl.BlockSpec((1, 2 * SLAB_STATE, SLAB_COLS), lambda s, h: (s, 0, 0)),
            pl.BlockSpec((1, 2, SLAB_STATE), lambda s, h: (s, 0, 0)),
        ],
        out_specs=pl.BlockSpec((rows * CHUNK, V7X_LANES), lambda s, h: (h, s)),
        out_shape=jax.ShapeDtypeStruct((t, n_slabs * V7X_LANES), F32),
        scratch_shapes=[pltpu.VMEM((2 * SLAB_STATE // V7X_LANES, scan_rows, V7X_LANES), F32),
                        pltpu.VMEM((2 * SLAB_STATE // V7X_LANES, scan_rows, V7X_LANES), F32)],
        compiler_params=pltpu.CompilerParams(
            dimension_semantics=("arbitrary", "arbitrary"),
            vmem_limit_bytes=_vmem_limit(est)),
        name="ssm",
    )(u_c, rev, bpow, cpow, mu)


def _zoh(lam_re, lam_im, log_step):
    step = jnp.exp(log_step)
    mag = jnp.exp(lam_re * step)
    bar_re = mag * jnp.cos(lam_im * step)
    bar_im = mag * jnp.sin(lam_im * step)
    num_re, num_im = bar_re - 1.0, bar_im
    inv_den = 1.0 / (lam_re * lam_re + lam_im * lam_im)
    coef_re = (num_re * lam_re + num_im * lam_im) * inv_den
    coef_im = (num_im * lam_re - num_re * lam_im) * inv_den
    return bar_re, bar_im, coef_re, coef_im


def _ssm_ops_kernel(lam_r_ref, bt_re_ref, bt_im_ref, ct_re_ref, ct_im_ref, d_ref,
                    rev_ref, bpow_ref, cpow_ref, mu_ref, km_scr):
    lanes = V7X_LANES
    bar_re, bar_im, coef_re, coef_im = _zoh(lam_r_ref[0, 0:1, :], lam_r_ref[0, 1:2, :], lam_r_ref[0, 2:3, :])
    bb_re, bb_im = _cmul(bt_re_ref[0], bt_im_ref[0], coef_re, coef_im)
    p_re, p_im = jnp.ones_like(bar_re), jnp.zeros_like(bar_re)
    for jp in range(CHUNK - 1, -1, -1):
        r0 = jp * lanes
        blk_re, blk_im = _cmul(bb_re, bb_im, p_re, p_im)
        bpow_ref[0, r0:r0 + lanes, :SLAB_STATE] = blk_re.astype(BF16)
        bpow_ref[0, r0:r0 + lanes, SLAB_STATE:] = blk_im.astype(BF16)
        p_re, p_im = _cmul(p_re, p_im, bar_re, bar_im)
    mu_ref[0, 0:1, :] = p_re
    mu_ref[0, 1:2, :] = p_im

    cbar_re = jnp.broadcast_to(bar_re, (lanes, SLAB_STATE)).T
    cbar_im = jnp.broadcast_to(bar_im, (lanes, SLAB_STATE)).T
    w_re, w_im = ct_re_ref[0], ct_im_ref[0]
    row_i = lax.broadcasted_iota(jnp.int32, (lanes, lanes), 0)
    col_i = lax.broadcasted_iota(jnp.int32, (lanes, lanes), 1)
    skip = jnp.where(row_i == col_i, d_ref[0], 0.0)
    bb_cat = jnp.concatenate([bb_re, -bb_im], axis=1).astype(BF16)
    for m in range(CHUNK):
        w_cat = jnp.concatenate([w_re, w_im], axis=0).astype(BF16)
        km = jnp.dot(bb_cat, w_cat, preferred_element_type=F32)
        km_scr[m] = km + skip if m == 0 else km
        w_re, w_im = _cmul(w_re, w_im, cbar_re, cbar_im)
        cpow_ref[0, :SLAB_STATE, m * lanes:(m + 1) * lanes] = w_re.astype(BF16)
        cpow_ref[0, SLAB_STATE:, m * lanes:(m + 1) * lanes] = (-w_im).astype(BF16)

    assert STEPS_PER_TILE == 2
    n_tiles = SLAB_COLS // TILE
    for i in range(n_tiles):
        d = n_tiles - 1 - i
        r0 = i * TILE
        below = km_scr[2 * d - 1] if d > 0 else jnp.zeros((lanes, lanes), F32)
        rev_ref[0, r0:r0 + lanes, :lanes] = km_scr[2 * d].astype(BF16)
        rev_ref[0, r0:r0 + lanes, lanes:] = km_scr[2 * d + 1].astype(BF16)
        rev_ref[0, r0 + lanes:r0 + TILE, :lanes] = below.astype(BF16)
        rev_ref[0, r0 + lanes:r0 + TILE, lanes:] = km_scr[2 * d].astype(BF16)


def _ssm_operators(lam_re, lam_im, log_step, b_re, b_im, c_re, c_im, d_skip):
    s, sg = N_SLABS, SLAB_GROUPS
    eye_g = jnp.eye(sg, dtype=F32)

    def per_state(a):
        return a.reshape(s, SLAB_STATE)

    step_gp = jnp.broadcast_to(log_step[:, None], lam_re.shape)
    lam_rows = jnp.stack([per_state(lam_re), per_state(lam_im), per_state(step_gp)], axis=1)

    def block_diag_bt(b):
        bt = b.reshape(s, sg, STATE, SSM_GROUP).transpose(0, 1, 3, 2)
        return (bt[:, :, :, None, :] * eye_g[None, :, None, :, None]).reshape(s, V7X_LANES, SLAB_STATE)

    def block_diag_ct(c):
        ct = c.reshape(s, sg, SSM_GROUP, STATE).transpose(0, 1, 3, 2)
        return (ct[:, :, :, None, :] * eye_g[None, :, None, :, None]).reshape(s, SLAB_STATE, V7X_LANES)

    d_rows = d_skip.reshape(s, 1, V7X_LANES)
    operands = (lam_rows, block_diag_bt(b_re), block_diag_bt(b_im),
                block_diag_ct(c_re), block_diag_ct(c_im), d_rows)
    return pl.pallas_call(
        _ssm_ops_kernel,
        grid=(s,),
        in_specs=[
            pl.BlockSpec((1, 3, SLAB_STATE), lambda i: (i, 0, 0)),
            pl.BlockSpec((1, V7X_LANES, SLAB_STATE), lambda i: (i, 0, 0)),
            pl.BlockSpec((1, V7X_LANES, SLAB_STATE), lambda i: (i, 0, 0)),
            pl.BlockSpec((1, SLAB_STATE, V7X_LANES), lambda i: (i, 0, 0)),
            pl.BlockSpec((1, SLAB_STATE, V7X_LANES), lambda i: (i, 0, 0)),
            pl.BlockSpec((1, 1, V7X_LANES), lambda i: (i, 0, 0)),
        ],
        out_specs=[
            pl.BlockSpec((1, SLAB_COLS, TILE), lambda i: (i, 0, 0)),
            pl.BlockSpec((1, SLAB_COLS, 2 * SLAB_STATE), lambda i: (i, 0, 0)),
            pl.BlockSpec((1, 2 * SLAB_STATE, SLAB_COLS), lambda i: (i, 0, 0)),
            pl.BlockSpec((1, 2, SLAB_STATE), lambda i: (i, 0, 0)),
        ],
        out_shape=[
            jax.ShapeDtypeStruct((s, SLAB_COLS, TILE), BF16),
            jax.ShapeDtypeStruct((s, SLAB_COLS, 2 * SLAB_STATE), BF16),
            jax.ShapeDtypeStruct((s, 2 * SLAB_STATE, SLAB_COLS), BF16),
            jax.ShapeDtypeStruct((s, 2, SLAB_STATE), F32),
        ],
        scratch_shapes=[pltpu.VMEM((CHUNK, V7X_LANES, V7X_LANES), F32)],
        compiler_params=pltpu.CompilerParams(
            dimension_semantics=("arbitrary",),
            vmem_limit_bytes=_vmem_limit(2 * (SLAB_COLS * TILE + 4 * SLAB_COLS * SLAB_STATE) * 2
                                         + 24 * SLAB_STATE * V7X_LANES * 4)),
        name="ssm_ops",
    )(*operands)


def _glu_kernel(y_ref, w_ref, b_ref, g_ref, o_ref):
    y = y_ref[...]
    z = jnp.dot(y.astype(BF16), w_ref[...], preferred_element_type=F32) + b_ref[...]
    out = y * jax.nn.sigmoid(z)
    o_ref[...] = (out * _rms_scale(out, D_SSM) * g_ref[...]).astype(BF16)


def _glu(y, w_glu, b_glu, g_ssm, tm=512):
    t, d = y.shape
    est = 2 * tm * d * 4 + 2 * tm * d * 2 + 2 * d * d * 2 + 4 * tm * d * 4
    return pl.pallas_call(
        _glu_kernel,
        grid=(t // tm,),
        in_specs=[
            pl.BlockSpec((tm, d), lambda i: (i, 0)),
            pl.BlockSpec((d, d), lambda i: (0, 0)),
            pl.BlockSpec((1, d), lambda i: (0, 0)),
            pl.BlockSpec((1, d), lambda i: (0, 0)),
        ],
        out_specs=pl.BlockSpec((tm, d), lambda i: (i, 0)),
        out_shape=jax.ShapeDtypeStruct((t, d), BF16),
        compiler_params=pltpu.CompilerParams(
            dimension_semantics=("arbitrary",),
            vmem_limit_bytes=_vmem_limit(est)),
        name="glu",
    )(y, w_glu, b_glu, g_ssm)


def _out_proj_kernel(a_ref, s_ref, wa_ref, ws_ref, x_ref, mod_ref, o_ref):
    acc = jnp.dot(a_ref[...], wa_ref[...], preferred_element_type=F32)
    acc = acc + jnp.dot(s_ref[...], ws_ref[...], preferred_element_type=F32)
    o_ref[...] = x_ref[...] + mod_ref[0, 2:3, :] * acc


def _out_proj(attn_n, ssm_n, w_out, x2d, mod3, seq, tm=512, tn=1024):
    t, d = x2d.shape
    nb = seq // tm
    ka = attn_n.shape[1]
    est = 2 * 2 * tm * ka * 2 + 2 * 2 * ka * tn * 2 + 4 * tm * tn * 4 + 2 * tm * tn * 4
    return pl.pallas_call(
        _out_proj_kernel,
        grid=(t // tm, d // tn),
        in_specs=[
            pl.BlockSpec((tm, ka), lambda i, j: (i, 0)),
            pl.BlockSpec((tm, ka), lambda i, j: (i, 0)),
            pl.BlockSpec((ka, tn), lambda i, j: (0, j)),
            pl.BlockSpec((ka, tn), lambda i, j: (1, j)),
            pl.BlockSpec((tm, tn), lambda i, j: (i, j)),
            pl.BlockSpec((1, N_MOD, tn), lambda i, j: (i // nb, 0, j)),
        ],
        out_specs=pl.BlockSpec((tm, tn), lambda i, j: (i, j)),
        out_shape=jax.ShapeDtypeStruct((t, d), F32),
        compiler_params=pltpu.CompilerParams(
            dimension_semantics=("arbitrary", "arbitrary"),
            vmem_limit_bytes=_vmem_limit(est)),
        name="out_proj",
    )(attn_n, ssm_n, w_out, w_out, x2d, mod3)


def _ffn_kernel(x_ref, mod_ref, g2_ref, w1_hbm, w2_hbm, gf_ref, o_ref,
                h_scr, gain_scr, rs_scr, w1_buf, w2_buf, sem, *, tf):
    n_blocks = w1_hbm.shape[1] // tf

    def w_copies(k, slot):
        cols = pl.ds(pl.multiple_of(k * tf, tf), tf)
        return (pltpu.make_async_copy(w1_hbm.at[:, cols], w1_buf.at[slot], sem.at[0, slot]),
                pltpu.make_async_copy(w2_hbm.at[cols, :], w2_buf.at[slot], sem.at[1, slot]))

    def start(k, slot):
        for cp in w_copies(k, slot):
            cp.start()

    def wait(k, slot):
        for cp in w_copies(k, slot):
            cp.wait()

    start(0, 0)
    _modulated_norm(x_ref, g2_ref[...] * (1.0 + mod_ref[0, 4:5, :]), mod_ref[0, 3:4, :], gain_scr, h_scr)
    o_ref[...] = jnp.zeros_like(o_ref)

    def block(k, slot):
        @pl.when(k + 1 < n_blocks)
        def _():
            start(k + 1, (slot + 1) % N_WBUF)

        wait(k, slot)
        a = jnp.dot(h_scr[...], w1_buf[slot], preferred_element_type=F32)
        a = jnp.square(jnp.maximum(a, 0.0)).astype(BF16)
        for n0 in range(0, D_MODEL, tf):
            o_ref[:, n0:n0 + tf] += jnp.dot(a, w2_buf[slot, :, n0:n0 + tf], preferred_element_type=F32)

    def trip(kk, carry):
        for slot in range(N_WBUF):
            block(kk * N_WBUF + slot, slot)
        return carry

    lax.fori_loop(0, n_blocks // N_WBUF, trip, 0)

    gain_scr[0] = jnp.broadcast_to(mod_ref[0, 5:6, :], (V7X_SUBLANES, D_MODEL))
    gain_scr[1] = jnp.broadcast_to(gf_ref[...], (V7X_SUBLANES, D_MODEL))

    def scale_rows(r, carry):
        rows = pl.ds(pl.multiple_of(r * V7X_SUBLANES, V7X_SUBLANES), V7X_SUBLANES)
        x2 = x_ref[rows, :] + gain_scr[0] * o_ref[rows, :]
        rs_scr[rows, :] = jnp.broadcast_to(_rms_scale(x2, D_MODEL), (V7X_SUBLANES, V7X_LANES))
        return carry

    lax.fori_loop(0, o_ref.shape[0] // V7X_SUBLANES, scale_rows, 0, unroll=NORM_UNROLL)

    def rescale_rows(r, carry):
        rows = pl.ds(pl.multiple_of(r * V7X_SUBLANES, V7X_SUBLANES), V7X_SUBLANES)
        x2 = x_ref[rows, :] + gain_scr[0] * o_ref[rows, :]
        rs = jnp.concatenate([rs_scr[rows, :]] * (D_MODEL // V7X_LANES), axis=1)
        o_ref[rows, :] = x2 * rs * gain_scr[1]
        return carry

    lax.fori_loop(0, o_ref.shape[0] // V7X_SUBLANES, rescale_rows, 0, unroll=NORM_UNROLL)


def _ffn(x1, mod3, g2, w1, w2, gf, seq, tm=512, tf=512):
    t, d = x1.shape
    nb = seq // tm
    assert (w1.shape[1] // tf) % N_WBUF == 0 and w2.shape == (w1.shape[1], d)
    est = (2 * tm * d * 4 + 2 * tm * d * 4 + tm * d * 2 + N_WBUF * 2 * d * tf * 2
           + tm * V7X_LANES * 4 + 4 * tm * tf * 4)
    return pl.pallas_call(
        functools.partial(_ffn_kernel, tf=tf),
        grid=(t // tm,),
        in_specs=[
            pl.BlockSpec((tm, d), lambda i: (i, 0)),
            pl.BlockSpec((1, N_MOD, d), lambda i: (i // nb, 0, 0)),
            pl.BlockSpec((1, d), lambda i: (0, 0)),
            pl.BlockSpec(memory_space=pl.ANY),
            pl.BlockSpec(memory_space=pl.ANY),
            pl.BlockSpec((1, d), lambda i: (0, 0)),
        ],
        out_specs=pl.BlockSpec((tm, d), lambda i: (i, 0)),
        out_shape=jax.ShapeDtypeStruct((t, d), F32),
        scratch_shapes=[
            pltpu.VMEM((tm, d), BF16),
            pltpu.VMEM((2, V7X_SUBLANES, d), F32),
            pltpu.VMEM((tm, V7X_LANES), F32),
            pltpu.VMEM((N_WBUF, d, tf), BF16),
            pltpu.VMEM((N_WBUF, tf, d), BF16),
            pltpu.SemaphoreType.DMA((2, N_WBUF)),
        ],
        compiler_params=pltpu.CompilerParams(
            dimension_semantics=("arbitrary",),
            vmem_limit_bytes=_vmem_limit(est)),
        name="ffn",
    )(x1, mod3, g2, w1, w2, gf)


def _rope_tables(seq):
    half = HEAD_DIM // 2
    inv_freq = ROPE_THETA ** (-jnp.arange(half, dtype=F32) / half)
    ang = jnp.arange(seq, dtype=F32)[:, None] * inv_freq[None, :]
    cos = jnp.cos(ang)
    sin = jnp.sin(ang)
    reps = V7X_LANES // HEAD_DIM
    cos_t = jnp.tile(jnp.concatenate([cos, cos], axis=1), (1, reps))
    sin_t = jnp.tile(jnp.concatenate([-sin, sin], axis=1), (1, reps))
    return cos_t, sin_t


def kernel(x, c, w_ada, b_ada, norm1_g, w_in, sinks, ssm_lam_re, ssm_lam_im, ssm_log_step, ssm_b_re, ssm_b_im, ssm_c_re, ssm_c_im, ssm_d, w_glu, b_glu, attn_out_g, ssm_out_g, w_out, norm2_g, w_ff1, w_ff2, final_g):
    bsz, seq, d = x.shape
    t = bsz * seq
    x2d = x.reshape(t, d)

    c_pad = jnp.pad(c, ((0, V7X_SUBLANES - bsz), (0, 0)))
    mod = _adaln(c_pad, w_ada[0], b_ada[0].reshape(1, -1))
    mod3 = mod[:bsz].reshape(bsz, N_MOD, d)

    cos_t, sin_t = _rope_tables(seq)
    q, kvd, u_c, w1b, w2b = _in_proj(x2d, mod3, norm1_g[0].reshape(1, d), w_in[0].astype(BF16), cos_t, sin_t,
                                     w_ff1, w_ff2, seq)

    attn_n = _attention(q, kvd, sinks[0], attn_out_g[0].reshape(1, -1), bsz, seq)

    rev, bpow, cpow, mu = _ssm_operators(ssm_lam_re[0], ssm_lam_im[0], ssm_log_step[0], ssm_b_re[0],
                                         ssm_b_im[0], ssm_c_re[0], ssm_c_im[0], ssm_d[0])
    y = _ssm(u_c, rev, bpow, cpow, mu, bsz, seq)
    ssm_n = _glu(y, w_glu[0].astype(BF16), b_glu[0].reshape(1, -1), ssm_out_g[0].reshape(1, -1))

    x1 = _out_proj(attn_n, ssm_n, w_out[0].astype(BF16), x2d, mod3, seq)
    out = _ffn(x1, mod3, norm2_g[0].reshape(1, d), w1b, w2b, final_g.reshape(1, d), seq)
    return out.reshape(bsz, seq, d)
```
